```python
import jax, jax.numpy as jnp
from jax import lax
import numpy as np

D_MODEL = 1024
BATCH = 8
SEQ = 8192
DEPTH = 1
DEC_BATCH = 128
DEC_SEQ = 4
PAST_LEN = 8192
PAGE_SIZE = 128

HEAD_DIM = 64
NSA_HEADS = 8
NSA_KV_HEADS = 2
NSA_GROUP = NSA_HEADS // NSA_KV_HEADS
FOX_HEADS = 8
MIX_WIDTH = (NSA_HEADS + FOX_HEADS) * HEAD_DIM
CMP_STRIDE = 16
CMP_LEN = 2 * CMP_STRIDE
CMP_HIDDEN = 2 * HEAD_DIM
SLC_BLOCK = 64
SLC_TOPK = 16
WINDOW = 512
QBLK = 128
IN_WIDTH = NSA_HEADS * HEAD_DIM + 3 * 2 * NSA_KV_HEADS * HEAD_DIM + 3 * NSA_HEADS + 3 * FOX_HEADS * HEAD_DIM + FOX_HEADS
N_EXPERTS = 64
TOP_K = 8
EXPERT_DIM = 256
SHARED_DIM = 256
ROUTED_SCALE = 2.5
MOE_BLOCK = 128
RMS_EPS = 1e-6
NEG = -1e30
TINY = 1e-30
FORCE = 1e4

kernel_name = 'hymba_nsa_fox_moe_adaln_step'


def rmsnorm(x, g):
    xf = x.astype(jnp.float32)
    y = xf * lax.rsqrt(jnp.mean(xf * xf, axis=-1, keepdims=True) + RMS_EPS)
    return (y * g.astype(jnp.float32)).astype(x.dtype)


def modulation(c, w_ada, b_ada):
    m = jnp.einsum('bd,de->be', jax.nn.silu(c), w_ada) + b_ada
    shift1, scale1, gate1, shift2, scale2, gate2 = jnp.split(m[:, None, :], 6, axis=-1)
    return (shift1, scale1, gate1, shift2, scale2, gate2)


def masked_softmax(s, mask):
    s = jnp.where(mask, s, NEG)
    e = jnp.exp(s - jnp.max(s, axis=-1, keepdims=True)) * mask
    return e / jnp.maximum(jnp.sum(e, axis=-1, keepdims=True), TINY)


def alibi_slopes():
    return jnp.exp2(-8.0 * jnp.arange(1, NSA_HEADS + 1, dtype=jnp.float32) / NSA_HEADS)


def split_projection(h, w_in, b_nsa_gate, b_forget):
    B, T, _ = h.shape
    sizes = [NSA_HEADS * HEAD_DIM, 2 * NSA_KV_HEADS * HEAD_DIM, 2 * NSA_KV_HEADS * HEAD_DIM,
             2 * NSA_KV_HEADS * HEAD_DIM, 3 * NSA_HEADS, FOX_HEADS * HEAD_DIM,
             2 * FOX_HEADS * HEAD_DIM, FOX_HEADS]
    offsets = np.cumsum(sizes)[:-1].tolist()
    p = jnp.einsum('btd,de->bte', h, w_in)
    q_n, kv_c, kv_s, kv_w, g_n, q_f, kv_f, f_f = jnp.split(p, offsets, axis=-1)
    kv_shape = (B, T, 2, NSA_KV_HEADS, HEAD_DIM)
    gates = jax.nn.sigmoid((g_n + b_nsa_gate).astype(jnp.float32)).reshape(B, T, NSA_HEADS, 3)
    logf = jax.nn.log_sigmoid((f_f + b_forget).astype(jnp.float32))
    return (q_n.reshape(B, T, NSA_HEADS, HEAD_DIM), kv_c.reshape(kv_shape), kv_s.reshape(kv_shape),
            kv_w.reshape(kv_shape), gates, q_f.reshape(B, T, FOX_HEADS, HEAD_DIM),
            kv_f.reshape(B, T, 2, FOX_HEADS, HEAD_DIM), logf)


def compress(rows, w1, b1, w2, pe):
    B, L, G, hd = rows.shape
    n_sub = L // CMP_STRIDE
    sub = rows[:, :n_sub * CMP_STRIDE].reshape(B, n_sub, CMP_STRIDE, G, hd)
    sub = sub.transpose(0, 1, 3, 2, 4).reshape(B, n_sub, G, CMP_STRIDE * hd)
    half = CMP_STRIDE * hd
    bias = pe.reshape(-1) @ w1 + b1
    hid = jax.nn.gelu(sub[:, :-1] @ w1[:half] + sub[:, 1:] @ w1[half:] + bias)
    return hid @ w2


def to_blocks(rows):
    B, L, G, hd = rows.shape
    n_slc = -(-L // SLC_BLOCK)
    rows = jnp.pad(rows, ((0, 0), (0, n_slc * SLC_BLOCK - L), (0, 0), (0, 0)))
    return rows.reshape(B, n_slc, SLC_BLOCK, G, hd).transpose(0, 3, 1, 2, 4)


def cmp_to_slc(n_cmp, n_slc):
    start = jnp.arange(n_cmp)[:, None] * CMP_STRIDE
    bstart = jnp.arange(n_slc)[None, :] * SLC_BLOCK
    return ((start < bstart + SLC_BLOCK) & (start + CMP_LEN > bstart)).astype(jnp.float32)


def nsa_attend(q, q_pos, kc, vc, ks_blk, vs_blk, kw, vw, w_pos, gates):
    B, Tq, H, hd = q.shape
    G, Hg = NSA_KV_HEADS, NSA_GROUP
    qg = q.reshape(B, Tq, G, Hg, hd).transpose(0, 2, 3, 1, 4) * (hd ** -0.5)
    slopes = alibi_slopes().reshape(G, Hg, 1, 1)
    n_cmp = kc.shape[1]
    c_end = jnp.arange(n_cmp) * CMP_STRIDE + CMP_LEN - 1
    d_c = q_pos[:, None] - c_end[None, :]
    s_c = jnp.einsum('bghqd,bngd->bghqn', qg, kc).astype(jnp.float32) - slopes * d_c.astype(jnp.float32)
    p_c = masked_softmax(s_c, d_c >= 0)
    o_c = jnp.einsum('bghqn,bngd->bghqd', p_c.astype(vc.dtype), vc)
    n_slc = ks_blk.shape[2]
    imp = jnp.einsum('bghqn,nj->bgqj', p_c, cmp_to_slc(n_cmp, n_slc))
    blk = jnp.arange(n_slc)[None, :]
    cur = (q_pos // SLC_BLOCK)[:, None]
    forced = (blk == 0) | (blk == cur) | (blk == cur - 1)
    valid = blk * SLC_BLOCK <= q_pos[:, None]
    imp = jnp.where(valid, imp + FORCE * forced, NEG)
    n_sel = min(SLC_TOPK, n_slc)
    _, sel = lax.top_k(imp, n_sel)
    bi = jnp.arange(B)[:, None, None, None]
    gi = jnp.arange(G)[None, :, None, None]
    n_key = n_sel * SLC_BLOCK
    k_sel = ks_blk[bi, gi, sel].reshape(B, G, Tq, n_key, hd)
    v_sel = vs_blk[bi, gi, sel].reshape(B, G, Tq, n_key, hd)
    pos = (sel[..., None] * SLC_BLOCK + jnp.arange(SLC_BLOCK)).reshape(B, G, Tq, n_key)
    d_s = (q_pos[:, None] - pos)[:, :, None]
    s_s = jnp.einsum('bghqd,bgqkd->bghqk', qg, k_sel).astype(jnp.float32) - slopes * d_s.astype(jnp.float32)
    p_s = masked_softmax(s_s, d_s >= 0)
    o_s = jnp.einsum('bghqk,bgqkd->bghqd', p_s.astype(v_sel.dtype), v_sel)
    d_w = q_pos[:, None] - w_pos[None, :]
    s_w = jnp.einsum('bghqd,bkgd->bghqk', qg, kw).astype(jnp.float32) - slopes * d_w.astype(jnp.float32)
    p_w = masked_softmax(s_w, (d_w >= 0) & (d_w < WINDOW) & (w_pos[None, :] >= 0))
    o_w = jnp.einsum('bghqk,bkgd->bghqd', p_w.astype(vw.dtype), vw)
    gt = gates.reshape(B, Tq, G, Hg, 3).transpose(0, 2, 3, 1, 4)
    o = gt[..., 0:1] * o_c + gt[..., 1:2] * o_s + gt[..., 2:3] * o_w
    return o.transpose(0, 3, 1, 2, 4).reshape(B, Tq, H * hd).astype(q.dtype)


def fox_logits(q, D_q, q_pos, k, D_k, k_pos):
    s = jnp.einsum('bqhd,bkhd->bhqk', q, k).astype(jnp.float32) * (HEAD_DIM ** -0.5)
    s = s + jnp.transpose(D_q, (0, 2, 1))[..., None] - jnp.transpose(D_k, (0, 2, 1))[:, :, None, :]
    return jnp.where(k_pos[None, :] <= q_pos[:, None], s, NEG)


def mixer_prompt(h, w_in, b_nsa_gate, b_forget, w_cmp1, b_cmp1, w_cmp2, pe_cmp, w_out):
    B, T, _ = h.shape
    q_n, kv_c, kv_s, kv_w, gates, q_f, kv_f, logf = split_projection(h, w_in, b_nsa_gate, b_forget)
    kc = compress(kv_c[:, :, 0], w_cmp1[0], b_cmp1[0], w_cmp2[0], pe_cmp[0])
    vc = compress(kv_c[:, :, 1], w_cmp1[1], b_cmp1[1], w_cmp2[1], pe_cmp[1])
    ks_blk = to_blocks(kv_s[:, :, 0])
    vs_blk = to_blocks(kv_s[:, :, 1])
    kw_pad = jnp.pad(kv_w, ((0, 0), (WINDOW, 0), (0, 0), (0, 0), (0, 0)))
    D = jnp.cumsum(logf, axis=1)
    k_pos = jnp.arange(T)

    def query_block(j):
        t0 = j * QBLK
        q_pos = t0 + jnp.arange(QBLK)

        def rows(a):
            return lax.dynamic_slice_in_dim(a, t0, QBLK, axis=1)

        win = lax.dynamic_slice_in_dim(kw_pad, t0, WINDOW + QBLK, axis=1)
        w_pos = t0 - WINDOW + jnp.arange(WINDOW + QBLK)
        o_n = nsa_attend(rows(q_n), q_pos, kc, vc, ks_blk, vs_blk, win[:, :, 0], win[:, :, 1], w_pos, rows(gates))
        p_f = jax.nn.softmax(fox_logits(rows(q_f), rows(D), q_pos, kv_f[:, :, 0], D, k_pos), axis=-1)
        o_f = jnp.einsum('bhqk,bkhd->bqhd', p_f.astype(kv_f.dtype), kv_f[:, :, 1])
        return jnp.concatenate([o_n, o_f.reshape(B, QBLK, FOX_HEADS * HEAD_DIM).astype(o_n.dtype)], axis=-1)

    o = lax.map(query_block, jnp.arange(T // QBLK))
    o = o.transpose(1, 0, 2, 3).reshape(B, T, MIX_WIDTH)
    y = jnp.einsum('bte,ed->btd', o, w_out)
    win_rows = min(WINDOW, T)
    return (y, kv_c, kv_s, kv_f, logf, kv_w[:, T - win_rows:])


def mixer_sample(h, cache_cmp_kv, cache_slc_kv, cache_fox_kv, cache_fox_logf, state_win_kv, page_table,
                 w_in, b_nsa_gate, b_forget, w_cmp1, b_cmp1, w_cmp2, pe_cmp, w_out):
    B, T, _ = h.shape
    q_n, kv_c, kv_s, kv_w, gates, q_f, kv_f, logf = split_projection(h, w_in, b_nsa_gate, b_forget)
    past = page_table.shape[1] * cache_cmp_kv.shape[1]

    def gather(cache):
        g = cache[page_table]
        return g.reshape((B, past) + cache.shape[2:])

    q_pos = past + jnp.arange(T)
    cmp_all = jnp.concatenate([gather(cache_cmp_kv), kv_c], axis=1)
    slc_all = jnp.concatenate([gather(cache_slc_kv), kv_s], axis=1)
    kc = compress(cmp_all[:, :, 0], w_cmp1[0], b_cmp1[0], w_cmp2[0], pe_cmp[0])
    vc = compress(cmp_all[:, :, 1], w_cmp1[1], b_cmp1[1], w_cmp2[1], pe_cmp[1])
    ks_blk = to_blocks(slc_all[:, :, 0])
    vs_blk = to_blocks(slc_all[:, :, 1])
    win_all = jnp.concatenate([state_win_kv, kv_w], axis=1)
    w_pos = past - state_win_kv.shape[1] + jnp.arange(win_all.shape[1])
    o_n = nsa_attend(q_n, q_pos, kc, vc, ks_blk, vs_blk, win_all[:, :, 0], win_all[:, :, 1], w_pos, gates)
    past_kv = gather(cache_fox_kv)
    D = jnp.cumsum(jnp.concatenate([gather(cache_fox_logf).astype(jnp.float32), logf], axis=1), axis=1)
    D_past, D_new = D[:, :past], D[:, past:]
    s = jnp.concatenate([fox_logits(q_f, D_new, q_pos, past_kv[:, :, 0], D_past, jnp.arange(past)),
                         fox_logits(q_f, D_new, q_pos, kv_f[:, :, 0], D_new, q_pos)], axis=-1)
    p = jax.nn.softmax(s, axis=-1)
    o_f = (jnp.einsum('bhqk,bkhd->bqhd', p[..., :past].astype(past_kv.dtype), past_kv[:, :, 1])
           + jnp.einsum('bhqk,bkhd->bqhd', p[..., past:].astype(kv_f.dtype), kv_f[:, :, 1]))
    o = jnp.concatenate([o_n, o_f.reshape(B, T, FOX_HEADS * HEAD_DIM).astype(o_n.dtype)], axis=-1)
    y = jnp.einsum('bte,ed->btd', o, w_out)
    return (y, kv_c, kv_s, kv_f, logf, win_all[:, T:])


def moe_ffn(h, w_router, b_router, w_exp_gate, w_exp_up, w_exp_down, w_sh_gate, w_sh_up, w_sh_down):
    B, T, D = h.shape
    x = h.reshape(-1, D)
    N = x.shape[0]
    scores = jax.nn.sigmoid(jnp.einsum('nd,de->ne', x, w_router).astype(jnp.float32))
    _, idx = lax.top_k(scores + b_router.astype(jnp.float32), TOP_K)
    g = jnp.take_along_axis(scores, idx, axis=1)
    g = g / jnp.sum(g, axis=-1, keepdims=True) * ROUTED_SCALE
    A = N * TOP_K
    e_flat = idx.reshape(-1)
    order = jnp.argsort(e_flat)
    e_sorted = e_flat[order]
    counts = jnp.bincount(e_flat, length=N_EXPERTS)
    blocks_per = (counts + MOE_BLOCK - 1) // MOE_BLOCK
    blk_end = jnp.cumsum(blocks_per)
    blk_start = blk_end - blocks_per
    tok_start = jnp.cumsum(counts) - counts
    dest = blk_start[e_sorted] * MOE_BLOCK + jnp.arange(A) - tok_start[e_sorted]
    n_blk = -(-A // MOE_BLOCK) + N_EXPERTS
    slot_row = jnp.full((n_blk * MOE_BLOCK,), N, jnp.int32).at[dest].set((order // TOP_K).astype(jnp.int32))
    slot_gate = jnp.zeros((n_blk * MOE_BLOCK,), jnp.float32).at[dest].set(g.reshape(-1)[order])
    blk_expert = jnp.minimum(jnp.searchsorted(blk_end, jnp.arange(n_blk), side='right'), N_EXPERTS - 1)
    x_pad = jnp.concatenate([x, jnp.zeros((1, D), x.dtype)], axis=0)

    def run_block(args):
        r, e = args
        xb = x_pad[r]
        hb = jax.nn.silu(xb @ w_exp_gate[e]) * (xb @ w_exp_up[e])
        return hb @ w_exp_down[e]

    out = lax.map(run_block, (slot_row.reshape(n_blk, MOE_BLOCK), blk_expert)).reshape(-1, D)
    out = out * slot_gate[:, None].astype(out.dtype)
    routed = jax.ops.segment_sum(out, slot_row, num_segments=N + 1)[:N]
    shared = (jax.nn.silu(x @ w_sh_gate) * (x @ w_sh_up)) @ w_sh_down
    return (routed + shared).reshape(B, T, D)


def sandwich_layer(x, mod, mixer_fn, norms, moe_w):
    shift1, scale1, gate1, shift2, scale2, gate2 = mod
    g_pre_mix, g_post_mix, g_pre_ffn, g_post_ffn = norms
    h = rmsnorm(x, g_pre_mix) * (1 + scale1) + shift1
    mix, *states = mixer_fn(h)
    x = x + gate1 * rmsnorm(mix, g_post_mix)
    h = rmsnorm(x, g_pre_ffn) * (1 + scale2) + shift2
    x = x + gate2 * rmsnorm(moe_ffn(h, *moe_w), g_post_ffn)
    return x, states


def setup_inputs(seed: int = 0) -> dict:
    key = jax.random.key(seed)
    ks = jax.random.split(key, 32)
    n_pages = PAST_LEN // PAGE_SIZE
    n_phys = (DEC_BATCH * n_pages * 5) // 4
    win_buf = min(WINDOW, PAST_LEN)
    D = D_MODEL

    def nrm(k, shape, scale=1.0):
        return jax.random.normal(k, shape, jnp.float32) * scale

    page_table = jax.random.permutation(ks[9], n_phys)[:DEC_BATCH * n_pages].reshape(DEC_BATCH, n_pages).astype(jnp.int32)
    return {
        'x_prompt': nrm(ks[0], (BATCH, SEQ, D)),
        'x_sample': nrm(ks[1], (DEC_BATCH, DEC_SEQ, D)),
        'c_prompt': nrm(ks[2], (BATCH, D)),
        'c_sample': nrm(ks[3], (DEC_BATCH, D)),
        'cache_cmp_kv': nrm(ks[4], (DEPTH, n_phys, PAGE_SIZE, 2, NSA_KV_HEADS, HEAD_DIM)),
        'cache_slc_kv': nrm(ks[5], (DEPTH, n_phys, PAGE_SIZE, 2, NSA_KV_HEADS, HEAD_DIM)),
        'cache_fox_kv': nrm(ks[6], (DEPTH, n_phys, PAGE_SIZE, 2, FOX_HEADS, HEAD_DIM)),
        'cache_fox_logf': jax.nn.log_sigmoid(2.0 + nrm(ks[7], (DEPTH, n_phys, PAGE_SIZE, FOX_HEADS))),
        'state_win_kv': nrm(ks[8], (DEPTH, DEC_BATCH, win_buf, 2, NSA_KV_HEADS, HEAD_DIM)),
        'page_table': page_table,
        'w_ada': nrm(ks[10], (DEPTH, D, 6 * D), 0.5 * D ** -0.5),
        'b_ada': nrm(ks[11], (DEPTH, 6 * D), 0.01),
        'g_pre_mix': 1.0 + nrm(ks[12], (DEPTH, D), 0.1),
        'g_post_mix': 1.0 + nrm(ks[13], (DEPTH, D), 0.1),
        'g_pre_ffn': 1.0 + nrm(ks[14], (DEPTH, D), 0.1),
        'g_post_ffn': 1.0 + nrm(ks[15], (DEPTH, D), 0.1),
        'w_in': nrm(ks[16], (DEPTH, D, IN_WIDTH), D ** -0.5),
        'b_nsa_gate': nrm(ks[17], (DEPTH, 3 * NSA_HEADS), 0.1),
        'b_forget': 2.0 + nrm(ks[18], (DEPTH, FOX_HEADS), 0.5),
        'w_cmp1': nrm(ks[19], (DEPTH, 2, CMP_LEN * HEAD_DIM, CMP_HIDDEN), (CMP_LEN * HEAD_DIM) ** -0.5),
        'b_cmp1': nrm(ks[20], (DEPTH, 2, CMP_HIDDEN), 0.01),
        'w_cmp2': nrm(ks[21], (DEPTH, 2, CMP_HIDDEN, HEAD_DIM), CMP_HIDDEN ** -0.5),
        'pe_cmp': nrm(ks[22], (DEPTH, 2, CMP_LEN, HEAD_DIM), 0.5),
        'w_out': nrm(ks[23], (DEPTH, MIX_WIDTH, D), MIX_WIDTH ** -0.5),
        'w_router': nrm(ks[24], (DEPTH, D, N_EXPERTS), D ** -0.5),
        'b_router': nrm(ks[25], (DEPTH, N_EXPERTS), 0.01),
        'w_exp_gate': nrm(ks[26], (DEPTH, N_EXPERTS, D, EXPERT_DIM), D ** -0.5),
        'w_exp_up': nrm(ks[27], (DEPTH, N_EXPERTS, D, EXPERT_DIM), D ** -0.5),
        'w_exp_down': nrm(ks[28], (DEPTH, N_EXPERTS, EXPERT_DIM, D), EXPERT_DIM ** -0.5),
        'w_sh_gate': nrm(ks[29], (DEPTH, D, SHARED_DIM), D ** -0.5),
        'w_sh_up': nrm(ks[30], (DEPTH, D, SHARED_DIM), D ** -0.5),
        'w_sh_down': nrm(ks[31], (DEPTH, SHARED_DIM, D), SHARED_DIM ** -0.5),
    }


def reference(x_prompt, x_sample, c_prompt, c_sample, cache_cmp_kv, cache_slc_kv, cache_fox_kv,
              cache_fox_logf, state_win_kv, page_table, w_ada, b_ada, g_pre_mix, g_post_mix, g_pre_ffn,
              g_post_ffn, w_in, b_nsa_gate, b_forget, w_cmp1, b_cmp1, w_cmp2, pe_cmp, w_out, w_router,
              b_router, w_exp_gate, w_exp_up, w_exp_down, w_sh_gate, w_sh_up, w_sh_down):
    xp, xs = x_prompt, x_sample
    st_p = ([], [], [], [], [])
    st_s = ([], [], [], [], [])
    for l in range(DEPTH):
        mix_w = (w_in[l], b_nsa_gate[l], b_forget[l], w_cmp1[l], b_cmp1[l], w_cmp2[l], pe_cmp[l], w_out[l])
        moe_w = (w_router[l], b_router[l], w_exp_gate[l], w_exp_up[l], w_exp_down[l],
                 w_sh_gate[l], w_sh_up[l], w_sh_down[l])
        norms = (g_pre_mix[l], g_post_mix[l], g_pre_ffn[l], g_post_ffn[l])
        xp, sp = sandwich_layer(xp, modulation(c_prompt, w_ada[l], b_ada[l]),
                                lambda h: mixer_prompt(h, *mix_w), norms, moe_w)
        caches = (cache_cmp_kv[l], cache_slc_kv[l], cache_fox_kv[l], cache_fox_logf[l], state_win_kv[l], page_table)
        xs, ss = sandwich_layer(xs, modulation(c_sample, w_ada[l], b_ada[l]),
                                lambda h: mixer_sample(h, *caches, *mix_w), norms, moe_w)
        for lst, a in zip(st_p, sp):
            lst.append(a)
        for lst, a in zip(st_s, ss):
            lst.append(a)
    return (xp, xs,
            jnp.stack(st_p[0]), jnp.stack(st_p[1]), jnp.stack(st_p[2]), jnp.stack(st_p[3]), jnp.stack(st_p[4]),
            jnp.stack(st_s[0]), jnp.stack(st_s[1]), jnp.stack(st_s[2]), jnp.stack(st_s[3]), jnp.stack(st_s[4]))
```

```python
import functools

import numpy as np
import jax
import jax.numpy as jnp
from jax import lax
from jax.experimental import pallas as pl
from jax.experimental.pallas import tpu as pltpu

F32 = jnp.float32
BF16 = jnp.bfloat16

HD = 64
NSA_H = 8
NSA_G = 2
HG = NSA_H // NSA_G
FOX_H = 8
CMP_STRIDE = 16
CMP_LEN = 32
CMP_HID = 128
SLC_BLOCK = 64
SLC_TOPK = 16
WINDOW = 512
TOP_K = 8
ROUTED_SCALE = 2.5
RMS_EPS = 1e-6
NEG = -1e30
TINY = 1e-30
FORCE = 1e4
NEG_INF = float("-inf")

LANES = 128
KVW = 2 * NSA_G * HD
KVF = 2 * FOX_H * HD
QW = NSA_H * HD
VMEM_LIMIT = 56 * 1024 * 1024


def _cp(*sem):
    return pltpu.CompilerParams(dimension_semantics=sem, vmem_limit_bytes=VMEM_LIMIT)


def _rms(x, g):
    return x * lax.rsqrt(jnp.mean(x * x, axis=-1, keepdims=True) + RMS_EPS) * g


def _dot(a, b):
    return jnp.dot(a, b, preferred_element_type=F32)


def _dot_nt(a, b):
    return lax.dot_general(a, b, (((1,), (1,)), ((), ())), preferred_element_type=F32)


def _iota(shape, dim):
    return lax.broadcasted_iota(jnp.int32, shape, dim)


def _onehot(cond):
    return jnp.where(cond, 1.0, 0.0).astype(BF16)


def _pick_head(width_in, h_off):
    r = _iota((width_in, LANES), 0)
    c = _iota((width_in, LANES), 1)
    return _onehot((r == c + h_off) & (c < HD))


def _place_head(width_out, off):
    r = _iota((LANES, width_out), 0)
    c = _iota((LANES, width_out), 1)
    return _onehot((r >= HD) & (c == r - HD + off))


def _split3(x):
    hi = x.astype(BF16)
    r1 = x - hi.astype(F32)
    mid = r1.astype(BF16)
    lo = (r1 - mid.astype(F32)).astype(BF16)
    return hi, mid, lo


def _mod_kernel(c_ref, w_ref, b_ref, o_ref):
    c = c_ref[...]
    a = (c * jax.nn.sigmoid(c)).astype(BF16)
    o_ref[...] = _dot(a, w_ref[...].astype(BF16)) + b_ref[...]


def _modulation(c_all, w_ada, b_ada):
    nb, d = c_all.shape
    n_chunk = w_ada.shape[1] // d
    return pl.pallas_call(
        _mod_kernel,
        grid=(n_chunk,),
        in_specs=[pl.BlockSpec((nb, d), lambda i: (0, 0)),
                  pl.BlockSpec((d, d), lambda i: (0, i)),
                  pl.BlockSpec((1, d), lambda i: (0, i))],
        out_specs=pl.BlockSpec((nb, d), lambda i: (0, i)),
        out_shape=jax.ShapeDtypeStruct((nb, w_ada.shape[1]), F32),
        compiler_params=_cp("parallel"),
        name="modulation",
    )(c_all, w_ada, b_ada.reshape(1, -1))


_IN_F32_SEGS = (("kvc", KVW), ("kvs", KVW), ("kvw", KVW), ("kvf", KVF), ("small", LANES))
_IN_BF16_SEGS = (("qn", QW), ("qf", QW), ("kvs_b", KVW), ("kvw_b", KVW), ("kvf_b", KVF))
_IN_COLS = sum(w for _, w in _IN_F32_SEGS + _IN_BF16_SEGS)
N_GATE = 3 * NSA_H


def _in_kernel(x_ref, sc_ref, sh_ref, g_ref, w_ref, bias_ref, *out_refs):
    x = x_ref[...]
    h = _rms(x, g_ref[...]) * (1.0 + sc_ref[...]) + sh_ref[...]
    hb = h.astype(BF16)
    off = 0
    for (name, width), o_ref in zip(_IN_F32_SEGS + _IN_BF16_SEGS, out_refs):
        p = _dot(hb, w_ref[:, off:off + width])
        if name == "small":
            z = p + bias_ref[...]
            lane = _iota(z.shape, 1)
            logsig = jnp.minimum(z, 0.0) - jnp.log1p(jnp.exp(-jnp.abs(z)))
            p = jnp.where(lane < N_GATE, jax.nn.sigmoid(z), logsig)
        o_ref[...] = p.astype(o_ref.dtype)
        off += width


def _in_weights(w_in, b_nsa_gate, b_forget):
    d = w_in.shape[0]
    o_qn, o_kvc, o_kvs, o_kvw = 0, QW, QW + KVW, QW + 2 * KVW
    o_gn = QW + 3 * KVW
    o_qf = o_gn + N_GATE
    o_kvf = o_qf + QW
    o_ff = o_kvf + KVF
    nsa_perm = np.arange(KVW).reshape(2, NSA_G, HD).transpose(1, 0, 2).reshape(-1)
    fox_perm = np.arange(KVF).reshape(2, FOX_H, HD).transpose(1, 0, 2).reshape(-1)
    small = jnp.concatenate([w_in[:, o_gn:o_gn + N_GATE], w_in[:, o_ff:o_ff + FOX_H],
                             jnp.zeros((d, LANES - N_GATE - FOX_H), F32)], axis=1)
    scale = HD ** -0.5
    w_all = jnp.concatenate([
        w_in[:, o_kvc:o_kvc + KVW], w_in[:, o_kvs:o_kvs + KVW], w_in[:, o_kvw:o_kvw + KVW],
        w_in[:, o_kvf:o_kvf + KVF], small,
        w_in[:, o_qn:o_qn + QW] * scale, w_in[:, o_qf:o_qf + QW] * scale,
        w_in[:, o_kvs:o_kvs + KVW][:, nsa_perm], w_in[:, o_kvw:o_kvw + KVW][:, nsa_perm],
        w_in[:, o_kvf:o_kvf + KVF][:, fox_perm]], axis=1).astype(BF16)
    bias = jnp.concatenate([b_nsa_gate, b_forget, jnp.zeros((LANES - N_GATE - FOX_H,), F32)]).reshape(1, LANES)
    return w_all, bias


def _in_proj(x2, scale, shift, g_pre, w_all, bias, rows_per_mod):
    n, d = x2.shape
    tm = min(256, n)
    if rows_per_mod >= tm:
        per = rows_per_mod // tm
        mod_spec = pl.BlockSpec((None, 1, d), lambda i: (i // per, 0, 0))
    else:
        scale = jnp.repeat(scale[:, 0, :], rows_per_mod, axis=0)
        shift = jnp.repeat(shift[:, 0, :], rows_per_mod, axis=0)
        mod_spec = pl.BlockSpec((tm, d), lambda i: (i, 0))
    segs = _IN_F32_SEGS + _IN_BF16_SEGS
    dts = [F32] * len(_IN_F32_SEGS) + [BF16] * len(_IN_BF16_SEGS)
    outs = pl.pallas_call(
        _in_kernel,
        grid=(n // tm,),
        in_specs=[pl.BlockSpec((tm, d), lambda i: (i, 0)), mod_spec, mod_spec,
                  pl.BlockSpec((1, d), lambda i: (0, 0)),
                  pl.BlockSpec((d, _IN_COLS), lambda i: (0, 0)),
                  pl.BlockSpec((1, LANES), lambda i: (0, 0))],
        out_specs=[pl.BlockSpec((tm, w), lambda i: (i, 0)) for _, w in segs],
        out_shape=[jax.ShapeDtypeStruct((n, w), dt) for (_, w), dt in zip(segs, dts)],
        compiler_params=_cp("parallel"),
        name="in_proj",
    )(x2, scale, shift, g_pre.reshape(1, d), w_all, bias)
    return dict(zip([s for s, _ in segs], outs))


def _cmp_kernel(subk_ref, subv_ref, w1_ref, pe_ref, b1_ref, w2_ref, o_ref):
    half = CMP_STRIDE * HD
    out = None
    for kv, sub_ref in enumerate((subk_ref, subv_ref)):
        sub = sub_ref[0, 0]
        w1 = w1_ref[kv]
        a = _dot(sub, w1[:half])
        b = _dot(sub, w1[half:])
        n_sub = a.shape[0]
        b_next = pltpu.roll(b, n_sub - 1, 0)
        bias = _dot(pe_ref[kv], w1)[0:1] + b1_ref[kv]
        hid = jax.nn.gelu(a + b_next + bias).astype(BF16)
        term = _dot(hid, w2_ref[kv])
        out = term if out is None else out + term
    o_ref[0, 0] = out.astype(o_ref.dtype)


def _cmp_weights(w_cmp1, b_cmp1, w_cmp2, pe_cmp):
    w1 = w_cmp1.astype(BF16)
    pe = jnp.broadcast_to(pe_cmp.reshape(2, 1, CMP_LEN * HD), (2, 8, CMP_LEN * HD)).astype(BF16)
    b1 = b_cmp1.reshape(2, 1, CMP_HID)
    z = jnp.zeros((CMP_HID, HD), F32)
    w2 = jnp.stack([jnp.concatenate([w_cmp2[0], z], axis=1),
                    jnp.concatenate([z, w_cmp2[1]], axis=1)]).astype(BF16)
    return w1, pe, b1, w2


def _compress_prompt(kvc, B, T, cw):
    w1, pe, b1, w2 = cw
    n_sub = T // CMP_STRIDE
    sub = kvc.reshape(B, n_sub, CMP_STRIDE, 2 * NSA_G, HD).transpose(0, 3, 1, 2, 4)
    sub = sub.reshape(B, 2 * NSA_G, n_sub, CMP_STRIDE * HD).astype(BF16)
    feat = CMP_STRIDE * HD
    return pl.pallas_call(
        _cmp_kernel,
        grid=(B, NSA_G),
        in_specs=[pl.BlockSpec((1, 1, n_sub, feat), lambda b, g: (b, g, 0, 0)),
                  pl.BlockSpec((1, 1, n_sub, feat), lambda b, g: (b, NSA_G + g, 0, 0)),
                  pl.BlockSpec(w1.shape, lambda b, g: (0, 0, 0)),
                  pl.BlockSpec(pe.shape, lambda b, g: (0, 0, 0)),
                  pl.BlockSpec(b1.shape, lambda b, g: (0, 0, 0)),
                  pl.BlockSpec(w2.shape, lambda b, g: (0, 0, 0))],
        out_specs=pl.BlockSpec((1, 1, n_sub, LANES), lambda b, g: (b, g, 0, 0)),
        out_shape=jax.ShapeDtypeStruct((B, NSA_G, n_sub, LANES), BF16),
        compiler_params=_cp("parallel", "parallel"),
        name="compress_prompt",
    )(sub, sub, w1, pe, b1, w2)


def _cumsum_block(x):
    n = x.shape[1]
    u = _onehot(_iota((n, n), 0) <= _iota((n, n), 1))
    hi, mid, lo = _split3(x)
    return _dot(hi, u) + _dot(mid, u) + _dot(lo, u)


def _cumsum_kernel(x_ref, o_ref, carry_ref):
    @pl.when(pl.program_id(1) == 0)
    def _():
        carry_ref[...] = jnp.zeros_like(carry_ref)

    d = _cumsum_block(x_ref[0]) + carry_ref[:, 0:1]
    o_ref[0] = d
    carry_ref[...] = jnp.broadcast_to(d[:, d.shape[1] - 1:], carry_ref.shape)


def _cumsum_prompt(logf_t):
    B, H, T = logf_t.shape
    tc = min(512, T)
    return pl.pallas_call(
        _cumsum_kernel,
        grid=(B, T // tc),
        in_specs=[pl.BlockSpec((1, H, tc), lambda b, c: (b, 0, c))],
        out_specs=pl.BlockSpec((1, H, tc), lambda b, c: (b, 0, c)),
        out_shape=jax.ShapeDtypeStruct((B, H, T), F32),
        scratch_shapes=[pltpu.VMEM((H, LANES), F32)],
        compiler_params=_cp("parallel", "arbitrary"),
        name="cumsum_prompt",
    )(logf_t)


def _topk_mask(val, n_take):
    lane_f = _iota(val.shape, 1).astype(F32)
    sel = jnp.zeros(val.shape, F32)
    for _ in range(n_take):
        m = jnp.max(val, axis=1, keepdims=True)
        first = jnp.min(jnp.where(val == m, lane_f, float(LANES)), axis=1, keepdims=True)
        pick = lane_f == first
        sel = jnp.where(pick, 1.0, sel)
        val = jnp.where(pick, NEG_INF, val)
    return sel


def _masked_softmax(s, mask):
    s = jnp.where(mask, s, NEG)
    m = jnp.max(s, axis=1, keepdims=True)
    e = jnp.where(mask, jnp.exp(s - m), 0.0)
    return e / jnp.maximum(jnp.sum(e, axis=1, keepdims=True), TINY)


def _cmp_to_slc_np(n_slots, first_token):
    tok = np.arange(n_slots)[:, None] + first_token
    start = tok * CMP_STRIDE
    bstart = np.arange(LANES)[None, :] * SLC_BLOCK
    m = (start < bstart + SLC_BLOCK) & (start + CMP_LEN > bstart) & (tok >= 0)
    return m.astype(np.float32)


def _nsa_sel_kernel(slopes_ref, q_ref, cmp_ref, c2s_ref, oc_ref, sel_ref, *, tq, n_sel):
    g = pl.program_id(1)
    t0 = pl.program_id(2) * tq
    n_slots = cmp_ref.shape[2]
    qpos = t0 + _iota((tq, 1), 0)
    cend = _iota((1, n_slots), 1) * CMP_STRIDE + (CMP_LEN - 1)
    d_c = qpos - cend
    mask_c = d_c >= 0
    d_cf = d_c.astype(F32)
    q = q_ref[...]
    cmp = cmp_ref[0, 0]
    c2s = c2s_ref[...]
    imp = jnp.zeros((tq, LANES), F32)
    for h in range(HG):
        q128 = _dot(q, _pick_head(HG * HD, h * HD)).astype(BF16)
        s = _dot_nt(q128, cmp) - slopes_ref[g * HG + h] * d_cf
        pb = _masked_softmax(s, mask_c).astype(BF16)
        oc_ref[0, 0, h] = _dot(pb, cmp)
        imp = imp + _dot(pb, c2s)
    blk = _iota((tq, LANES), 1)
    cur = qpos // SLC_BLOCK
    forced = (blk == 0) | (blk == cur) | (blk == cur - 1)
    valid = blk * SLC_BLOCK <= qpos
    val = jnp.where(valid, imp + jnp.where(forced, FORCE, 0.0), NEG)
    sel_ref[0, 0] = _topk_mask(val, n_sel).astype(sel_ref.dtype)


def _nsa_select(slopes, qn, cmp, B, T):
    tq = min(128, T)
    nq = T // tq
    n_slots = cmp.shape[2]
    n_slc = -(-T // SLC_BLOCK)
    assert n_slc <= LANES
    c2s = jnp.asarray(_cmp_to_slc_np(n_slots, 0), BF16)
    return pl.pallas_call(
        functools.partial(_nsa_sel_kernel, tq=tq, n_sel=min(SLC_TOPK, n_slc)),
        grid=(B, NSA_G, nq),
        in_specs=[pl.BlockSpec(memory_space=pltpu.SMEM),
                  pl.BlockSpec((tq, HG * HD), lambda b, g, j: (b * nq + j, g)),
                  pl.BlockSpec((1, 1, n_slots, LANES), lambda b, g, j: (b, g, 0, 0)),
                  pl.BlockSpec((n_slots, LANES), lambda b, g, j: (0, 0))],
        out_specs=[pl.BlockSpec((1, 1, HG, tq, LANES), lambda b, g, j: (b, g, 0, j, 0)),
                   pl.BlockSpec((1, 1, tq, LANES), lambda b, g, j: (b, g, j, 0))],
        out_shape=[jax.ShapeDtypeStruct((B, NSA_G, HG, T, LANES), F32),
                   jax.ShapeDtypeStruct((B, NSA_G, T, LANES), BF16)],
        compiler_params=_cp("parallel", "parallel", "parallel"),
        name="nsa_select",
    )(slopes, qn, cmp, c2s)


def _nsa_attn_kernel(slopes_ref, q_ref, kvs_ref, kvw_ref, sel_ref, oc_ref, gates_ref, o_ref,
                     m_s, l_s, acc_s, *, tq, tk, wk):
    g = pl.program_id(1)
    t0 = pl.program_id(2) * tq
    q = q_ref[...]
    q4 = jnp.concatenate([_dot(q, _pick_head(HG * HD, h * HD)).astype(BF16) for h in range(HG)], axis=0)
    qpos = t0 + _iota((tq, 1), 0)
    sel = sel_ref[0, 0]
    blk_i = _iota((LANES, 1), 0)
    slopes = [slopes_ref[g * HG + h] for h in range(HG)]

    m_s[...] = jnp.full(m_s.shape, NEG, F32)
    l_s[...] = jnp.zeros(l_s.shape, F32)
    acc_s[...] = jnp.zeros(acc_s.shape, F32)

    def body(kt, carry):
        k0 = pl.multiple_of(kt * tk, tk)
        kpos = k0 + _iota((1, tk), 1)
        expand = _onehot(blk_i == (kpos >> 6))
        m01 = _dot(sel, expand)
        dist = qpos - kpos
        ok = (m01 > 0.5) & (dist >= 0)
        n_ok = jnp.max(jnp.where(ok, 1.0, 0.0))

        @pl.when(n_ok > 0.0)
        def _():
            kv = kvs_ref[pl.ds(k0, tk), :]
            s = _dot_nt(q4, kv)
            distf = dist.astype(F32)
            for h in range(HG):
                r = slice(h * tq, (h + 1) * tq)
                sh = jnp.where(ok, s[r] - slopes[h] * distf, NEG)
                m_old = m_s[r]
                m_new = jnp.maximum(m_old, jnp.max(sh, axis=1, keepdims=True))
                alpha = jnp.exp(m_old - m_new)
                e = jnp.where(ok, jnp.exp(sh - m_new), 0.0)
                l_s[r] = alpha * l_s[r] + jnp.sum(e, axis=1, keepdims=True)
                acc_s[r] = alpha * acc_s[r] + _dot(e.astype(BF16), kv)
                m_s[r] = m_new
        return carry

    lax.fori_loop(0, (t0 + tq + tk - 1) // tk, body, 0)

    w0 = pl.multiple_of(jnp.maximum(t0 + tq - wk, 0), tq)
    kvw = kvw_ref[pl.ds(w0, wk), :]
    d_w = qpos - (w0 + _iota((1, wk), 1))
    mask_w = (d_w >= 0) & (d_w < WINDOW)
    d_wf = d_w.astype(F32)
    s_w = _dot_nt(q4, kvw)
    gates = gates_ref[0]
    out = jnp.zeros(o_ref.shape, F32)
    for h in range(HG):
        r = slice(h * tq, (h + 1) * tq)
        p_w = _masked_softmax(s_w[r] - slopes[h] * d_wf, mask_w)
        o_w = _dot(p_w.astype(BF16), kvw)
        o_s = acc_s[r] / jnp.maximum(l_s[r], TINY)
        o_h = (gates[:, 3 * h:3 * h + 1] * oc_ref[0, 0, h] + gates[:, 3 * h + 1:3 * h + 2] * o_s
               + gates[:, 3 * h + 2:3 * h + 3] * o_w)
        out = out + _dot(o_h.astype(BF16), _place_head(HG * HD, h * HD))
    o_ref[...] = out.astype(o_ref.dtype)


def _nsa_attend(slopes, qn, kvs_b, kvw_b, sel, oc, gates_g, B, T):
    tq = min(128, T)
    nq = T // tq
    tk = min(512, T)
    wk = min(WINDOW + tq, T)
    n = B * T
    return pl.pallas_call(
        functools.partial(_nsa_attn_kernel, tq=tq, tk=tk, wk=wk),
        grid=(B, NSA_G, nq),
        in_specs=[pl.BlockSpec(memory_space=pltpu.SMEM),
                  pl.BlockSpec((tq, HG * HD), lambda b, g, j: (b * nq + j, g)),
                  pl.BlockSpec((T, LANES), lambda b, g, j: (b, g)),
                  pl.BlockSpec((T, LANES), lambda b, g, j: (b, g)),
                  pl.BlockSpec((1, 1, tq, LANES), lambda b, g, j: (b, g, j, 0)),
                  pl.BlockSpec((1, 1, HG, tq, LANES), lambda b, g, j: (b, g, 0, j, 0)),
                  pl.BlockSpec((1, tq, LANES), lambda b, g, j: (g, b * nq + j, 0))],
        out_specs=pl.BlockSpec((tq, HG * HD), lambda b, g, j: (b * nq + j, g)),
        out_shape=jax.ShapeDtypeStruct((n, QW), BF16),
        scratch_shapes=[pltpu.VMEM((HG * tq, 1), F32), pltpu.VMEM((HG * tq, 1), F32),
                        pltpu.VMEM((HG * tq, LANES), F32)],
        compiler_params=_cp("parallel", "parallel", "parallel"),
        name="nsa_attend",
    )(slopes, qn, kvs_b, kvw_b, sel, oc, gates_g)


def _fox_kernel(q_ref, kv0_ref, kv1_ref, d0_ref, d1_ref, o_ref, m_s, l_s, acc_s, *, tq):
    j = pl.program_id(2)
    q = q_ref[...]
    out = jnp.zeros(o_ref.shape, F32)
    for hh, (kv_ref, d_ref) in enumerate(((kv0_ref, d0_ref), (kv1_ref, d1_ref))):
        q128 = _dot(q, _pick_head(2 * HD, hh * HD)).astype(BF16)
        m_s[...] = jnp.full(m_s.shape, NEG, F32)
        l_s[...] = jnp.zeros(l_s.shape, F32)
        acc_s[...] = jnp.zeros(acc_s.shape, F32)

        def step(kt, causal):
            k0 = pl.multiple_of(kt * tq, tq)
            kv = kv_ref[pl.ds(k0, tq), :]
            s = _dot_nt(q128, kv) - d_ref[0, kt]
            if causal:
                s = jnp.where(_iota((tq, tq), 1) <= _iota((tq, tq), 0), s, NEG)
            m_old = m_s[...]
            m_new = jnp.maximum(m_old, jnp.max(s, axis=1, keepdims=True))
            alpha = jnp.exp(m_old - m_new)
            e = jnp.exp(s - m_new)
            l_s[...] = alpha * l_s[...] + jnp.sum(e, axis=1, keepdims=True)
            acc_s[...] = alpha * acc_s[...] + _dot(e.astype(BF16), kv)
            m_s[...] = m_new

        def body(kt, carry):
            step(kt, False)
            return carry

        lax.fori_loop(0, j, body, 0)
        step(j, True)
        o = (acc_s[...] / l_s[...]).astype(BF16)
        out = out + _dot(o, _place_head(2 * HD, hh * HD))
    o_ref[...] = out.astype(o_ref.dtype)


def _fox_attend(qf, kvf_b, d_t, B, T):
    tq = min(512, T)
    nq = T // tq
    d4 = d_t.reshape(B * FOX_H, nq, 1, tq)
    kv_spec = lambda hh: pl.BlockSpec((T, LANES), lambda b, hp, j: (b, 2 * hp + hh))
    d_spec = lambda hh: pl.BlockSpec((1, nq, 1, tq), lambda b, hp, j: (b * FOX_H + 2 * hp + hh, 0, 0, 0))
    return pl.pallas_call(
        functools.partial(_fox_kernel, tq=tq),
        grid=(B, FOX_H // 2, nq),
        in_specs=[pl.BlockSpec((tq, 2 * HD), lambda b, hp, j: (b * nq + j, hp)),
                  kv_spec(0), kv_spec(1), d_spec(0), d_spec(1)],
        out_specs=pl.BlockSpec((tq, 2 * HD), lambda b, hp, j: (b * nq + j, hp)),
        out_shape=jax.ShapeDtypeStruct((B * T, FOX_H * HD), BF16),
        scratch_shapes=[pltpu.VMEM((tq, 1), F32), pltpu.VMEM((tq, 1), F32), pltpu.VMEM((tq, LANES), F32)],
        compiler_params=_cp("parallel", "parallel", "parallel"),
        name="fox_attend",
    )(qf, kvf_b, kvf_b, d4, d4)


def _out_kernel(on_ref, of_ref, x_ref, g1_ref, sc2_ref, sh2_ref, gpm_ref, gpf_ref, wo_ref, wr_ref, br_ref,
                x1_ref, h2_ref, gd_ref):
    y = _dot(on_ref[...], wo_ref[:QW]) + _dot(of_ref[...], wo_ref[QW:])
    x1 = x_ref[...] + g1_ref[...] * _rms(y, gpm_ref[...])
    x1_ref[...] = x1
    h2 = (_rms(x1, gpf_ref[...]) * (1.0 + sc2_ref[...]) + sh2_ref[...]).astype(BF16)
    h2_ref[...] = h2
    scores = jax.nn.sigmoid(_dot(h2, wr_ref[...]))
    sel = _topk_mask(scores + br_ref[...], TOP_K)
    gsel = sel * scores
    gd_ref[...] = gsel / jnp.sum(gsel, axis=1, keepdims=True) * ROUTED_SCALE


def _out_proj(o_n, o_f, x2, gate1, scale2, shift2, g_post_mix, g_pre_ffn, wo, wr, br, rows_per_mod):
    n, d = x2.shape
    tm = min(256, n)
    if rows_per_mod >= tm:
        per = rows_per_mod // tm
        mod_spec = pl.BlockSpec((None, 1, d), lambda i: (i // per, 0, 0))
    else:
        gate1, scale2, shift2 = (jnp.repeat(a[:, 0, :], rows_per_mod, axis=0) for a in (gate1, scale2, shift2))
        mod_spec = pl.BlockSpec((tm, d), lambda i: (i, 0))
    row = lambda w: pl.BlockSpec((tm, w), lambda i: (i, 0))
    full = lambda a: pl.BlockSpec(a.shape, lambda i: (0,) * a.ndim)
    gpm, gpf = g_post_mix.reshape(1, d), g_pre_ffn.reshape(1, d)
    return pl.pallas_call(
        _out_kernel,
        grid=(n // tm,),
        in_specs=[row(QW), row(QW), row(d), mod_spec, mod_spec, mod_spec, full(gpm), full(gpf),
                  full(wo), full(wr), full(br)],
        out_specs=[row(d), row(d), row(LANES)],
        out_shape=[jax.ShapeDtypeStruct((n, d), F32), jax.ShapeDtypeStruct((n, d), BF16),
                   jax.ShapeDtypeStruct((n, LANES), F32)],
        compiler_params=_cp("parallel"),
        name="out_proj_router",
    )(o_n, o_f, x2, gate1, scale2, shift2, gpm, gpf, wo, wr, br)


def _moe_kernel(h_ref, gd_ref, wg_ref, wu_ref, wd_ref, o_ref):
    e = pl.program_id(1)

    @pl.when(e == 0)
    def _():
        o_ref[...] = jnp.zeros_like(o_ref)

    gd = gd_ref[...]
    gcol = jnp.sum(jnp.where(_iota(gd.shape, 1) == e, gd, 0.0), axis=1, keepdims=True)

    @pl.when(jnp.max(gcol) > 0.0)
    def _():
        h = h_ref[...]
        a = _dot(h, wg_ref[0])
        hb = (a * jax.nn.sigmoid(a) * _dot(h, wu_ref[0])).astype(BF16)
        o_ref[...] += _dot(hb, wd_ref[0]) * gcol


def _moe_routed(h2, gd, wg, wu, wd):
    n, d = h2.shape
    n_e, _, f = wg.shape
    tm = min(1024, n)
    return pl.pallas_call(
        _moe_kernel,
        grid=(n // tm, n_e),
        in_specs=[pl.BlockSpec((tm, d), lambda i, e: (i, 0)),
                  pl.BlockSpec((tm, LANES), lambda i, e: (i, 0)),
                  pl.BlockSpec((1, d, f), lambda i, e: (e, 0, 0)),
                  pl.BlockSpec((1, d, f), lambda i, e: (e, 0, 0)),
                  pl.BlockSpec((1, f, d), lambda i, e: (e, 0, 0))],
        out_specs=pl.BlockSpec((tm, d), lambda i, e: (i, 0)),
        out_shape=jax.ShapeDtypeStruct((n, d), F32),
        compiler_params=_cp("parallel", "arbitrary"),
        name="moe_routed",
    )(h2, gd, wg, wu, wd)


def _final_kernel(r_ref, h_ref, x1_ref, g2_ref, gpf_ref, wg_ref, wu_ref, wd_ref, y_ref):
    h = h_ref[...]
    a = _dot(h, wg_ref[...])
    hb = (a * jax.nn.sigmoid(a) * _dot(h, wu_ref[...])).astype(BF16)
    f = r_ref[...] + _dot(hb, wd_ref[...])
    y_ref[...] = x1_ref[...] + g2_ref[...] * _rms(f, gpf_ref[...])


def _final(routed, h2, x1, gate2, g_post_ffn, wsg, wsu, wsd, rows_per_mod):
    n, d = x1.shape
    tm = min(512, n)
    if rows_per_mod >= tm:
        per = rows_per_mod // tm
        mod_spec = pl.BlockSpec((None, 1, d), lambda i: (i // per, 0, 0))
    else:
        gate2 = jnp.repeat(gate2[:, 0, :], rows_per_mod, axis=0)
        mod_spec = pl.BlockSpec((tm, d), lambda i: (i, 0))
    row = pl.BlockSpec((tm, d), lambda i: (i, 0))
    full = lambda a: pl.BlockSpec(a.shape, lambda i: (0,) * a.ndim)
    gpf = g_post_ffn.reshape(1, d)
    return pl.pallas_call(
        _final_kernel,
        grid=(n // tm,),
        in_specs=[row, row, row, mod_spec, full(gpf), full(wsg), full(wsu), full(wsd)],
        out_specs=row,
        out_shape=jax.ShapeDtypeStruct((n, d), F32),
        compiler_params=_cp("parallel"),
        name="shared_ffn_final",
    )(routed, h2, x1, gate2, gpf, wsg, wsu, wsd)


PAGES_PER_STEP = 16


def _page_specs(block, n_pages, pp):
    def spec(jj):
        return pl.BlockSpec(block, lambda b, c, pt: (pt[b * n_pages + c * pp + jj],) + (0,) * (len(block) - 1))
    return [spec(jj) for jj in range(pp)]


def _cmp_sample_kernel(pt_ref, *refs, pp):
    pages = refs[:pp]
    w1_ref, bias_ref, w2_ref, o_ref, carry_ref = refs[pp:]
    c = pl.program_id(1)
    sub_per_page = pages[0].shape[1] // (2 * CMP_STRIDE)
    rows = pp * sub_per_page

    @pl.when(c == 0)
    def _():
        carry_ref[...] = jnp.zeros_like(carry_ref)

    acc = [jnp.zeros((rows, 4 * CMP_HID), F32) for _ in range(2)]
    for t in range(CMP_STRIDE):
        for kv in range(2):
            x_t = jnp.concatenate([p[0, pl.ds(2 * t + kv, sub_per_page, stride=2 * CMP_STRIDE), :] for p in pages],
                                  axis=0)
            acc[kv] = acc[kv] + _dot(x_t.astype(BF16), w1_ref[kv, t])
    out = [None, None]
    for kv in range(2):
        a = acc[kv][:, :2 * CMP_HID]
        b = acc[kv][:, 2 * CMP_HID:]
        a_prev = pltpu.roll(a, 1, 0)
        a_prev = jnp.where(_iota(a.shape, 0) == 0, carry_ref[kv][0:1], a_prev)
        carry_ref[kv] = jnp.broadcast_to(a[rows - 1:rows], carry_ref.shape[1:])
        hid = jax.nn.gelu(a_prev + b + bias_ref[kv]).astype(BF16)
        for g in range(NSA_G):
            term = _dot(hid[:, g * CMP_HID:(g + 1) * CMP_HID], w2_ref[kv, g])
            out[g] = term if out[g] is None else out[g] + term
    for g in range(NSA_G):
        o_ref[0, g] = out[g].astype(o_ref.dtype)


def _cmp_sample_weights(w_cmp1, b_cmp1, w_cmp2, pe_cmp):
    w1 = w_cmp1.reshape(2, 2, CMP_STRIDE, HD, CMP_HID)
    z = jnp.zeros_like(w1[:, 0])
    top = jnp.concatenate([w1[:, 0], z, w1[:, 1], z], axis=-1)
    bot = jnp.concatenate([z, w1[:, 0], z, w1[:, 1]], axis=-1)
    w1bd = jnp.concatenate([top, bot], axis=2).astype(BF16)
    bias = (jnp.einsum("kf,kfh->kh", pe_cmp.reshape(2, -1).astype(BF16), w_cmp1.astype(BF16),
                       preferred_element_type=F32) + b_cmp1)
    bias = jnp.concatenate([bias, bias], axis=-1).reshape(2, 1, 2 * CMP_HID)
    z2 = jnp.zeros((CMP_HID, HD), F32)
    lo = lambda w: jnp.concatenate([w, z2], axis=1)
    hi = lambda w: jnp.concatenate([z2, w], axis=1)
    w2 = jnp.stack([jnp.stack([lo(w_cmp2[0]), hi(w_cmp2[0])]),
                    jnp.stack([hi(w_cmp2[1]), lo(w_cmp2[1])])]).astype(BF16)
    return w1bd, bias, w2


def _compress_sample(cache, pt_flat, Bs, n_pages, csw):
    w1bd, bias, w2 = csw
    pp = min(PAGES_PER_STEP, n_pages)
    page = cache.shape[1]
    rows = pp * (page // CMP_STRIDE)
    n_slots = n_pages * (page // CMP_STRIDE)
    full = lambda a: pl.BlockSpec(a.shape, lambda b, c, pt: (0,) * a.ndim)
    return pl.pallas_call(
        functools.partial(_cmp_sample_kernel, pp=pp),
        grid_spec=pltpu.PrefetchScalarGridSpec(
            num_scalar_prefetch=1, grid=(Bs, n_pages // pp),
            in_specs=_page_specs((1, 2 * page, LANES), n_pages, pp) + [full(w1bd), full(bias), full(w2)],
            out_specs=pl.BlockSpec((1, NSA_G, rows, LANES), lambda b, c, pt: (b, 0, c, 0)),
            scratch_shapes=[pltpu.VMEM((2, 8, 2 * CMP_HID), F32)]),
        out_shape=jax.ShapeDtypeStruct((Bs, NSA_G, n_slots, LANES), BF16),
        compiler_params=_cp("parallel", "arbitrary"),
        name="compress_sample",
    )(pt_flat, *([cache.reshape(-1, 2 * page, LANES)] * pp), w1bd, bias, w2)


def _logf_sample_kernel(pt_ref, *refs, pp):
    pages = refs[:pp]
    o_ref, carry_ref = refs[pp:]

    @pl.when(pl.program_id(1) == 0)
    def _():
        carry_ref[...] = jnp.zeros_like(carry_ref)

    x = jnp.concatenate([p[0] for p in pages], axis=1)
    d = _cumsum_block(x) + carry_ref[:, 0:1]
    o_ref[0] = d
    carry_ref[...] = jnp.broadcast_to(d[:, d.shape[1] - 1:], carry_ref.shape)


def _cumsum_sample(logf_cache_t, pt_flat, Bs, n_pages):
    pp = min(4, n_pages)
    page = logf_cache_t.shape[2]
    return pl.pallas_call(
        functools.partial(_logf_sample_kernel, pp=pp),
        grid_spec=pltpu.PrefetchScalarGridSpec(
            num_scalar_prefetch=1, grid=(Bs, n_pages // pp),
            in_specs=_page_specs((1, FOX_H, page), n_pages, pp),
            out_specs=pl.BlockSpec((1, FOX_H, pp * page), lambda b, c, pt: (b, 0, c)),
            scratch_shapes=[pltpu.VMEM((FOX_H, LANES), F32)]),
        out_shape=jax.ShapeDtypeStruct((Bs, FOX_H, n_pages * page), F32),
        compiler_params=_cp("parallel", "arbitrary"),
        name="cumsum_sample",
    )(pt_flat, *([logf_cache_t] * pp))


ROWS_S = NSA_G * HG * 8


def _sel_sample_kernel(q_ref, cmp_ref, c2s_ref, slope_ref, qidx_ref, oc_ref, sel_ref, *, past, n_take):
    n_slots = cmp_ref.shape[2]
    rg = ROWS_S // NSA_G
    slot = _iota((1, n_slots), 1)
    cend = slot * CMP_STRIDE + (CMP_STRIDE - 1)
    c2s = c2s_ref[...]
    blk = _iota((8, LANES), 1)
    n_blk = past // SLC_BLOCK
    for g in range(NSA_G):
        r = slice(g * rg, (g + 1) * rg)
        q = q_ref[0, r]
        cmp = cmp_ref[0, g]
        qpos = past + qidx_ref[r]
        d_c = qpos - cend
        mask = (d_c >= 0) & (slot >= 1)
        s = _dot_nt(q, cmp) - slope_ref[r] * d_c.astype(F32)
        pb = _masked_softmax(s, mask).astype(BF16)
        oc_ref[0, r] = pltpu.roll(_dot(pb, cmp), HD, 1)
        imp_h = _dot(pb, c2s)
        imp = imp_h[0:8] + imp_h[8:16] + imp_h[16:24] + imp_h[24:32]
        forced = (blk == 0) | (blk == n_blk - 1)
        val = jnp.where(blk < n_blk, imp + jnp.where(forced, FORCE, 0.0), NEG)
        sel8 = _topk_mask(val, n_take)
        sel_ref[0, r] = jnp.concatenate([sel8] * HG, axis=0).astype(sel_ref.dtype)


def _select_sample(qs, cmp_s, slope_rows, qidx_rows, Bs, past):
    n_slots = cmp_s.shape[2]
    n_slc = past // SLC_BLOCK + 1
    assert past % SLC_BLOCK == 0 and n_slc - 1 <= LANES
    c2s = jnp.asarray(_cmp_to_slc_np(n_slots, -1), BF16)
    full = lambda a: pl.BlockSpec(a.shape, lambda b: (0,) * a.ndim)
    return pl.pallas_call(
        functools.partial(_sel_sample_kernel, past=past, n_take=min(SLC_TOPK, n_slc) - 1),
        grid=(Bs,),
        in_specs=[pl.BlockSpec((1, ROWS_S, LANES), lambda b: (b, 0, 0)),
                  pl.BlockSpec((1, NSA_G, n_slots, LANES), lambda b: (b, 0, 0, 0)),
                  full(c2s), full(slope_rows), full(qidx_rows)],
        out_specs=[pl.BlockSpec((1, ROWS_S, LANES), lambda b: (b, 0, 0)),
                   pl.BlockSpec((1, ROWS_S, LANES), lambda b: (b, 0, 0))],
        out_shape=[jax.ShapeDtypeStruct((Bs, ROWS_S, LANES), F32),
                   jax.ShapeDtypeStruct((Bs, ROWS_S, LANES), BF16)],
        compiler_params=_cp("parallel"),
        name="select_sample",
    )(qs, cmp_s, c2s, slope_rows, qidx_rows)


def _online_update(s, ok, v, m_s, l_s, acc_s):
    s = jnp.where(ok, s, NEG)
    m_old = m_s[...]
    m_new = jnp.maximum(m_old, jnp.max(s, axis=1, keepdims=True))
    alpha = jnp.exp(m_old - m_new)
    e = jnp.where(ok, jnp.exp(s - m_new), 0.0)
    l_s[...] = alpha * l_s[...] + jnp.sum(e, axis=1, keepdims=True)
    acc_s[...] = alpha * acc_s[...] + _dot(e.astype(BF16), v)
    m_s[...] = m_new


def _pad_rows8(x):
    return jnp.concatenate([x, jnp.zeros((8 - x.shape[0], x.shape[1]), x.dtype)], axis=0)


def _slc_sample_kernel(pt_ref, *refs, pp, past, t_new):
    pages = refs[:pp]
    q_ref, sel_ref, new_ref, slope_ref, qidx_ref, o_ref, m_s, l_s, acc_s = refs[pp:]
    c = pl.program_id(1)
    page = pages[0].shape[1]
    nk = pp * page

    @pl.when(c == 0)
    def _():
        m_s[...] = jnp.full(m_s.shape, NEG, F32)
        l_s[...] = jnp.zeros(l_s.shape, F32)
        acc_s[...] = jnp.zeros(acc_s.shape, F32)

    q = q_ref[0]
    slope = slope_ref[...]
    qidx = qidx_ref[...]
    kpos = c * nk + _iota((1, nk), 1)
    expand = _onehot(_iota((LANES, 1), 0) == (kpos >> 6))
    ok = _dot(sel_ref[0], expand) > 0.5

    @pl.when(jnp.max(jnp.where(ok, 1.0, 0.0)) > 0.0)
    def _():
        k = jnp.concatenate([p[0, :, :LANES].astype(BF16) for p in pages], axis=0)
        v = jnp.concatenate([p[0, :, LANES:].astype(BF16) for p in pages], axis=0)
        dist = (past + qidx - kpos).astype(F32)
        _online_update(_dot_nt(q, k) - slope * dist, ok, v, m_s, l_s, acc_s)

    @pl.when(c == pl.num_programs(1) - 1)
    def _():
        new = _pad_rows8(new_ref[0])
        kj = _iota((1, 8), 1)
        ok_new = (kj <= qidx) & (kj < t_new)
        s_new = _dot_nt(q, new[:, :LANES].astype(BF16)) - slope * (qidx - kj).astype(F32)
        _online_update(s_new, ok_new, new[:, LANES:].astype(BF16), m_s, l_s, acc_s)
        o_ref[0] = acc_s[...] / l_s[...]


def _slc_sample(cache, pt_flat, qs, sel_rows, kvs_new, slope_rows, qidx_rows, Bs, n_pages, past):
    pp = min(PAGES_PER_STEP, n_pages)
    page = cache.shape[1]
    t_new = kvs_new.shape[1]
    full = lambda a: pl.BlockSpec(a.shape, lambda b, c, pt: (0,) * a.ndim)
    per_seq = lambda a: pl.BlockSpec((1,) + a.shape[1:], lambda b, c, pt: (b,) + (0,) * (a.ndim - 1))
    return pl.pallas_call(
        functools.partial(_slc_sample_kernel, pp=pp, past=past, t_new=t_new),
        grid_spec=pltpu.PrefetchScalarGridSpec(
            num_scalar_prefetch=1, grid=(Bs, n_pages // pp),
            in_specs=_page_specs((1, page, KVW), n_pages, pp)
            + [per_seq(qs), per_seq(sel_rows), per_seq(kvs_new), full(slope_rows), full(qidx_rows)],
            out_specs=pl.BlockSpec((1, ROWS_S, LANES), lambda b, c, pt: (b, 0, 0)),
            scratch_shapes=[pltpu.VMEM((ROWS_S, 1), F32), pltpu.VMEM((ROWS_S, 1), F32),
                            pltpu.VMEM((ROWS_S, LANES), F32)]),
        out_shape=jax.ShapeDtypeStruct((Bs, ROWS_S, LANES), F32),
        compiler_params=_cp("parallel", "arbitrary"),
        name="selected_sample",
    )(pt_flat, *([cache] * pp), qs, sel_rows, kvs_new, slope_rows, qidx_rows)


def _win_sample_kernel(q_ref, win_ref, new_ref, oc_ref, os_ref, gates_ref, slope_ref, qidx_ref, o_ref, *, t_new):
    q = q_ref[0]
    win = win_ref[0]
    n_win = win.shape[0]
    new = _pad_rows8(new_ref[0])
    k = jnp.concatenate([win[:, :LANES].astype(BF16), new[:, :LANES].astype(BF16)], axis=0)
    v = jnp.concatenate([win[:, LANES:].astype(BF16), new[:, LANES:].astype(BF16)], axis=0)
    kj = _iota((1, n_win + 8), 1)
    d_w = n_win + qidx_ref[...] - kj
    mask = (d_w >= 0) & (d_w < WINDOW) & (kj < n_win + t_new)
    s = _dot_nt(q, k) - slope_ref[...] * d_w.astype(F32)
    o_w = _dot(_masked_softmax(s, mask).astype(BF16), v)
    gates = gates_ref[0]
    o_ref[0] = gates[:, 0:1] * oc_ref[0] + gates[:, 1:2] * os_ref[0] + gates[:, 2:3] * o_w


def _win_sample(qs, state_win, kvw_new, oc, o_s, gate_rows, slope_rows, qidx_rows, Bs):
    t_new = kvw_new.shape[1]
    full = lambda a: pl.BlockSpec(a.shape, lambda b: (0,) * a.ndim)
    per_seq = lambda a: pl.BlockSpec((1,) + a.shape[1:], lambda b: (b,) + (0,) * (a.ndim - 1))
    return pl.pallas_call(
        functools.partial(_win_sample_kernel, t_new=t_new),
        grid=(Bs,),
        in_specs=[per_seq(qs), per_seq(state_win), per_seq(kvw_new), per_seq(oc), per_seq(o_s),
                  per_seq(gate_rows), full(slope_rows), full(qidx_rows)],
        out_specs=pl.BlockSpec((1, ROWS_S, LANES), lambda b: (b, 0, 0)),
        out_shape=jax.ShapeDtypeStruct((Bs, ROWS_S, LANES), F32),
        compiler_params=_cp("parallel"),
        name="window_sample",
    )(qs, state_win, kvw_new, oc, o_s, gate_rows, slope_rows, qidx_rows)


ROWS_F = FOX_H * 8
FOX_K = FOX_H * HD


def _fox_sample_kernel(pt_ref, *refs, pp, t_new):
    pages = refs[:pp]
    q_ref, d_ref, new_ref, dnew_ref, qidx_ref, o_ref, m_s, l_s, acc_s = refs[pp:]
    c = pl.program_id(1)

    @pl.when(c == 0)
    def _():
        m_s[...] = jnp.full(m_s.shape, NEG, F32)
        l_s[...] = jnp.zeros(l_s.shape, F32)
        acc_s[...] = jnp.zeros(acc_s.shape, F32)

    q = q_ref[0]
    k = jnp.concatenate([p[0, :, :FOX_K].astype(BF16) for p in pages], axis=0)
    v = jnp.concatenate([p[0, :, FOX_K:].astype(BF16) for p in pages], axis=0)
    d = d_ref[0]
    nk = d.shape[1]
    bias = jnp.concatenate([jnp.broadcast_to(d[h:h + 1], (8, nk)) for h in range(FOX_H)], axis=0)
    _online_update(_dot_nt(q, k) - bias, jnp.full((ROWS_F, nk), True), v, m_s, l_s, acc_s)

    @pl.when(c == pl.num_programs(1) - 1)
    def _():
        new = _pad_rows8(new_ref[0])
        kj = _iota((1, 8), 1)
        ok_new = (kj <= qidx_ref[...]) & (kj < t_new)
        s_new = _dot_nt(q, new[:, :FOX_K].astype(BF16)) - dnew_ref[0]
        _online_update(s_new, ok_new, new[:, FOX_K:].astype(BF16), m_s, l_s, acc_s)
        o_ref[0] = acc_s[...] / l_s[...]


def _fox_sample(cache, pt_flat, qf_bd, d_past_t, kvf_new, dnew_rows, qidx_f, Bs, n_pages):
    pp = min(PAGES_PER_STEP, n_pages)
    page = cache.shape[1]
    t_new = kvf_new.shape[1]
    full = lambda a: pl.BlockSpec(a.shape, lambda b, c, pt: (0,) * a.ndim)
    per_seq = lambda a: pl.BlockSpec((1,) + a.shape[1:], lambda b, c, pt: (b,) + (0,) * (a.ndim - 1))
    return pl.pallas_call(
        functools.partial(_fox_sample_kernel, pp=pp, t_new=t_new),
        grid_spec=pltpu.PrefetchScalarGridSpec(
            num_scalar_prefetch=1, grid=(Bs, n_pages // pp),
            in_specs=_page_specs((1, page, KVF), n_pages, pp)
            + [per_seq(qf_bd), pl.BlockSpec((1, FOX_H, pp * page), lambda b, c, pt: (b, 0, c)),
               per_seq(kvf_new), per_seq(dnew_rows), full(qidx_f)],
            out_specs=pl.BlockSpec((1, ROWS_F, FOX_K), lambda b, c, pt: (b, 0, 0)),
            scratch_shapes=[pltpu.VMEM((ROWS_F, 1), F32), pltpu.VMEM((ROWS_F, 1), F32),
                            pltpu.VMEM((ROWS_F, FOX_K), F32)]),
        out_shape=jax.ShapeDtypeStruct((Bs, ROWS_F, FOX_K), F32),
        compiler_params=_cp("parallel", "arbitrary"),
        name="fox_sample",
    )(pt_flat, *([cache] * pp), qf_bd, d_past_t, kvf_new, dnew_rows, qidx_f)


def _alibi_slopes():
    return jnp.exp2(-8.0 * jnp.arange(1, NSA_H + 1, dtype=F32) / NSA_H)


def _ffn_tail(o_n, o_f, x2, mod, norms, lw, rows_per_mod):
    gate1, scale2, shift2, gate2 = mod
    g_post_mix, g_pre_ffn, g_post_ffn = norms
    x1, h2, gd = _out_proj(o_n, o_f, x2, gate1, scale2, shift2, g_post_mix, g_pre_ffn,
                           lw["wo"], lw["wr"], lw["br"], rows_per_mod)
    routed = _moe_routed(h2, gd, lw["weg"], lw["weu"], lw["wed"])
    return _final(routed, h2, x1, gate2, g_post_ffn, lw["wsg"], lw["wsu"], lw["wsd"], rows_per_mod)


def _prompt_layer(x, mod, norms, lw):
    B, T, d = x.shape
    n = B * T
    shift1, scale1, gate1, shift2, scale2, gate2 = mod
    g_pre_mix, g_post_mix, g_pre_ffn, g_post_ffn = norms
    x2 = x.reshape(n, d)
    pr = _in_proj(x2, scale1, shift1, g_pre_mix, lw["w_all"], lw["b_small"], T)
    small = pr["small"]
    logf = small[:, N_GATE:N_GATE + FOX_H]
    gates_g = small[:, :N_GATE].reshape(n, NSA_G, HG * 3).transpose(1, 0, 2)
    gates_g = jnp.pad(gates_g, ((0, 0), (0, 0), (0, LANES - HG * 3)))
    slopes = _alibi_slopes()

    cmp = _compress_prompt(pr["kvc"], B, T, lw["cmp_w"])
    oc, sel = _nsa_select(slopes, pr["qn"], cmp, B, T)
    o_n = _nsa_attend(slopes, pr["qn"], pr["kvs_b"], pr["kvw_b"], sel, oc, gates_g, B, T)

    d_t = _cumsum_prompt(logf.reshape(B, T, FOX_H).transpose(0, 2, 1))
    o_f = _fox_attend(pr["qf"], pr["kvf_b"], d_t, B, T)

    y = _ffn_tail(o_n, o_f, x2, (gate1, scale2, shift2, gate2), (g_post_mix, g_pre_ffn, g_post_ffn), lw, T)
    win_rows = min(WINDOW, T)
    states = (pr["kvc"].reshape(B, T, 2, NSA_G, HD), pr["kvs"].reshape(B, T, 2, NSA_G, HD),
              pr["kvf"].reshape(B, T, 2, FOX_H, HD), logf.reshape(B, T, FOX_H),
              pr["kvw"].reshape(B, T, 2, NSA_G, HD)[:, T - win_rows:])
    return y.reshape(B, T, d), states


def _sample_layer(x, mod, norms, lw, caches):
    Bs, Tn, d = x.shape
    n = Bs * Tn
    assert Tn <= 8
    cache_cmp, cache_slc, cache_fox, cache_logf, state_win, page_table = caches
    n_pages = page_table.shape[1]
    page = cache_cmp.shape[1]
    past = n_pages * page
    shift1, scale1, gate1, shift2, scale2, gate2 = mod
    g_pre_mix, g_post_mix, g_pre_ffn, g_post_ffn = norms
    x2 = x.reshape(n, d)
    pr = _in_proj(x2, scale1, shift1, g_pre_mix, lw["w_all"], lw["b_small"], Tn)
    small = pr["small"]
    logf = small[:, N_GATE:N_GATE + FOX_H].reshape(Bs, Tn, FOX_H)
    pt_flat = page_table.reshape(-1)
    padq = ((0, 0), (0, 0), (0, 0), (0, 8 - Tn), (0, 0))

    qn = pr["qn"].reshape(Bs, Tn, NSA_G, HG, HD).transpose(0, 2, 3, 1, 4)
    qn = jnp.pad(qn, padq)
    z = jnp.zeros_like(qn[:, 0])
    qs = jnp.stack([jnp.concatenate([qn[:, 0], z], -1), jnp.concatenate([z, qn[:, 1]], -1)], axis=1)
    qs = qs.reshape(Bs, ROWS_S, LANES)
    slope_rows = jnp.broadcast_to(_alibi_slopes().reshape(NSA_H, 1, 1), (NSA_H, 8, 1)).reshape(ROWS_S, 1)
    qidx_rows = jnp.broadcast_to(jnp.arange(8, dtype=jnp.int32).reshape(1, 8, 1), (NSA_H, 8, 1)).reshape(ROWS_S, 1)
    gate_rows = small[:, :N_GATE].reshape(Bs, Tn, NSA_H, 3).transpose(0, 2, 1, 3)
    gate_rows = jnp.pad(gate_rows, ((0, 0), (0, 0), (0, 8 - Tn), (0, LANES - 3))).reshape(Bs, ROWS_S, LANES)

    cmp_s = _compress_sample(cache_cmp.reshape(-1, page, KVW), pt_flat, Bs, n_pages, lw["cmp_sw"])
    oc, sel_rows = _select_sample(qs, cmp_s, slope_rows, qidx_rows, Bs, past)
    o_s = _slc_sample(cache_slc.reshape(-1, page, KVW), pt_flat, qs, sel_rows, pr["kvs"].reshape(Bs, Tn, KVW),
                      slope_rows, qidx_rows, Bs, n_pages, past)
    win = state_win.reshape(Bs, -1, KVW)
    o_rows = _win_sample(qs, win, pr["kvw"].reshape(Bs, Tn, KVW), oc, o_s, gate_rows, slope_rows, qidx_rows, Bs)
    o_rows = o_rows.reshape(Bs, NSA_G, HG, 8, NSA_G, HD)[:, :, :, :Tn]
    o_n = jnp.stack([o_rows[:, 0, :, :, 0], o_rows[:, 1, :, :, 1]], axis=1)
    o_n = o_n.transpose(0, 3, 1, 2, 4).reshape(n, QW).astype(BF16)

    qf = jnp.pad(pr["qf"].reshape(Bs, Tn, FOX_H, HD).transpose(0, 2, 1, 3), ((0, 0), (0, 0), (0, 8 - Tn), (0, 0)))
    qf_bd = (qf[:, :, :, None, :] * jnp.eye(FOX_H, dtype=BF16)[None, :, None, :, None]).reshape(Bs, ROWS_F, FOX_K)
    d_past_t = _cumsum_sample(cache_logf.transpose(0, 2, 1), pt_flat, Bs, n_pages)
    d_new = d_past_t[:, :, past - 1:] + jnp.cumsum(logf, axis=1).transpose(0, 2, 1)
    dnew_rows = jnp.broadcast_to(jnp.pad(d_new, ((0, 0), (0, 0), (0, 8 - Tn)))[:, :, None, :],
                                 (Bs, FOX_H, 8, 8)).reshape(Bs, ROWS_F, 8)
    qidx_f = jnp.broadcast_to(jnp.arange(8, dtype=jnp.int32).reshape(1, 8, 1), (FOX_H, 8, 1)).reshape(ROWS_F, 1)
    o_frows = _fox_sample(cache_fox.reshape(-1, page, KVF), pt_flat, qf_bd, d_past_t,
                          pr["kvf"].reshape(Bs, Tn, KVF), dnew_rows, qidx_f, Bs, n_pages)
    o_frows = o_frows.reshape(Bs, FOX_H, 8, FOX_H, HD)[:, :, :Tn]
    o_f = jnp.stack([o_frows[:, h, :, h] for h in range(FOX_H)], axis=2)
    o_f = o_f.reshape(n, FOX_H * HD).astype(BF16)

    y = _ffn_tail(o_n, o_f, x2, (gate1, scale2, shift2, gate2), (g_post_mix, g_pre_ffn, g_post_ffn), lw, Tn)
    kvw_new = pr["kvw"].reshape(Bs, Tn, 2, NSA_G, HD)
    states = (pr["kvc"].reshape(Bs, Tn, 2, NSA_G, HD), pr["kvs"].reshape(Bs, Tn, 2, NSA_G, HD),
              pr["kvf"].reshape(Bs, Tn, 2, FOX_H, HD), logf,
              jnp.concatenate([state_win, kvw_new], axis=1)[:, Tn:])
    return y.reshape(Bs, Tn, d), states


def _layer_weights(w_in, b_nsa_gate, b_forget, w_cmp1, b_cmp1, w_cmp2, pe_cmp, w_out, w_router, b_router,
                   w_exp_gate, w_exp_up, w_exp_down, w_sh_gate, w_sh_up, w_sh_down):
    w_all, b_small = _in_weights(w_in, b_nsa_gate, b_forget)
    n_e = w_router.shape[1]
    wr = jnp.pad(w_router, ((0, 0), (0, LANES - n_e))).astype(BF16)
    br = jnp.concatenate([b_router.astype(F32), jnp.full((LANES - n_e,), NEG_INF, F32)]).reshape(1, LANES)
    return dict(w_all=w_all, b_small=b_small,
                cmp_w=_cmp_weights(w_cmp1, b_cmp1, w_cmp2, pe_cmp),
                cmp_sw=_cmp_sample_weights(w_cmp1, b_cmp1, w_cmp2, pe_cmp),
                wo=w_out.astype(BF16), wr=wr, br=br,
                weg=w_exp_gate.astype(BF16), weu=w_exp_up.astype(BF16), wed=w_exp_down.astype(BF16),
                wsg=w_sh_gate.astype(BF16), wsu=w_sh_up.astype(BF16), wsd=w_sh_down.astype(BF16))


def kernel(x_prompt, x_sample, c_prompt, c_sample, cache_cmp_kv, cache_slc_kv, cache_fox_kv, cache_fox_logf,
           state_win_kv, page_table, w_ada, b_ada, g_pre_mix, g_post_mix, g_pre_ffn, g_post_ffn, w_in,
           b_nsa_gate, b_forget, w_cmp1, b_cmp1, w_cmp2, pe_cmp, w_out, w_router, b_router, w_exp_gate,
           w_exp_up, w_exp_down, w_sh_gate, w_sh_up, w_sh_down):
    depth = w_in.shape[0]
    Bp = x_prompt.shape[0]
    xp, xs = x_prompt, x_sample
    st_p = [[] for _ in range(5)]
    st_s = [[] for _ in range(5)]
    for l in range(depth):
        lw = _layer_weights(w_in[l], b_nsa_gate[l], b_forget[l], w_cmp1[l], b_cmp1[l], w_cmp2[l], pe_cmp[l],
                            w_out[l], w_router[l], b_router[l], w_exp_gate[l], w_exp_up[l], w_exp_down[l],
                            w_sh_gate[l], w_sh_up[l], w_sh_down[l])
        m = _modulation(jnp.concatenate([c_prompt, c_sample], axis=0), w_ada[l], b_ada[l])
        mods = jnp.split(m[:, None, :], 6, axis=-1)
        norms = (g_pre_mix[l], g_post_mix[l], g_pre_ffn[l], g_post_ffn[l])
        xp, sp = _prompt_layer(xp, [a[:Bp] for a in mods], norms, lw)
        caches = (cache_cmp_kv[l], cache_slc_kv[l], cache_fox_kv[l], cache_fox_logf[l], state_win_kv[l], page_table)
        xs, ss = _sample_layer(xs, [a[Bp:] for a in mods], norms, lw, caches)
        for lst, a in zip(st_p, sp):
            lst.append(a)
        for lst, a in zip(st_s, ss):
            lst.append(a)
    return (xp, xs, *[jnp.stack(s) for s in st_p], *[jnp.stack(s) for s in st_s])
```

```python
import functools

import numpy as np
import jax
import jax.numpy as jnp
from jax import lax
from jax.experimental import pallas as pl
from jax.experimental.pallas import tpu as pltpu

F32 = jnp.float32
BF16 = jnp.bfloat16

HD = 64
NSA_H = 8
NSA_G = 2
HG = NSA_H // NSA_G
FOX_H = 8
CMP_STRIDE = 16
CMP_LEN = 32
CMP_HID = 128
SLC_BLOCK = 64
SLC_TOPK = 16
WINDOW = 512
TOP_K = 8
ROUTED_SCALE = 2.5
RMS_EPS = 1e-6
NEG = -1e30
TINY = 1e-30
FORCE = 1e4
NEG_INF = float("-inf")

LANES = 128
KVW = 2 * NSA_G * HD
KVF = 2 * FOX_H * HD
QW = NSA_H * HD
VMEM_LIMIT = 56 * 1024 * 1024


def _cp(*sem):
    return pltpu.CompilerParams(dimension_semantics=sem, vmem_limit_bytes=VMEM_LIMIT)


def _rms(x, g):
    return x * lax.rsqrt(jnp.mean(x * x, axis=-1, keepdims=True) + RMS_EPS) * g


def _dot(a, b):
    return jnp.dot(a, b, preferred_element_type=F32)


def _dot_nt(a, b):
    return lax.dot_general(a, b, (((1,), (1,)), ((), ())), preferred_element_type=F32)


def _iota(shape, dim):
    return lax.broadcasted_iota(jnp.int32, shape, dim)


def _onehot(cond):
    return jnp.where(cond, 1.0, 0.0).astype(BF16)


def _pick_head(width_in, h_off):
    r = _iota((width_in, LANES), 0)
    c = _iota((width_in, LANES), 1)
    return _onehot((r == c + h_off) & (c < HD))


def _place_head(width_out, off):
    r = _iota((LANES, width_out), 0)
    c = _iota((LANES, width_out), 1)
    return _onehot((r >= HD) & (c == r - HD + off))


def _split3(x):
    hi = x.astype(BF16)
    r1 = x - hi.astype(F32)
    mid = r1.astype(BF16)
    lo = (r1 - mid.astype(F32)).astype(BF16)
    return hi, mid, lo


def _mod_kernel(c_ref, w_ref, b_ref, o_ref):
    c = c_ref[...]
    a = (c * jax.nn.sigmoid(c)).astype(BF16)
    o_ref[...] = _dot(a, w_ref[...].astype(BF16)) + b_ref[...]


def _modulation(c_all, w_ada, b_ada):
    nb, d = c_all.shape
    n_chunk = w_ada.shape[1] // d
    return pl.pallas_call(
        _mod_kernel,
        grid=(n_chunk,),
        in_specs=[pl.BlockSpec((nb, d), lambda i: (0, 0)),
                  pl.BlockSpec((d, d), lambda i: (0, i)),
                  pl.BlockSpec((1, d), lambda i: (0, i))],
        out_specs=pl.BlockSpec((nb, d), lambda i: (0, i)),
        out_shape=jax.ShapeDtypeStruct((nb, w_ada.shape[1]), F32),
        compiler_params=_cp("parallel"),
        name="modulation",
    )(c_all, w_ada, b_ada.reshape(1, -1))


N_GATE = 3 * NSA_H
SEL_TK = 256
LOG2E = 1.4426950408889634
_OPERAND_SEGS = (("ks_op", NSA_G * LANES, "bf16"), ("vs_op", NSA_G * LANES, "bf16"),
                 ("kw_op", NSA_G * LANES, "bf16"), ("vw_op", NSA_G * LANES, "bf16"),
                 ("kf_op", FOX_H * LANES, "bf16"), ("vf_op", FOX_H * LANES, "bf16"))
_IN_SEGS = {
    "prompt": (("kvc", KVW, "f32"), ("small", LANES, "f32"), ("qn", QW, "bf16"), ("qf", QW, "bf16"))
    + _OPERAND_SEGS + (("kvc_t", KVW, "f32_t"), ("kvs_t", KVW, "f32_t"), ("kvw_t", KVW, "f32_t"),
                       ("kvf_t", KVF, "f32_t")),
    "sample": (("kvc", KVW, "f32"), ("kvs", KVW, "f32"), ("kvw", KVW, "f32"), ("kvf", KVF, "f32"),
               ("small", LANES, "f32"), ("qn", QW, "bf16"), ("qf", QW, "bf16")),
}


def _in_kernel(x_ref, sc_ref, sh_ref, g_ref, w_ref, wt_ref, bias_ref, *out_refs, segs, seq_len):
    x = x_ref[...]
    tm = x.shape[0]
    h = _rms(x, g_ref[...]) * (1.0 + sc_ref[...]) + sh_ref[...]
    hb = h.astype(BF16)
    off = off_t = 0
    for (name, width, kind), o_ref in zip(segs, out_refs):
        if kind == "f32_t":
            o_ref[0] = _dot_nt(wt_ref[off_t:off_t + width, :], hb)
            off_t += width
            continue
        p = _dot(hb, w_ref[:, off:off + width])
        if name == "small":
            z = p + bias_ref[...]
            lane = _iota(z.shape, 1)
            logsig = jnp.minimum(z, 0.0) - jnp.log1p(jnp.exp(-jnp.abs(z)))
            p = jnp.where(lane < N_GATE, jax.nn.sigmoid(z), logsig)
        elif name in ("vs_op", "vw_op", "vf_op"):
            p = jnp.where((_iota(p.shape, 1) & (LANES - 1)) < HD, 1.0, p)
        elif name == "ks_op":
            t = (pl.program_id(0) * tm) % seq_len + _iota(p.shape, 0)
            p = jnp.where((_iota(p.shape, 1) & (LANES - 1)) == HD, (t & (SEL_TK - 1)).astype(F32), p)
        o_ref[...] = p.astype(o_ref.dtype)
        off += width


def _in_weights(w_in, b_nsa_gate, b_forget):
    d = w_in.shape[0]
    o_qn, o_kvc, o_kvs, o_kvw = 0, QW, QW + KVW, QW + 2 * KVW
    o_gn = QW + 3 * KVW
    o_qf = o_gn + N_GATE
    o_kvf = o_qf + QW
    o_ff = o_kvf + KVF
    z = jnp.zeros((d, HD), F32)

    def chunks(w, n, lo_first):
        parts = []
        for i in range(n):
            c = w[:, i * HD:(i + 1) * HD]
            parts += [c, z] if lo_first(i) else [z, c]
        return jnp.concatenate(parts, axis=1)

    w_kvc, w_kvs, w_kvw, w_kvf = (w_in[:, o:o + wd] for o, wd in ((o_kvc, KVW), (o_kvs, KVW), (o_kvw, KVW),
                                                                   (o_kvf, KVF)))
    nk, fk = NSA_G * HD, FOX_H * HD
    scale = HD ** -0.5
    cols = dict(
        kvc=w_kvc, kvs=w_kvs, kvw=w_kvw, kvf=w_kvf, kvc_t=w_kvc, kvs_t=w_kvs, kvw_t=w_kvw, kvf_t=w_kvf,
        small=jnp.concatenate([w_in[:, o_gn:o_gn + N_GATE], w_in[:, o_ff:o_ff + FOX_H],
                               jnp.zeros((d, LANES - N_GATE - FOX_H), F32)], axis=1),
        qn=w_in[:, o_qn:o_qn + QW] * scale, qf=w_in[:, o_qf:o_qf + QW] * (scale * LOG2E),
        ks_op=chunks(w_kvs[:, :nk], NSA_G, lambda i: True), vs_op=chunks(w_kvs[:, nk:], NSA_G, lambda i: False),
        kw_op=chunks(w_kvw[:, :nk], NSA_G, lambda i: True), vw_op=chunks(w_kvw[:, nk:], NSA_G, lambda i: False),
        kf_op=chunks(w_kvf[:, :fk], FOX_H, lambda i: i % 2 == 0), vf_op=chunks(w_kvf[:, fk:], FOX_H, lambda i: False))
    bias = jnp.concatenate([b_nsa_gate, b_forget, jnp.zeros((LANES - N_GATE - FOX_H,), F32)]).reshape(1, LANES)
    return cols, bias


def _in_proj(x2, scale, shift, g_pre, cols, bias, rows_per_mod, group):
    n, d = x2.shape
    tm = min(256, n)
    segs = _IN_SEGS[group]
    if rows_per_mod >= tm:
        per = rows_per_mod // tm
        mod_spec = pl.BlockSpec((None, 1, d), lambda i: (i // per, 0, 0))
    else:
        assert not any(k == "f32_t" for _, _, k in segs)
        per = 1
        scale = jnp.repeat(scale[:, 0, :], rows_per_mod, axis=0)
        shift = jnp.repeat(shift[:, 0, :], rows_per_mod, axis=0)
        mod_spec = pl.BlockSpec((tm, d), lambda i: (i, 0))
    w = jnp.concatenate([cols[s] for s, _, k in segs if k != "f32_t"], axis=1).astype(BF16)
    t_names = [s for s, _, k in segs if k == "f32_t"]
    wt = (jnp.concatenate([cols[s] for s in t_names], axis=1).T.astype(BF16) if t_names
          else jnp.zeros((8, d), BF16))
    n_seq = n // rows_per_mod
    out_specs, out_shape = [], []
    for _, width, kind in segs:
        if kind == "f32_t":
            out_specs.append(pl.BlockSpec((1, width, tm), lambda i: (i // per, 0, i % per)))
            out_shape.append(jax.ShapeDtypeStruct((n_seq, width, rows_per_mod), F32))
        else:
            out_specs.append(pl.BlockSpec((tm, width), lambda i: (i, 0)))
            out_shape.append(jax.ShapeDtypeStruct((n, width), F32 if kind == "f32" else BF16))
    full = lambda a: pl.BlockSpec(a.shape, lambda i: (0,) * a.ndim)
    outs = pl.pallas_call(
        functools.partial(_in_kernel, segs=segs, seq_len=rows_per_mod),
        grid=(n // tm,),
        in_specs=[pl.BlockSpec((tm, d), lambda i: (i, 0)), mod_spec, mod_spec,
                  pl.BlockSpec((1, d), lambda i: (0, 0)), full(w), full(wt), full(bias)],
        out_specs=out_specs,
        out_shape=out_shape,
        compiler_params=_cp("parallel"),
        name="in_proj_" + group,
    )(x2, scale, shift, g_pre.reshape(1, d), w, wt, bias)
    return dict(zip([s for s, _, _ in segs], outs))


def _cmp_kernel(subk_ref, subv_ref, w1_ref, pe_ref, b1_ref, w2_ref, o_ref):
    half = CMP_STRIDE * HD
    out = None
    for kv, sub_ref in enumerate((subk_ref, subv_ref)):
        sub = sub_ref[0, 0]
        w1 = w1_ref[kv]
        a = _dot(sub, w1[:half])
        b = _dot(sub, w1[half:])
        n_sub = a.shape[0]
        b_next = pltpu.roll(b, n_sub - 1, 0)
        bias = _dot(pe_ref[kv], w1)[0:1] + b1_ref[kv]
        hid = jax.nn.gelu(a + b_next + bias).astype(BF16)
        term = _dot(hid, w2_ref[kv])
        out = term if out is None else out + term
    o_ref[0, 0] = out.astype(o_ref.dtype)


def _cmp_weights(w_cmp1, b_cmp1, w_cmp2, pe_cmp):
    w1 = w_cmp1.astype(BF16)
    pe = jnp.broadcast_to(pe_cmp.reshape(2, 1, CMP_LEN * HD), (2, 8, CMP_LEN * HD)).astype(BF16)
    b1 = b_cmp1.reshape(2, 1, CMP_HID)
    z = jnp.zeros((CMP_HID, HD), F32)
    w2 = jnp.stack([jnp.concatenate([w_cmp2[0], z], axis=1),
                    jnp.concatenate([z, w_cmp2[1]], axis=1)]).astype(BF16)
    return w1, pe, b1, w2


def _compress_prompt(kvc, B, T, cw):
    w1, pe, b1, w2 = cw
    n_sub = T // CMP_STRIDE
    sub = kvc.reshape(B, n_sub, CMP_STRIDE, 2 * NSA_G, HD).transpose(0, 3, 1, 2, 4)
    sub = sub.reshape(B, 2 * NSA_G, n_sub, CMP_STRIDE * HD).astype(BF16)
    feat = CMP_STRIDE * HD
    return pl.pallas_call(
        _cmp_kernel,
        grid=(B, NSA_G),
        in_specs=[pl.BlockSpec((1, 1, n_sub, feat), lambda b, g: (b, g, 0, 0)),
                  pl.BlockSpec((1, 1, n_sub, feat), lambda b, g: (b, NSA_G + g, 0, 0)),
                  pl.BlockSpec(w1.shape, lambda b, g: (0, 0, 0)),
                  pl.BlockSpec(pe.shape, lambda b, g: (0, 0, 0)),
                  pl.BlockSpec(b1.shape, lambda b, g: (0, 0, 0)),
                  pl.BlockSpec(w2.shape, lambda b, g: (0, 0, 0))],
        out_specs=pl.BlockSpec((1, 1, n_sub, LANES), lambda b, g: (b, g, 0, 0)),
        out_shape=jax.ShapeDtypeStruct((B, NSA_G, n_sub, LANES), BF16),
        compiler_params=_cp("parallel", "parallel"),
        name="compress_prompt",
    )(sub, sub, w1, pe, b1, w2)


def _cumsum_block(x):
    n = x.shape[1]
    u = _onehot(_iota((n, n), 0) <= _iota((n, n), 1))
    hi, mid, lo = _split3(x)
    return _dot(hi, u) + _dot(mid, u) + _dot(lo, u)


def _cumsum_kernel(x_ref, o_ref, carry_ref):
    @pl.when(pl.program_id(1) == 0)
    def _():
        carry_ref[...] = jnp.zeros_like(carry_ref)

    d = _cumsum_block(x_ref[0]) + carry_ref[:, 0:1]
    o_ref[0] = d
    carry_ref[...] = jnp.broadcast_to(d[:, d.shape[1] - 1:], carry_ref.shape)


def _cumsum_prompt(logf_t):
    B, H, T = logf_t.shape
    tc = min(512, T)
    return pl.pallas_call(
        _cumsum_kernel,
        grid=(B, T // tc),
        in_specs=[pl.BlockSpec((1, H, tc), lambda b, c: (b, 0, c))],
        out_specs=pl.BlockSpec((1, H, tc), lambda b, c: (b, 0, c)),
        out_shape=jax.ShapeDtypeStruct((B, H, T), F32),
        scratch_shapes=[pltpu.VMEM((H, LANES), F32)],
        compiler_params=_cp("parallel", "arbitrary"),
        name="cumsum_prompt",
    )(logf_t)


MASK_BIG = 2.0 ** 30


def _topk_mask(val, n_take, axis):
    idx_f = _iota(val.shape, axis).astype(F32)
    sel = jnp.zeros(val.shape, F32)
    for _ in range(n_take):
        m = jnp.max(val, axis=axis, keepdims=True)
        first = jnp.min(jnp.where(val == m, idx_f, float(val.shape[axis])), axis=axis, keepdims=True)
        pick = idx_f == first
        sel = jnp.where(pick, 1.0, sel)
        val = jnp.where(pick, NEG_INF, val)
    return sel


def _masked_softmax(s, mask, axis=1):
    s = jnp.where(mask, s, NEG)
    m = jnp.max(s, axis=axis, keepdims=True)
    e = jnp.where(mask, jnp.exp(s - m), 0.0)
    return e / jnp.maximum(jnp.sum(e, axis=axis, keepdims=True), TINY)


def _flash_step(s, v, m_ref, acc_ref, base2):
    nk = s.shape[1]
    ex = jnp.exp2 if base2 else jnp.exp
    m_old = m_ref[...]
    m_new = jnp.maximum(m_old, jnp.max(s, axis=1, keepdims=True))
    alpha = ex(m_old - m_new)
    e = jnp.concatenate([ex(s[:, c * LANES:(c + 1) * LANES] - m_new) for c in range(nk // LANES)], axis=1)
    acc_ref[...] = alpha * acc_ref[...] + _dot(e.astype(BF16), v)
    m_ref[...] = m_new


def _hi_over_lo(acc):
    return acc / jnp.maximum(pltpu.roll(acc, HD, 1), TINY)


def _pair_heads(o_even, o_odd):
    lane = _iota(o_even.shape, 1)
    return jnp.where(lane < HD, pltpu.roll(o_even, HD, 1), o_odd)


def _cmp_to_slc_np(n_slots, first_token):
    tok = np.arange(n_slots)[:, None] + first_token
    start = tok * CMP_STRIDE
    bstart = np.arange(LANES)[None, :] * SLC_BLOCK
    m = (start < bstart + SLC_BLOCK) & (start + CMP_LEN > bstart) & (tok >= 0)
    return m.astype(np.float32)


def _nsa_sel_kernel(slopes_ref, q_ref, cmp_ref, cmpt_ref, c2st_ref, oc_ref, sel_ref, *, tq, n_sel):
    g = pl.program_id(1)
    t0 = pl.program_id(2) * tq
    n_slots = cmp_ref.shape[2]
    qpos = t0 + _iota((1, tq), 1)
    cend = _iota((n_slots, 1), 0) * CMP_STRIDE + (CMP_LEN - 1)
    d_c = qpos - cend
    mask_c = d_c >= 0
    d_cf = d_c.astype(F32)
    q = q_ref[...]
    cmp = cmp_ref[0, 0]
    cmpt = cmpt_ref[0, 0]
    c2st = c2st_ref[...]
    imp_t = jnp.zeros((LANES, tq), F32)
    for h in range(HG):
        q128 = _dot(q, _pick_head(HG * HD, h * HD)).astype(BF16)
        s_t = _dot_nt(cmp, q128) - slopes_ref[g * HG + h] * d_cf
        p_t = _masked_softmax(s_t, mask_c, axis=0).astype(BF16)
        oc_ref[0, 0, h] = _dot(cmpt, p_t).T
        imp_t = imp_t + _dot(c2st, p_t)
    blk = _iota((LANES, tq), 0)
    cur = qpos >> 6
    forced = (blk == 0) | (blk == cur) | (blk == cur - 1)
    valid = blk * SLC_BLOCK <= qpos
    val = jnp.where(valid, imp_t + jnp.where(forced, FORCE, 0.0), NEG)
    sel_t = jnp.where(valid, _topk_mask(val, n_sel, 0), 0.0)
    sel_ref[0, 0] = ((sel_t - 1.0) * MASK_BIG).T.astype(sel_ref.dtype)


def _nsa_select(slopes, qn, cmp, B, T):
    tq = min(128, T)
    nq = T // tq
    n_slots = cmp.shape[2]
    n_slc = -(-T // SLC_BLOCK)
    assert n_slc <= LANES and SLC_BLOCK == 64
    c2st = jnp.asarray(_cmp_to_slc_np(n_slots, 0).T, BF16)
    cmpt = jnp.swapaxes(cmp, 2, 3)
    return pl.pallas_call(
        functools.partial(_nsa_sel_kernel, tq=tq, n_sel=min(SLC_TOPK, n_slc)),
        grid=(B, NSA_G, nq),
        in_specs=[pl.BlockSpec(memory_space=pltpu.SMEM),
                  pl.BlockSpec((tq, HG * HD), lambda b, g, j: (b * nq + j, g)),
                  pl.BlockSpec((1, 1, n_slots, LANES), lambda b, g, j: (b, g, 0, 0)),
                  pl.BlockSpec((1, 1, LANES, n_slots), lambda b, g, j: (b, g, 0, 0)),
                  pl.BlockSpec((LANES, n_slots), lambda b, g, j: (0, 0))],
        out_specs=[pl.BlockSpec((1, 1, HG, tq, LANES), lambda b, g, j: (b, g, 0, j, 0)),
                   pl.BlockSpec((1, 1, tq, LANES), lambda b, g, j: (b, g, j, 0))],
        out_shape=[jax.ShapeDtypeStruct((B, NSA_G, HG, T, LANES), F32),
                   jax.ShapeDtypeStruct((B, NSA_G, T, LANES), BF16)],
        compiler_params=_cp("parallel", "parallel", "parallel"),
        name="nsa_select",
    )(slopes, qn, cmp, cmpt, c2st)


def _tile_bits(selneg, B, T, tq, tk):
    nq = T // tq
    n_tiles = T // tk
    bpt = tk // SLC_BLOCK
    assert n_tiles <= 32
    anyb = (selneg.reshape(B, NSA_G, nq, tq, LANES) == 0).any(axis=3)
    anyt = anyb[..., :n_tiles * bpt].reshape(B, NSA_G, nq, n_tiles, bpt).any(axis=-1)
    w = jnp.sum(anyt.astype(jnp.uint32) << jnp.arange(n_tiles, dtype=jnp.uint32), axis=-1, dtype=jnp.uint32)
    return lax.bitcast_convert_type(w, jnp.int32).reshape(-1)


def _nsa_attn_kernel(bits_ref, slopes_ref, q_ref, ks_ref, vs_ref, kw_ref, vw_ref, blk1h_ref, sel_ref, oc_ref,
                     gates_ref, o_ref, m_s, acc_s, *, tq, tk, wk):
    b, g, j = pl.program_id(0), pl.program_id(1), pl.program_id(2)
    t0 = j * tq
    q = q_ref[...]
    selneg = sel_ref[0, 0]
    lane = _iota((tq, LANES), 1)
    slopes = [slopes_ref[g * HG + h] for h in range(HG)]
    q_sel, q_win = [], []
    for h in range(HG):
        q128 = _dot(q, _pick_head(HG * HD, h * HD))
        q_win.append(q128.astype(BF16))
        q_sel.append(jnp.concatenate([jnp.where(lane == HD, slopes[h], q128).astype(BF16), selneg], axis=1))
    q4 = jnp.concatenate(q_sel, axis=0)
    qw4 = jnp.concatenate(q_win, axis=0)
    qpos = t0 + _iota((tq, 1), 0)

    m_s[...] = jnp.full(m_s.shape, NEG, F32)
    acc_s[...] = jnp.zeros(acc_s.shape, F32)
    word = bits_ref[(b * NSA_G + g) * pl.num_programs(2) + j]

    def tile(kt, diagonal):
        k0 = pl.multiple_of(kt * tk, tk)
        kop = jnp.concatenate([ks_ref[pl.ds(k0, tk), :], blk1h_ref[pl.ds(k0, tk), :]], axis=1)
        vop = vs_ref[pl.ds(k0, tk), :]
        s = _dot_nt(q4, kop)
        off = (k0 - t0).astype(F32)
        if diagonal:
            causal = (k0 + _iota((1, tk), 1)) <= qpos
        for h in range(HG):
            sh = s[h * tq:(h + 1) * tq] + slopes[h] * off
            if diagonal:
                sh = jnp.where(causal, sh, NEG)
            _flash_step(sh, vop, m_s.at[h], acc_s.at[h], False)

    def body(kt, carry):
        @pl.when(((word >> kt) & 1) == 1)
        def _():
            tile(kt, False)
        return carry

    n_before = t0 // tk
    lax.fori_loop(0, n_before, body, 0)
    tile(n_before, True)

    w0 = pl.multiple_of(jnp.maximum(t0 + tq - wk, 0), tq)
    kw = kw_ref[pl.ds(w0, wk), :]
    vw = vw_ref[pl.ds(w0, wk), :]
    d_w = qpos - (w0 + _iota((1, wk), 1))
    mask_w = (d_w >= 0) & (d_w < WINDOW)
    d_wf = d_w.astype(F32)
    s_w = _dot_nt(qw4, kw)
    gates = gates_ref[0]
    o_heads = []
    for h in range(HG):
        sw = jnp.where(mask_w, s_w[h * tq:(h + 1) * tq] - slopes[h] * d_wf, NEG)
        e_w = jnp.where(mask_w, jnp.exp(sw - jnp.max(sw, axis=1, keepdims=True)), 0.0)
        o_w = _hi_over_lo(_dot(e_w.astype(BF16), vw))
        o_s = _hi_over_lo(acc_s[h])
        o_heads.append(gates[:, 3 * h:3 * h + 1] * oc_ref[0, 0, h] + gates[:, 3 * h + 1:3 * h + 2] * o_s
                       + gates[:, 3 * h + 2:3 * h + 3] * o_w)
    o_ref[...] = jnp.concatenate([_pair_heads(o_heads[0], o_heads[1]), _pair_heads(o_heads[2], o_heads[3])],
                                 axis=1).astype(o_ref.dtype)


def _nsa_attend(slopes, qn, pr, selneg, oc, gates_g, B, T):
    tq = min(128, T)
    nq = T // tq
    tk = min(SEL_TK, T)
    assert tk == SEL_TK and HG == 4
    wk = min(WINDOW + tq, T)
    bits = _tile_bits(selneg, B, T, tq, tk)
    blk1h = jnp.asarray(np.arange(T)[:, None] // SLC_BLOCK == np.arange(LANES)[None, :], BF16)
    kv_spec = pl.BlockSpec((T, LANES), lambda b, g, j, bits: (b, g))
    return pl.pallas_call(
        functools.partial(_nsa_attn_kernel, tq=tq, tk=tk, wk=wk),
        grid_spec=pltpu.PrefetchScalarGridSpec(
            num_scalar_prefetch=1, grid=(B, NSA_G, nq),
            in_specs=[pl.BlockSpec(memory_space=pltpu.SMEM),
                      pl.BlockSpec((tq, HG * HD), lambda b, g, j, bits: (b * nq + j, g)),
                      kv_spec, kv_spec, kv_spec, kv_spec,
                      pl.BlockSpec((T, LANES), lambda b, g, j, bits: (0, 0)),
                      pl.BlockSpec((1, 1, tq, LANES), lambda b, g, j, bits: (b, g, j, 0)),
                      pl.BlockSpec((1, 1, HG, tq, LANES), lambda b, g, j, bits: (b, g, 0, j, 0)),
                      pl.BlockSpec((1, tq, LANES), lambda b, g, j, bits: (g, b * nq + j, 0))],
            out_specs=pl.BlockSpec((tq, HG * HD), lambda b, g, j, bits: (b * nq + j, g)),
            scratch_shapes=[pltpu.VMEM((HG, tq, LANES), F32), pltpu.VMEM((HG, tq, LANES), F32)]),
        out_shape=jax.ShapeDtypeStruct((B * T, QW), BF16),
        compiler_params=_cp("parallel", "parallel", "parallel"),
        name="nsa_attend",
    )(bits, slopes, qn, pr["ks_op"], pr["vs_op"], pr["kw_op"], pr["vw_op"], blk1h, selneg, oc, gates_g)


FOX_ROWS = 256


def _fox_kernel(q_ref, k0_ref, k1_ref, v0_ref, v1_ref, d0_ref, d1_ref, o_ref, m_s, acc_s, *, tq, rows):
    j = pl.program_id(2)
    n_part = tq // rows
    k_refs, v_refs, d_refs = (k0_ref, k1_ref), (v0_ref, v1_ref), (d0_ref, d1_ref)
    m_s[...] = jnp.full(m_s.shape, NEG, F32)
    acc_s[...] = jnp.zeros(acc_s.shape, F32)

    def step(kt, diagonal):
        k0 = pl.multiple_of(kt * tq, tq)
        for hh in range(2):
            k = k_refs[hh][pl.ds(k0, tq), :]
            v = v_refs[hh][pl.ds(k0, tq), :]
            bias = d_refs[hh][0, kt] * LOG2E
            for r in range(n_part):
                s = _dot_nt(q_ref[r * rows:(r + 1) * rows, :], k) - bias
                if diagonal:
                    s = jnp.where(_iota((rows, tq), 1) <= _iota((rows, tq), 0) + r * rows, s, NEG)
                _flash_step(s, v, m_s.at[hh * n_part + r], acc_s.at[hh * n_part + r], True)

    def body(kt, carry):
        step(kt, False)
        return carry

    lax.fori_loop(0, j, body, 0)
    step(j, True)
    for r in range(n_part):
        o_ref[r * rows:(r + 1) * rows, :] = _pair_heads(_hi_over_lo(acc_s[r]),
                                                         _hi_over_lo(acc_s[n_part + r])).astype(o_ref.dtype)


def _fox_attend(qf, kf_op, vf_op, d_t, B, T):
    tq = min(512, T)
    rows = min(FOX_ROWS, tq)
    nq = T // tq
    d4 = d_t.reshape(B * FOX_H, nq, 1, tq)
    kv_spec = lambda hh: pl.BlockSpec((T, LANES), lambda b, hp, j: (b, 2 * hp + hh))
    d_spec = lambda hh: pl.BlockSpec((1, nq, 1, tq), lambda b, hp, j: (b * FOX_H + 2 * hp + hh, 0, 0, 0))
    n_chain = 2 * (tq // rows)
    return pl.pallas_call(
        functools.partial(_fox_kernel, tq=tq, rows=rows),
        grid=(B, FOX_H // 2, nq),
        in_specs=[pl.BlockSpec((tq, 2 * HD), lambda b, hp, j: (b * nq + j, hp)),
                  kv_spec(0), kv_spec(1), kv_spec(0), kv_spec(1), d_spec(0), d_spec(1)],
        out_specs=pl.BlockSpec((tq, 2 * HD), lambda b, hp, j: (b * nq + j, hp)),
        out_shape=jax.ShapeDtypeStruct((B * T, FOX_H * HD), BF16),
        scratch_shapes=[pltpu.VMEM((n_chain, rows, LANES), F32), pltpu.VMEM((n_chain, rows, LANES), F32)],
        compiler_params=_cp("parallel", "parallel", "parallel"),
        name="fox_attend",
    )(qf, kf_op, kf_op, vf_op, vf_op, d4, d4)


def _out_kernel(on_ref, of_ref, x_ref, g1_ref, sc2_ref, sh2_ref, gpm_ref, gpf_ref, wo_ref, wr_ref, br_ref,
                x1_ref, h2_ref, gd_ref):
    y = _dot(on_ref[...], wo_ref[:QW]) + _dot(of_ref[...], wo_ref[QW:])
    x1 = x_ref[...] + g1_ref[...] * _rms(y, gpm_ref[...])
    x1_ref[...] = x1
    h2 = (_rms(x1, gpf_ref[...]) * (1.0 + sc2_ref[...]) + sh2_ref[...]).astype(BF16)
    h2_ref[...] = h2
    scores = jax.nn.sigmoid(_dot(h2, wr_ref[...]))
    sel = _topk_mask(scores + br_ref[...], TOP_K, 1)
    gsel = sel * scores
    gd_ref[...] = gsel / jnp.sum(gsel, axis=1, keepdims=True) * ROUTED_SCALE


def _out_proj(o_n, o_f, x2, gate1, scale2, shift2, g_post_mix, g_pre_ffn, wo, wr, br, rows_per_mod):
    n, d = x2.shape
    tm = min(256, n)
    if rows_per_mod >= tm:
        per = rows_per_mod // tm
        mod_spec = pl.BlockSpec((None, 1, d), lambda i: (i // per, 0, 0))
    else:
        gate1, scale2, shift2 = (jnp.repeat(a[:, 0, :], rows_per_mod, axis=0) for a in (gate1, scale2, shift2))
        mod_spec = pl.BlockSpec((tm, d), lambda i: (i, 0))
    row = lambda w: pl.BlockSpec((tm, w), lambda i: (i, 0))
    full = lambda a: pl.BlockSpec(a.shape, lambda i: (0,) * a.ndim)
    gpm, gpf = g_post_mix.reshape(1, d), g_pre_ffn.reshape(1, d)
    return pl.pallas_call(
        _out_kernel,
        grid=(n // tm,),
        in_specs=[row(QW), row(QW), row(d), mod_spec, mod_spec, mod_spec, full(gpm), full(gpf),
                  full(wo), full(wr), full(br)],
        out_specs=[row(d), row(d), row(LANES)],
        out_shape=[jax.ShapeDtypeStruct((n, d), F32), jax.ShapeDtypeStruct((n, d), BF16),
                   jax.ShapeDtypeStruct((n, LANES), F32)],
        compiler_params=_cp("parallel"),
        name="out_proj_router",
    )(o_n, o_f, x2, gate1, scale2, shift2, gpm, gpf, wo, wr, br)


def _moe_kernel(h_ref, gd_ref, wg_ref, wu_ref, wd_ref, o_ref):
    e = pl.program_id(1)

    @pl.when(e == 0)
    def _():
        o_ref[...] = jnp.zeros_like(o_ref)

    gd = gd_ref[...]
    gcol = jnp.sum(jnp.where(_iota(gd.shape, 1) == e, gd, 0.0), axis=1, keepdims=True)

    @pl.when(jnp.max(gcol) > 0.0)
    def _():
        h = h_ref[...]
        a = _dot(h, wg_ref[0])
        hb = (a * jax.nn.sigmoid(a) * _dot(h, wu_ref[0])).astype(BF16)
        o_ref[...] += _dot(hb, wd_ref[0]) * gcol


def _moe_routed(h2, gd, wg, wu, wd):
    n, d = h2.shape
    n_e, _, f = wg.shape
    tm = min(2048, n)
    return pl.pallas_call(
        _moe_kernel,
        grid=(n // tm, n_e),
        in_specs=[pl.BlockSpec((tm, d), lambda i, e: (i, 0)),
                  pl.BlockSpec((tm, LANES), lambda i, e: (i, 0)),
                  pl.BlockSpec((1, d, f), lambda i, e: (e, 0, 0)),
                  pl.BlockSpec((1, d, f), lambda i, e: (e, 0, 0)),
                  pl.BlockSpec((1, f, d), lambda i, e: (e, 0, 0))],
        out_specs=pl.BlockSpec((tm, d), lambda i, e: (i, 0)),
        out_shape=jax.ShapeDtypeStruct((n, d), F32),
        compiler_params=_cp("parallel", "arbitrary"),
        name="moe_routed",
    )(h2, gd, wg, wu, wd)


def _final_kernel(r_ref, h_ref, x1_ref, g2_ref, gpf_ref, wg_ref, wu_ref, wd_ref, y_ref):
    h = h_ref[...]
    a = _dot(h, wg_ref[...])
    hb = (a * jax.nn.sigmoid(a) * _dot(h, wu_ref[...])).astype(BF16)
    f = r_ref[...] + _dot(hb, wd_ref[...])
    y_ref[...] = x1_ref[...] + g2_ref[...] * _rms(f, gpf_ref[...])


def _final(routed, h2, x1, gate2, g_post_ffn, wsg, wsu, wsd, rows_per_mod):
    n, d = x1.shape
    tm = min(512, n)
    if rows_per_mod >= tm:
        per = rows_per_mod // tm
        mod_spec = pl.BlockSpec((None, 1, d), lambda i: (i // per, 0, 0))
    else:
        gate2 = jnp.repeat(gate2[:, 0, :], rows_per_mod, axis=0)
        mod_spec = pl.BlockSpec((tm, d), lambda i: (i, 0))
    row = pl.BlockSpec((tm, d), lambda i: (i, 0))
    full = lambda a: pl.BlockSpec(a.shape, lambda i: (0,) * a.ndim)
    gpf = g_post_ffn.reshape(1, d)
    return pl.pallas_call(
        _final_kernel,
        grid=(n // tm,),
        in_specs=[row, row, row, mod_spec, full(gpf), full(wsg), full(wsu), full(wsd)],
        out_specs=row,
        out_shape=jax.ShapeDtypeStruct((n, d), F32),
        compiler_params=_cp("parallel"),
        name="shared_ffn_final",
    )(routed, h2, x1, gate2, gpf, wsg, wsu, wsd)


PAGES_PER_STEP = 16


def _page_specs(block, n_pages, pp):
    def spec(jj):
        return pl.BlockSpec(block, lambda b, c, pt: (pt[b * n_pages + c * pp + jj],) + (0,) * (len(block) - 1))
    return [spec(jj) for jj in range(pp)]


def _cmp_sample_kernel(pt_ref, *refs, pp):
    pages = refs[:pp]
    w1_ref, bias_ref, w2_ref, o_ref, carry_ref, x_s = refs[pp:]
    c = pl.program_id(1)
    page = pages[0].shape[-1]
    rows = pp * page // CMP_STRIDE

    @pl.when(c == 0)
    def _():
        carry_ref[...] = jnp.zeros_like(carry_ref)

    for jj, p in enumerate(pages):
        for kv in range(2):
            x_s[kv, jj * page:(jj + 1) * page, :] = p[0, kv].reshape(NSA_G * HD, page).T

    acc = [jnp.zeros((rows, 4 * CMP_HID), F32) for _ in range(2)]
    for t in range(CMP_STRIDE):
        for kv in range(2):
            x_t = x_s[kv, pl.ds(t, rows, stride=CMP_STRIDE), :]
            acc[kv] = acc[kv] + _dot(x_t.astype(BF16), w1_ref[kv, t])
    out = [None, None]
    for kv in range(2):
        a = acc[kv][:, :2 * CMP_HID]
        b = acc[kv][:, 2 * CMP_HID:]
        a_prev = pltpu.roll(a, 1, 0)
        a_prev = jnp.where(_iota(a.shape, 0) == 0, carry_ref[kv][0:1], a_prev)
        carry_ref[kv] = jnp.broadcast_to(a[rows - 1:rows], carry_ref.shape[1:])
        hid = jax.nn.gelu(a_prev + b + bias_ref[kv]).astype(BF16)
        for g in range(NSA_G):
            term = _dot(hid[:, g * CMP_HID:(g + 1) * CMP_HID], w2_ref[kv, g])
            out[g] = term if out[g] is None else out[g] + term
    for g in range(NSA_G):
        o_ref[0, g] = out[g].astype(o_ref.dtype)


def _cmp_sample_weights(w_cmp1, b_cmp1, w_cmp2, pe_cmp):
    w1 = w_cmp1.reshape(2, 2, CMP_STRIDE, HD, CMP_HID)
    z = jnp.zeros_like(w1[:, 0])
    top = jnp.concatenate([w1[:, 0], z, w1[:, 1], z], axis=-1)
    bot = jnp.concatenate([z, w1[:, 0], z, w1[:, 1]], axis=-1)
    w1bd = jnp.concatenate([top, bot], axis=2).astype(BF16)
    bias = (jnp.einsum("kf,kfh->kh", pe_cmp.reshape(2, -1).astype(BF16), w_cmp1.astype(BF16),
                       preferred_element_type=F32) + b_cmp1)
    bias = jnp.concatenate([bias, bias], axis=-1).reshape(2, 1, 2 * CMP_HID)
    z2 = jnp.zeros((CMP_HID, HD), F32)
    lo = lambda w: jnp.concatenate([w, z2], axis=1)
    hi = lambda w: jnp.concatenate([z2, w], axis=1)
    w2 = jnp.stack([jnp.stack([lo(w_cmp2[0]), hi(w_cmp2[0])]),
                    jnp.stack([hi(w_cmp2[1]), lo(w_cmp2[1])])]).astype(BF16)
    return w1bd, bias, w2


def _compress_sample(cache, pt_flat, Bs, n_pages, csw):
    w1bd, bias, w2 = csw
    pp = min(PAGES_PER_STEP, n_pages)
    page = cache.shape[-1]
    rows = pp * (page // CMP_STRIDE)
    n_slots = n_pages * (page // CMP_STRIDE)
    full = lambda a: pl.BlockSpec(a.shape, lambda b, c, pt: (0,) * a.ndim)
    return pl.pallas_call(
        functools.partial(_cmp_sample_kernel, pp=pp),
        grid_spec=pltpu.PrefetchScalarGridSpec(
            num_scalar_prefetch=1, grid=(Bs, n_pages // pp),
            in_specs=_page_specs((1,) + cache.shape[1:], n_pages, pp) + [full(w1bd), full(bias), full(w2)],
            out_specs=pl.BlockSpec((1, NSA_G, rows, LANES), lambda b, c, pt: (b, 0, c, 0)),
            scratch_shapes=[pltpu.VMEM((2, 8, 2 * CMP_HID), F32), pltpu.VMEM((2, pp * page, LANES), F32)]),
        out_shape=jax.ShapeDtypeStruct((Bs, NSA_G, n_slots, LANES), BF16),
        compiler_params=_cp("parallel", "arbitrary"),
        name="compress_sample",
    )(pt_flat, *([cache] * pp), w1bd, bias, w2)


def _logf_sample_kernel(pt_ref, *refs, pp):
    pages = refs[:pp]
    o_ref, carry_ref = refs[pp:]

    @pl.when(pl.program_id(1) == 0)
    def _():
        carry_ref[...] = jnp.zeros_like(carry_ref)

    page = pages[0].shape[2]
    per = max(1, 512 // page)
    for c0 in range(0, pp, per):
        x = jnp.concatenate([p[0] for p in pages[c0:c0 + per]], axis=1)
        d = _cumsum_block(x) + carry_ref[:, 0:1]
        o_ref[0, :, c0 * page:(c0 + per) * page] = d
        carry_ref[...] = jnp.broadcast_to(d[:, d.shape[1] - 1:], carry_ref.shape)


def _cumsum_sample(logf_cache_t, pt_flat, Bs, n_pages):
    pp = min(PAGES_PER_STEP, n_pages)
    page = logf_cache_t.shape[2]
    return pl.pallas_call(
        functools.partial(_logf_sample_kernel, pp=pp),
        grid_spec=pltpu.PrefetchScalarGridSpec(
            num_scalar_prefetch=1, grid=(Bs, n_pages // pp),
            in_specs=_page_specs((1, FOX_H, page), n_pages, pp),
            out_specs=pl.BlockSpec((1, FOX_H, pp * page), lambda b, c, pt: (b, 0, c)),
            scratch_shapes=[pltpu.VMEM((FOX_H, LANES), F32)]),
        out_shape=jax.ShapeDtypeStruct((Bs, FOX_H, n_pages * page), F32),
        compiler_params=_cp("parallel", "arbitrary"),
        name="cumsum_sample",
    )(pt_flat, *([logf_cache_t] * pp))


ROWS_S = NSA_G * HG * 8


def _sel_sample_kernel(q_ref, cmp_ref, c2s_ref, slope_ref, qidx_ref, oc_ref, sel_ref, *, past, n_take):
    n_slots = cmp_ref.shape[2]
    rg = ROWS_S // NSA_G
    slot = _iota((1, n_slots), 1)
    cend = slot * CMP_STRIDE + (CMP_STRIDE - 1)
    c2s = c2s_ref[...]
    blk = _iota((8, LANES), 1)
    n_blk = past // SLC_BLOCK
    for g in range(NSA_G):
        r = slice(g * rg, (g + 1) * rg)
        q = q_ref[0, r]
        cmp = cmp_ref[0, g]
        qpos = past + qidx_ref[r]
        d_c = qpos - cend
        mask = (d_c >= 0) & (slot >= 1)
        s = _dot_nt(q, cmp) - slope_ref[r] * d_c.astype(F32)
        pb = _masked_softmax(s, mask).astype(BF16)
        oc_ref[0, r] = pltpu.roll(_dot(pb, cmp), HD, 1)
        imp_h = _dot(pb, c2s)
        imp = imp_h[0:8] + imp_h[8:16] + imp_h[16:24] + imp_h[24:32]
        forced = (blk == 0) | (blk == n_blk - 1)
        val = jnp.where(blk < n_blk, imp + jnp.where(forced, FORCE, 0.0), NEG)
        sel8 = _topk_mask(val, n_take, 1)
        sel_ref[0, r] = jnp.concatenate([sel8] * HG, axis=0).astype(sel_ref.dtype)


def _select_sample(qs, cmp_s, slope_rows, qidx_rows, Bs, past):
    n_slots = cmp_s.shape[2]
    n_slc = past // SLC_BLOCK + 1
    assert past % SLC_BLOCK == 0 and n_slc - 1 <= LANES
    c2s = jnp.asarray(_cmp_to_slc_np(n_slots, -1), BF16)
    full = lambda a: pl.BlockSpec(a.shape, lambda b: (0,) * a.ndim)
    return pl.pallas_call(
        functools.partial(_sel_sample_kernel, past=past, n_take=min(SLC_TOPK, n_slc) - 1),
        grid=(Bs,),
        in_specs=[pl.BlockSpec((1, ROWS_S, LANES), lambda b: (b, 0, 0)),
                  pl.BlockSpec((1, NSA_G, n_slots, LANES), lambda b: (b, 0, 0, 0)),
                  full(c2s), full(slope_rows), full(qidx_rows)],
        out_specs=[pl.BlockSpec((1, ROWS_S, LANES), lambda b: (b, 0, 0)),
                   pl.BlockSpec((1, ROWS_S, LANES), lambda b: (b, 0, 0))],
        out_shape=[jax.ShapeDtypeStruct((Bs, ROWS_S, LANES), F32),
                   jax.ShapeDtypeStruct((Bs, ROWS_S, LANES), BF16)],
        compiler_params=_cp("parallel"),
        name="select_sample",
    )(qs, cmp_s, c2s, slope_rows, qidx_rows)


def _online_update(s, ok, v, m_s, l_s, acc_s, base2=False, v_t=False):
    ex = jnp.exp2 if base2 else jnp.exp
    s = jnp.where(ok, s, NEG)
    m_old = m_s[...]
    m_new = jnp.maximum(m_old, jnp.max(s, axis=1, keepdims=True))
    alpha = ex(m_old - m_new)
    e = jnp.where(ok, ex(s - m_new), 0.0)
    l_s[...] = alpha * l_s[...] + jnp.sum(e, axis=1, keepdims=True)
    acc_s[...] = alpha * acc_s[...] + (_dot_nt if v_t else _dot)(e.astype(BF16), v)
    m_s[...] = m_new


def _pages_t(pages, kv, width):
    return jnp.concatenate([p[0, kv].reshape(width, p.shape[-1]).astype(BF16) for p in pages], axis=1)


def _pad_rows8(x):
    return jnp.concatenate([x, jnp.zeros((8 - x.shape[0], x.shape[1]), x.dtype)], axis=0)


def _slc_sample_kernel(pt_ref, *refs, pp, past, t_new):
    pages = refs[:pp]
    q_ref, sel_ref, new_ref, slope_ref, qidx_ref, o_ref, m_s, l_s, acc_s = refs[pp:]
    c = pl.program_id(1)
    page = pages[0].shape[-1]
    nk = pp * page

    @pl.when(c == 0)
    def _():
        m_s[...] = jnp.full(m_s.shape, NEG, F32)
        l_s[...] = jnp.zeros(l_s.shape, F32)
        acc_s[...] = jnp.zeros(acc_s.shape, F32)

    q = q_ref[0]
    slope = slope_ref[...]
    qidx = qidx_ref[...]
    kpos = c * nk + _iota((1, nk), 1)
    expand = _onehot(_iota((LANES, 1), 0) == (kpos >> 6))
    ok = _dot(sel_ref[0], expand) > 0.5

    @pl.when(jnp.max(jnp.where(ok, 1.0, 0.0)) > 0.0)
    def _():
        dist = (past + qidx - kpos).astype(F32)
        s = _dot(q, _pages_t(pages, 0, NSA_G * HD)) - slope * dist
        _online_update(s, ok, _pages_t(pages, 1, NSA_G * HD), m_s, l_s, acc_s, v_t=True)

    @pl.when(c == pl.num_programs(1) - 1)
    def _():
        new = _pad_rows8(new_ref[0])
        kj = _iota((1, 8), 1)
        ok_new = (kj <= qidx) & (kj < t_new)
        s_new = _dot_nt(q, new[:, :LANES].astype(BF16)) - slope * (qidx - kj).astype(F32)
        _online_update(s_new, ok_new, new[:, LANES:].astype(BF16), m_s, l_s, acc_s)
        o_ref[0] = acc_s[...] / l_s[...]


def _slc_sample(cache, pt_flat, qs, sel_rows, kvs_new, slope_rows, qidx_rows, Bs, n_pages, past):
    pp = min(PAGES_PER_STEP, n_pages)
    t_new = kvs_new.shape[1]
    full = lambda a: pl.BlockSpec(a.shape, lambda b, c, pt: (0,) * a.ndim)
    per_seq = lambda a: pl.BlockSpec((1,) + a.shape[1:], lambda b, c, pt: (b,) + (0,) * (a.ndim - 1))
    return pl.pallas_call(
        functools.partial(_slc_sample_kernel, pp=pp, past=past, t_new=t_new),
        grid_spec=pltpu.PrefetchScalarGridSpec(
            num_scalar_prefetch=1, grid=(Bs, n_pages // pp),
            in_specs=_page_specs((1,) + cache.shape[1:], n_pages, pp)
            + [per_seq(qs), per_seq(sel_rows), per_seq(kvs_new), full(slope_rows), full(qidx_rows)],
            out_specs=pl.BlockSpec((1, ROWS_S, LANES), lambda b, c, pt: (b, 0, 0)),
            scratch_shapes=[pltpu.VMEM((ROWS_S, 1), F32), pltpu.VMEM((ROWS_S, 1), F32),
                            pltpu.VMEM((ROWS_S, LANES), F32)]),
        out_shape=jax.ShapeDtypeStruct((Bs, ROWS_S, LANES), F32),
        compiler_params=_cp("parallel", "arbitrary"),
        name="selected_sample",
    )(pt_flat, *([cache] * pp), qs, sel_rows, kvs_new, slope_rows, qidx_rows)


def _win_sample_kernel(q_ref, win_ref, new_ref, oc_ref, os_ref, gates_ref, slope_ref, qidx_ref, o_ref, *, t_new):
    q = q_ref[0]
    n_win = win_ref.shape[-1]
    k_t = win_ref[0, 0].reshape(NSA_G * HD, n_win).astype(BF16)
    v_t = win_ref[0, 1].reshape(NSA_G * HD, n_win).astype(BF16)
    new = _pad_rows8(new_ref[0])
    kj = _iota((1, n_win + 8), 1)
    d_w = n_win + qidx_ref[...] - kj
    mask = (d_w >= 0) & (d_w < WINDOW) & (kj < n_win + t_new)
    s = jnp.concatenate([_dot(q, k_t), _dot_nt(q, new[:, :LANES].astype(BF16))], axis=1)
    p = _masked_softmax(s - slope_ref[...] * d_w.astype(F32), mask).astype(BF16)
    o_w = _dot_nt(p[:, :n_win], v_t) + _dot(p[:, n_win:], new[:, LANES:].astype(BF16))
    gates = gates_ref[0]
    o_ref[0] = gates[:, 0:1] * oc_ref[0] + gates[:, 1:2] * os_ref[0] + gates[:, 2:3] * o_w


def _win_sample(qs, state_win, kvw_new, oc, o_s, gate_rows, slope_rows, qidx_rows, Bs):
    t_new = kvw_new.shape[1]
    full = lambda a: pl.BlockSpec(a.shape, lambda b: (0,) * a.ndim)
    per_seq = lambda a: pl.BlockSpec((1,) + a.shape[1:], lambda b: (b,) + (0,) * (a.ndim - 1))
    return pl.pallas_call(
        functools.partial(_win_sample_kernel, t_new=t_new),
        grid=(Bs,),
        in_specs=[per_seq(qs), per_seq(state_win), per_seq(kvw_new), per_seq(oc), per_seq(o_s),
                  per_seq(gate_rows), full(slope_rows), full(qidx_rows)],
        out_specs=pl.BlockSpec((1, ROWS_S, LANES), lambda b: (b, 0, 0)),
        out_shape=jax.ShapeDtypeStruct((Bs, ROWS_S, LANES), F32),
        compiler_params=_cp("parallel"),
        name="window_sample",
    )(qs, state_win, kvw_new, oc, o_s, gate_rows, slope_rows, qidx_rows)


ROWS_F = FOX_H * 8
FOX_K = FOX_H * HD


def _fox_sample_kernel(pt_ref, *refs, pp, t_new):
    pages = refs[:pp]
    q_ref, d_ref, new_ref, dnew_ref, qidx_ref, o_ref, m_s, l_s, acc_s = refs[pp:]
    c = pl.program_id(1)

    @pl.when(c == 0)
    def _():
        m_s[...] = jnp.full(m_s.shape, NEG, F32)
        l_s[...] = jnp.zeros(l_s.shape, F32)
        acc_s[...] = jnp.zeros(acc_s.shape, F32)

    q = q_ref[0]
    d = d_ref[0]
    nk = d.shape[1]
    bias = jnp.concatenate([jnp.broadcast_to(d[h:h + 1], (8, nk)) for h in range(FOX_H)], axis=0)
    s = _dot(q, _pages_t(pages, 0, FOX_K)) - bias * LOG2E
    _online_update(s, jnp.full((ROWS_F, nk), True), _pages_t(pages, 1, FOX_K), m_s, l_s, acc_s,
                   base2=True, v_t=True)

    @pl.when(c == pl.num_programs(1) - 1)
    def _():
        new = _pad_rows8(new_ref[0])
        kj = _iota((1, 8), 1)
        ok_new = (kj <= qidx_ref[...]) & (kj < t_new)
        s_new = _dot_nt(q, new[:, :FOX_K].astype(BF16)) - dnew_ref[0] * LOG2E
        _online_update(s_new, ok_new, new[:, FOX_K:].astype(BF16), m_s, l_s, acc_s, base2=True)
        o_ref[0] = acc_s[...] / l_s[...]


def _fox_sample(cache, pt_flat, qf_bd, d_past_t, kvf_new, dnew_rows, qidx_f, Bs, n_pages):
    pp = min(PAGES_PER_STEP, n_pages)
    page = cache.shape[-1]
    t_new = kvf_new.shape[1]
    full = lambda a: pl.BlockSpec(a.shape, lambda b, c, pt: (0,) * a.ndim)
    per_seq = lambda a: pl.BlockSpec((1,) + a.shape[1:], lambda b, c, pt: (b,) + (0,) * (a.ndim - 1))
    return pl.pallas_call(
        functools.partial(_fox_sample_kernel, pp=pp, t_new=t_new),
        grid_spec=pltpu.PrefetchScalarGridSpec(
            num_scalar_prefetch=1, grid=(Bs, n_pages // pp),
            in_specs=_page_specs((1,) + cache.shape[1:], n_pages, pp)
            + [per_seq(qf_bd), pl.BlockSpec((1, FOX_H, pp * page), lambda b, c, pt: (b, 0, c)),
               per_seq(kvf_new), per_seq(dnew_rows), full(qidx_f)],
            out_specs=pl.BlockSpec((1, ROWS_F, FOX_K), lambda b, c, pt: (b, 0, 0)),
            scratch_shapes=[pltpu.VMEM((ROWS_F, 1), F32), pltpu.VMEM((ROWS_F, 1), F32),
                            pltpu.VMEM((ROWS_F, FOX_K), F32)]),
        out_shape=jax.ShapeDtypeStruct((Bs, ROWS_F, FOX_K), F32),
        compiler_params=_cp("parallel", "arbitrary"),
        name="fox_sample",
    )(pt_flat, *([cache] * pp), qf_bd, d_past_t, kvf_new, dnew_rows, qidx_f)


def _alibi_slopes():
    return jnp.exp2(-8.0 * jnp.arange(1, NSA_H + 1, dtype=F32) / NSA_H)


def _ffn_tail(o_n, o_f, x2, mod, norms, lw, rows_per_mod):
    gate1, scale2, shift2, gate2 = mod
    g_post_mix, g_pre_ffn, g_post_ffn = norms
    x1, h2, gd = _out_proj(o_n, o_f, x2, gate1, scale2, shift2, g_post_mix, g_pre_ffn,
                           lw["wo"], lw["wr"], lw["br"], rows_per_mod)
    routed = _moe_routed(h2, gd, lw["weg"], lw["weu"], lw["wed"])
    return _final(routed, h2, x1, gate2, g_post_ffn, lw["wsg"], lw["wsu"], lw["wsd"], rows_per_mod)


def _prompt_layer(x, mod, norms, lw):
    B, T, d = x.shape
    n = B * T
    shift1, scale1, gate1, shift2, scale2, gate2 = mod
    g_pre_mix, g_post_mix, g_pre_ffn, g_post_ffn = norms
    x2 = x.reshape(n, d)
    pr = _in_proj(x2, scale1, shift1, g_pre_mix, lw["in_cols"], lw["b_small"], T, "prompt")
    small = pr["small"]
    logf = small[:, N_GATE:N_GATE + FOX_H]
    gates_g = small[:, :N_GATE].reshape(n, NSA_G, HG * 3).transpose(1, 0, 2)
    gates_g = jnp.pad(gates_g, ((0, 0), (0, 0), (0, LANES - HG * 3)))
    slopes = _alibi_slopes()

    cmp = _compress_prompt(pr["kvc"], B, T, lw["cmp_w"])
    oc, selneg = _nsa_select(slopes, pr["qn"], cmp, B, T)
    o_n = _nsa_attend(slopes, pr["qn"], pr, selneg, oc, gates_g, B, T)

    d_t = _cumsum_prompt(logf.reshape(B, T, FOX_H).transpose(0, 2, 1))
    o_f = _fox_attend(pr["qf"], pr["kf_op"], pr["vf_op"], d_t, B, T)

    y = _ffn_tail(o_n, o_f, x2, (gate1, scale2, shift2, gate2), (g_post_mix, g_pre_ffn, g_post_ffn), lw, T)
    win_rows = min(WINDOW, T)
    state = lambda a, heads: jnp.moveaxis(a.reshape(B, 2, heads, HD, -1), -1, 1)
    states = (state(pr["kvc_t"], NSA_G), state(pr["kvs_t"], NSA_G), state(pr["kvf_t"], FOX_H),
              logf.reshape(B, T, FOX_H), state(pr["kvw_t"][:, :, T - win_rows:], NSA_G))
    return y.reshape(B, T, d), states


def _sample_layer(x, mod, norms, lw, caches):
    Bs, Tn, d = x.shape
    n = Bs * Tn
    assert Tn <= 8
    cache_cmp, cache_slc, cache_fox, cache_logf, state_win, page_table = caches
    n_pages = page_table.shape[1]
    page = cache_cmp.shape[1]
    past = n_pages * page
    shift1, scale1, gate1, shift2, scale2, gate2 = mod
    g_pre_mix, g_post_mix, g_pre_ffn, g_post_ffn = norms
    x2 = x.reshape(n, d)
    pr = _in_proj(x2, scale1, shift1, g_pre_mix, lw["in_cols"], lw["b_small"], Tn, "sample")
    small = pr["small"]
    logf = small[:, N_GATE:N_GATE + FOX_H].reshape(Bs, Tn, FOX_H)
    pt_flat = page_table.reshape(-1)
    padq = ((0, 0), (0, 0), (0, 0), (0, 8 - Tn), (0, 0))

    qn = pr["qn"].reshape(Bs, Tn, NSA_G, HG, HD).transpose(0, 2, 3, 1, 4)
    qn = jnp.pad(qn, padq)
    z = jnp.zeros_like(qn[:, 0])
    qs = jnp.stack([jnp.concatenate([qn[:, 0], z], -1), jnp.concatenate([z, qn[:, 1]], -1)], axis=1)
    qs = qs.reshape(Bs, ROWS_S, LANES)
    slope_rows = jnp.broadcast_to(_alibi_slopes().reshape(NSA_H, 1, 1), (NSA_H, 8, 1)).reshape(ROWS_S, 1)
    qidx_rows = jnp.broadcast_to(jnp.arange(8, dtype=jnp.int32).reshape(1, 8, 1), (NSA_H, 8, 1)).reshape(ROWS_S, 1)
    gate_rows = small[:, :N_GATE].reshape(Bs, Tn, NSA_H, 3).transpose(0, 2, 1, 3)
    gate_rows = jnp.pad(gate_rows, ((0, 0), (0, 0), (0, 8 - Tn), (0, LANES - 3))).reshape(Bs, ROWS_S, LANES)

    tok_minor = lambda a: jnp.moveaxis(a, 1, -1)
    cmp_s = _compress_sample(tok_minor(cache_cmp), pt_flat, Bs, n_pages, lw["cmp_sw"])
    oc, sel_rows = _select_sample(qs, cmp_s, slope_rows, qidx_rows, Bs, past)
    o_s = _slc_sample(tok_minor(cache_slc), pt_flat, qs, sel_rows, pr["kvs"].reshape(Bs, Tn, KVW),
                      slope_rows, qidx_rows, Bs, n_pages, past)
    win = tok_minor(state_win)
    o_rows = _win_sample(qs, win, pr["kvw"].reshape(Bs, Tn, KVW), oc, o_s, gate_rows, slope_rows, qidx_rows, Bs)
    o_rows = o_rows.reshape(Bs, NSA_G, HG, 8, NSA_G, HD)[:, :, :, :Tn]
    o_n = jnp.stack([o_rows[:, 0, :, :, 0], o_rows[:, 1, :, :, 1]], axis=1)
    o_n = o_n.transpose(0, 3, 1, 2, 4).reshape(n, QW).astype(BF16)

    qf = jnp.pad(pr["qf"].reshape(Bs, Tn, FOX_H, HD).transpose(0, 2, 1, 3), ((0, 0), (0, 0), (0, 8 - Tn), (0, 0)))
    qf_bd = (qf[:, :, :, None, :] * jnp.eye(FOX_H, dtype=BF16)[None, :, None, :, None]).reshape(Bs, ROWS_F, FOX_K)
    d_past_t = _cumsum_sample(cache_logf.transpose(0, 2, 1), pt_flat, Bs, n_pages)
    d_new = d_past_t[:, :, past - 1:] + jnp.cumsum(logf, axis=1).transpose(0, 2, 1)
    dnew_rows = jnp.broadcast_to(jnp.pad(d_new, ((0, 0), (0, 0), (0, 8 - Tn)))[:, :, None, :],
                                 (Bs, FOX_H, 8, 8)).reshape(Bs, ROWS_F, 8)
    qidx_f = jnp.broadcast_to(jnp.arange(8, dtype=jnp.int32).reshape(1, 8, 1), (FOX_H, 8, 1)).reshape(ROWS_F, 1)
    o_frows = _fox_sample(tok_minor(cache_fox), pt_flat, qf_bd, d_past_t,
                          pr["kvf"].reshape(Bs, Tn, KVF), dnew_rows, qidx_f, Bs, n_pages)
    o_frows = o_frows.reshape(Bs, FOX_H, 8, FOX_H, HD)[:, :, :Tn]
    o_f = jnp.stack([o_frows[:, h, :, h] for h in range(FOX_H)], axis=2)
    o_f = o_f.reshape(n, FOX_H * HD).astype(BF16)

    y = _ffn_tail(o_n, o_f, x2, (gate1, scale2, shift2, gate2), (g_post_mix, g_pre_ffn, g_post_ffn), lw, Tn)
    kvw_new = pr["kvw"].reshape(Bs, Tn, 2, NSA_G, HD)
    states = (pr["kvc"].reshape(Bs, Tn, 2, NSA_G, HD), pr["kvs"].reshape(Bs, Tn, 2, NSA_G, HD),
              pr["kvf"].reshape(Bs, Tn, 2, FOX_H, HD), logf,
              jnp.concatenate([state_win, kvw_new], axis=1)[:, Tn:])
    return y.reshape(Bs, Tn, d), states


def _layer_weights(w_in, b_nsa_gate, b_forget, w_cmp1, b_cmp1, w_cmp2, pe_cmp, w_out, w_router, b_router,
                   w_exp_gate, w_exp_up, w_exp_down, w_sh_gate, w_sh_up, w_sh_down):
    in_cols, b_small = _in_weights(w_in, b_nsa_gate, b_forget)
    n_e = w_router.shape[1]
    wr = jnp.pad(w_router, ((0, 0), (0, LANES - n_e))).astype(BF16)
    br = jnp.concatenate([b_router.astype(F32), jnp.full((LANES - n_e,), NEG_INF, F32)]).reshape(1, LANES)
    return dict(in_cols=in_cols, b_small=b_small,
                cmp_w=_cmp_weights(w_cmp1, b_cmp1, w_cmp2, pe_cmp),
                cmp_sw=_cmp_sample_weights(w_cmp1, b_cmp1, w_cmp2, pe_cmp),
                wo=w_out.astype(BF16), wr=wr, br=br,
                weg=w_exp_gate.astype(BF16), weu=w_exp_up.astype(BF16), wed=w_exp_down.astype(BF16),
                wsg=w_sh_gate.astype(BF16), wsu=w_sh_up.astype(BF16), wsd=w_sh_down.astype(BF16))


def kernel(x_prompt, x_sample, c_prompt, c_sample, cache_cmp_kv, cache_slc_kv, cache_fox_kv, cache_fox_logf,
           state_win_kv, page_table, w_ada, b_ada, g_pre_mix, g_post_mix, g_pre_ffn, g_post_ffn, w_in,
           b_nsa_gate, b_forget, w_cmp1, b_cmp1, w_cmp2, pe_cmp, w_out, w_router, b_router, w_exp_gate,
           w_exp_up, w_exp_down, w_sh_gate, w_sh_up, w_sh_down):
    depth = w_in.shape[0]
    Bp = x_prompt.shape[0]
    xp, xs = x_prompt, x_sample
    st_p = [[] for _ in range(5)]
    st_s = [[] for _ in range(5)]
    for l in range(depth):
        lw = _layer_weights(w_in[l], b_nsa_gate[l], b_forget[l], w_cmp1[l], b_cmp1[l], w_cmp2[l], pe_cmp[l],
                            w_out[l], w_router[l], b_router[l], w_exp_gate[l], w_exp_up[l], w_exp_down[l],
                            w_sh_gate[l], w_sh_up[l], w_sh_down[l])
        m = _modulation(jnp.concatenate([c_prompt, c_sample], axis=0), w_ada[l], b_ada[l])
        mods = jnp.split(m[:, None, :], 6, axis=-1)
        norms = (g_pre_mix[l], g_post_mix[l], g_pre_ffn[l], g_post_ffn[l])
        xp, sp = _prompt_layer(xp, [a[:Bp] for a in mods], norms, lw)
        caches = (cache_cmp_kv[l], cache_slc_kv[l], cache_fox_kv[l], cache_fox_logf[l], state_win_kv[l], page_table)
        xs, ss = _sample_layer(xs, [a[Bp:] for a in mods], norms, lw, caches)
        for lst, a in zip(st_p, sp):
            lst.append(a)
        for lst, a in zip(st_s, ss):
            lst.append(a)
    return (xp, xs, *[jnp.stack(s) for s in st_p], *[jnp.stack(s) for s in st_s])
```

```python
import functools

import numpy as np
import jax
import jax.numpy as jnp
from jax import lax
from jax.experimental import pallas as pl
from jax.experimental.pallas import tpu as pltpu

F32 = jnp.float32
BF16 = jnp.bfloat16

HD = 64
NSA_H = 8
NSA_G = 2
HG = NSA_H // NSA_G
FOX_H = 8
CMP_STRIDE = 16
CMP_LEN = 32
CMP_HID = 128
SLC_BLOCK = 64
SLC_TOPK = 16
WINDOW = 512
TOP_K = 8
ROUTED_SCALE = 2.5
RMS_EPS = 1e-6
NEG = -1e30
TINY = 1e-30
FORCE = 1e4
NEG_INF = float("-inf")

LANES = 128
KVW = 2 * NSA_G * HD
KVF = 2 * FOX_H * HD
QW = NSA_H * HD
VMEM_LIMIT = 56 * 1024 * 1024


def _cp(*sem):
    return pltpu.CompilerParams(dimension_semantics=sem, vmem_limit_bytes=VMEM_LIMIT)


def _rms(x, g):
    return x * lax.rsqrt(jnp.mean(x * x, axis=-1, keepdims=True) + RMS_EPS) * g


def _dot(a, b):
    return jnp.dot(a, b, preferred_element_type=F32)


def _dot_nt(a, b):
    return lax.dot_general(a, b, (((1,), (1,)), ((), ())), preferred_element_type=F32)


def _iota(shape, dim):
    return lax.broadcasted_iota(jnp.int32, shape, dim)


def _onehot(cond):
    return jnp.where(cond, 1.0, 0.0).astype(BF16)


def _pick_head(width_in, h_off):
    r = _iota((width_in, LANES), 0)
    c = _iota((width_in, LANES), 1)
    return _onehot((r == c + h_off) & (c < HD))


def _place_head(width_out, off):
    r = _iota((LANES, width_out), 0)
    c = _iota((LANES, width_out), 1)
    return _onehot((r >= HD) & (c == r - HD + off))


def _split3(x):
    hi = x.astype(BF16)
    r1 = x - hi.astype(F32)
    mid = r1.astype(BF16)
    lo = (r1 - mid.astype(F32)).astype(BF16)
    return hi, mid, lo


def _mod_kernel(c_ref, w_ref, b_ref, o_ref):
    c = c_ref[...]
    a = (c * jax.nn.sigmoid(c)).astype(BF16)
    o_ref[...] = _dot(a, w_ref[...].astype(BF16)) + b_ref[...]


def _modulation(c_all, w_ada, b_ada):
    nb, d = c_all.shape
    n_chunk = w_ada.shape[1] // d
    return pl.pallas_call(
        _mod_kernel,
        grid=(n_chunk,),
        in_specs=[pl.BlockSpec((nb, d), lambda i: (0, 0)),
                  pl.BlockSpec((d, d), lambda i: (0, i)),
                  pl.BlockSpec((1, d), lambda i: (0, i))],
        out_specs=pl.BlockSpec((nb, d), lambda i: (0, i)),
        out_shape=jax.ShapeDtypeStruct((nb, w_ada.shape[1]), F32),
        compiler_params=_cp("parallel"),
        name="modulation",
    )(c_all, w_ada, b_ada.reshape(1, -1))


N_GATE = 3 * NSA_H
SEL_TK = 256
LOG2E = 1.4426950408889634
_OPERAND_SEGS = (("ks_op", NSA_G * LANES, "bf16"), ("vs_op", NSA_G * LANES, "bf16"),
                 ("kw_op", NSA_G * LANES, "bf16"), ("vw_op", NSA_G * LANES, "bf16"),
                 ("kf_op", FOX_H * LANES, "bf16"), ("vf_op", FOX_H * LANES, "bf16"))
_IN_SEGS = {
    "prompt": (("kvc", KVW, "f32"), ("small", LANES, "f32"), ("qn", QW, "bf16"), ("qf", QW, "bf16"))
    + _OPERAND_SEGS + (("kvc_t", KVW, "f32_t"), ("kvs_t", KVW, "f32_t"), ("kvw_t", KVW, "f32_t"),
                       ("kvf_t", KVF, "f32_t")),
    "sample": (("kvc", KVW, "f32"), ("kvs", KVW, "f32"), ("kvw", KVW, "f32"), ("kvf", KVF, "f32"),
               ("small", LANES, "f32"), ("qn", QW, "bf16"), ("qf", QW, "bf16")),
}


def _in_kernel(x_ref, sc_ref, sh_ref, g_ref, w_ref, wt_ref, bias_ref, *out_refs, segs, seq_len):
    x = x_ref[...]
    tm = x.shape[0]
    h = _rms(x, g_ref[...]) * (1.0 + sc_ref[...]) + sh_ref[...]
    hb = h.astype(BF16)
    off = off_t = 0
    for (name, width, kind), o_ref in zip(segs, out_refs):
        if kind == "f32_t":
            o_ref[0] = _dot_nt(wt_ref[off_t:off_t + width, :], hb)
            off_t += width
            continue
        p = _dot(hb, w_ref[:, off:off + width])
        if name == "small":
            z = p + bias_ref[...]
            lane = _iota(z.shape, 1)
            logsig = jnp.minimum(z, 0.0) - jnp.log1p(jnp.exp(-jnp.abs(z)))
            p = jnp.where(lane < N_GATE, jax.nn.sigmoid(z), logsig)
        elif name in ("vs_op", "vw_op", "vf_op"):
            p = jnp.where((_iota(p.shape, 1) & (LANES - 1)) < HD, 1.0, p)
        elif name == "ks_op":
            t = (pl.program_id(0) * tm) % seq_len + _iota(p.shape, 0)
            p = jnp.where((_iota(p.shape, 1) & (LANES - 1)) == HD, (t & (SEL_TK - 1)).astype(F32), p)
        o_ref[...] = p.astype(o_ref.dtype)
        off += width


def _in_weights(w_in, b_nsa_gate, b_forget):
    d = w_in.shape[0]
    o_qn, o_kvc, o_kvs, o_kvw = 0, QW, QW + KVW, QW + 2 * KVW
    o_gn = QW + 3 * KVW
    o_qf = o_gn + N_GATE
    o_kvf = o_qf + QW
    o_ff = o_kvf + KVF
    z = jnp.zeros((d, HD), F32)

    def chunks(w, n, lo_first):
        parts = []
        for i in range(n):
            c = w[:, i * HD:(i + 1) * HD]
            parts += [c, z] if lo_first(i) else [z, c]
        return jnp.concatenate(parts, axis=1)

    w_kvc, w_kvs, w_kvw, w_kvf = (w_in[:, o:o + wd] for o, wd in ((o_kvc, KVW), (o_kvs, KVW), (o_kvw, KVW),
                                                                   (o_kvf, KVF)))
    nk, fk = NSA_G * HD, FOX_H * HD
    scale = HD ** -0.5
    cols = dict(
        kvc=w_kvc, kvs=w_kvs, kvw=w_kvw, kvf=w_kvf, kvc_t=w_kvc, kvs_t=w_kvs, kvw_t=w_kvw, kvf_t=w_kvf,
        small=jnp.concatenate([w_in[:, o_gn:o_gn + N_GATE], w_in[:, o_ff:o_ff + FOX_H],
                               jnp.zeros((d, LANES - N_GATE - FOX_H), F32)], axis=1),
        qn=w_in[:, o_qn:o_qn + QW] * scale, qf=w_in[:, o_qf:o_qf + QW] * (scale * LOG2E),
        ks_op=chunks(w_kvs[:, :nk], NSA_G, lambda i: True), vs_op=chunks(w_kvs[:, nk:], NSA_G, lambda i: False),
        kw_op=chunks(w_kvw[:, :nk], NSA_G, lambda i: True), vw_op=chunks(w_kvw[:, nk:], NSA_G, lambda i: False),
        kf_op=chunks(w_kvf[:, :fk], FOX_H, lambda i: i % 2 == 0), vf_op=chunks(w_kvf[:, fk:], FOX_H, lambda i: False))
    bias = jnp.concatenate([b_nsa_gate, b_forget, jnp.zeros((LANES - N_GATE - FOX_H,), F32)]).reshape(1, LANES)
    return cols, bias


def _in_proj(x2, scale, shift, g_pre, cols, bias, rows_per_mod, group):
    n, d = x2.shape
    tm = min(256, n)
    segs = _IN_SEGS[group]
    if rows_per_mod >= tm:
        per = rows_per_mod // tm
        mod_spec = pl.BlockSpec((None, 1, d), lambda i: (i // per, 0, 0))
    else:
        assert not any(k == "f32_t" for _, _, k in segs)
        per = 1
        scale = jnp.repeat(scale[:, 0, :], rows_per_mod, axis=0)
        shift = jnp.repeat(shift[:, 0, :], rows_per_mod, axis=0)
        mod_spec = pl.BlockSpec((tm, d), lambda i: (i, 0))
    w = jnp.concatenate([cols[s] for s, _, k in segs if k != "f32_t"], axis=1).astype(BF16)
    t_names = [s for s, _, k in segs if k == "f32_t"]
    wt = (jnp.concatenate([cols[s] for s in t_names], axis=1).T.astype(BF16) if t_names
          else jnp.zeros((8, d), BF16))
    n_seq = n // rows_per_mod
    out_specs, out_shape = [], []
    for _, width, kind in segs:
        if kind == "f32_t":
            out_specs.append(pl.BlockSpec((1, width, tm), lambda i: (i // per, 0, i % per)))
            out_shape.append(jax.ShapeDtypeStruct((n_seq, width, rows_per_mod), F32))
        else:
            out_specs.append(pl.BlockSpec((tm, width), lambda i: (i, 0)))
            out_shape.append(jax.ShapeDtypeStruct((n, width), F32 if kind == "f32" else BF16))
    full = lambda a: pl.BlockSpec(a.shape, lambda i: (0,) * a.ndim)
    outs = pl.pallas_call(
        functools.partial(_in_kernel, segs=segs, seq_len=rows_per_mod),
        grid=(n // tm,),
        in_specs=[pl.BlockSpec((tm, d), lambda i: (i, 0)), mod_spec, mod_spec,
                  pl.BlockSpec((1, d), lambda i: (0, 0)), full(w), full(wt), full(bias)],
        out_specs=out_specs,
        out_shape=out_shape,
        compiler_params=_cp("parallel"),
        name="in_proj_" + group,
    )(x2, scale, shift, g_pre.reshape(1, d), w, wt, bias)
    return dict(zip([s for s, _, _ in segs], outs))


def _cmp_kernel(subk_ref, subv_ref, w1_ref, pe_ref, b1_ref, w2_ref, o_ref):
    half = CMP_STRIDE * HD
    out = None
    for kv, sub_ref in enumerate((subk_ref, subv_ref)):
        sub = sub_ref[0, 0]
        w1 = w1_ref[kv]
        a = _dot(sub, w1[:half])
        b = _dot(sub, w1[half:])
        n_sub = a.shape[0]
        b_next = pltpu.roll(b, n_sub - 1, 0)
        bias = _dot(pe_ref[kv], w1)[0:1] + b1_ref[kv]
        hid = jax.nn.gelu(a + b_next + bias).astype(BF16)
        term = _dot(hid, w2_ref[kv])
        out = term if out is None else out + term
    o_ref[0, 0] = out.astype(o_ref.dtype)


def _cmp_weights(w_cmp1, b_cmp1, w_cmp2, pe_cmp):
    w1 = w_cmp1.astype(BF16)
    pe = jnp.broadcast_to(pe_cmp.reshape(2, 1, CMP_LEN * HD), (2, 8, CMP_LEN * HD)).astype(BF16)
    b1 = b_cmp1.reshape(2, 1, CMP_HID)
    z = jnp.zeros((CMP_HID, HD), F32)
    w2 = jnp.stack([jnp.concatenate([w_cmp2[0], z], axis=1),
                    jnp.concatenate([z, w_cmp2[1]], axis=1)]).astype(BF16)
    return w1, pe, b1, w2


def _compress_prompt(kvc, B, T, cw):
    w1, pe, b1, w2 = cw
    n_sub = T // CMP_STRIDE
    sub = kvc.reshape(B, n_sub, CMP_STRIDE, 2 * NSA_G, HD).transpose(0, 3, 1, 2, 4)
    sub = sub.reshape(B, 2 * NSA_G, n_sub, CMP_STRIDE * HD).astype(BF16)
    feat = CMP_STRIDE * HD
    return pl.pallas_call(
        _cmp_kernel,
        grid=(B, NSA_G),
        in_specs=[pl.BlockSpec((1, 1, n_sub, feat), lambda b, g: (b, g, 0, 0)),
                  pl.BlockSpec((1, 1, n_sub, feat), lambda b, g: (b, NSA_G + g, 0, 0)),
                  pl.BlockSpec(w1.shape, lambda b, g: (0, 0, 0)),
                  pl.BlockSpec(pe.shape, lambda b, g: (0, 0, 0)),
                  pl.BlockSpec(b1.shape, lambda b, g: (0, 0, 0)),
                  pl.BlockSpec(w2.shape, lambda b, g: (0, 0, 0))],
        out_specs=pl.BlockSpec((1, 1, n_sub, LANES), lambda b, g: (b, g, 0, 0)),
        out_shape=jax.ShapeDtypeStruct((B, NSA_G, n_sub, LANES), BF16),
        compiler_params=_cp("parallel", "parallel"),
        name="compress_prompt",
    )(sub, sub, w1, pe, b1, w2)


def _cumsum_block(x):
    n = x.shape[1]
    u = _onehot(_iota((n, n), 0) <= _iota((n, n), 1))
    r = x.shape[0]
    d3 = _dot(jnp.concatenate(_split3(x), axis=0), u)
    return d3[:r] + d3[r:2 * r] + d3[2 * r:]


def _cumsum_kernel(x_ref, o_ref, carry_ref):
    @pl.when(pl.program_id(1) == 0)
    def _():
        carry_ref[...] = jnp.zeros_like(carry_ref)

    d = _cumsum_block(x_ref[0]) + carry_ref[:, 0:1]
    o_ref[0] = d
    carry_ref[...] = jnp.broadcast_to(d[:, d.shape[1] - 1:], carry_ref.shape)


def _cumsum_prompt(logf_t):
    B, H, T = logf_t.shape
    tc = min(512, T)
    return pl.pallas_call(
        _cumsum_kernel,
        grid=(B, T // tc),
        in_specs=[pl.BlockSpec((1, H, tc), lambda b, c: (b, 0, c))],
        out_specs=pl.BlockSpec((1, H, tc), lambda b, c: (b, 0, c)),
        out_shape=jax.ShapeDtypeStruct((B, H, T), F32),
        scratch_shapes=[pltpu.VMEM((H, LANES), F32)],
        compiler_params=_cp("parallel", "arbitrary"),
        name="cumsum_prompt",
    )(logf_t)


MASK_BIG = 2.0 ** 30


def _topk_mask(val, n_take, axis):
    idx_f = _iota(val.shape, axis).astype(F32)
    sel = jnp.zeros(val.shape, F32)
    for _ in range(n_take):
        m = jnp.max(val, axis=axis, keepdims=True)
        first = jnp.min(jnp.where(val == m, idx_f, float(val.shape[axis])), axis=axis, keepdims=True)
        pick = idx_f == first
        sel = jnp.where(pick, 1.0, sel)
        val = jnp.where(pick, NEG_INF, val)
    return sel


def _masked_softmax(s, mask, axis=1):
    s = jnp.where(mask, s, NEG)
    m = jnp.max(s, axis=axis, keepdims=True)
    e = jnp.where(mask, jnp.exp(s - m), 0.0)
    return e / jnp.maximum(jnp.sum(e, axis=axis, keepdims=True), TINY)


def _flash_step(s, v, m_ref, acc_ref, base2):
    nk = s.shape[1]
    ex = jnp.exp2 if base2 else jnp.exp
    m_old = m_ref[...]
    m_new = jnp.maximum(m_old, jnp.max(s, axis=1, keepdims=True))
    alpha = ex(m_old - m_new)
    e = jnp.concatenate([ex(s[:, c * LANES:(c + 1) * LANES] - m_new) for c in range(nk // LANES)], axis=1)
    acc_ref[...] = alpha * acc_ref[...] + _dot(e.astype(BF16), v)
    m_ref[...] = m_new


def _hi_over_lo(acc):
    return acc / jnp.maximum(pltpu.roll(acc, HD, 1), TINY)


def _pair_heads(o_even, o_odd):
    lane = _iota(o_even.shape, 1)
    return jnp.where(lane < HD, pltpu.roll(o_even, HD, 1), o_odd)


def _cmp_to_slc_np(n_slots, first_token):
    tok = np.arange(n_slots)[:, None] + first_token
    start = tok * CMP_STRIDE
    bstart = np.arange(LANES)[None, :] * SLC_BLOCK
    m = (start < bstart + SLC_BLOCK) & (start + CMP_LEN > bstart) & (tok >= 0)
    return m.astype(np.float32)


def _nsa_sel_kernel(slopes_ref, q_ref, cmp_ref, cmpt_ref, c2st_ref, oc_ref, sel_ref, *, tq, n_sel):
    g = pl.program_id(1)
    t0 = pl.program_id(2) * tq
    n_slots = cmp_ref.shape[2]
    qpos = t0 + _iota((1, tq), 1)
    cend = _iota((n_slots, 1), 0) * CMP_STRIDE + (CMP_LEN - 1)
    d_c = qpos - cend
    mask_c = d_c >= 0
    d_cf = d_c.astype(F32)
    q = q_ref[...]
    cmp = cmp_ref[0, 0]
    cmpt = cmpt_ref[0, 0]
    c2st = c2st_ref[...]
    imp_t = jnp.zeros((LANES, tq), F32)
    for h in range(HG):
        q128 = _dot(q, _pick_head(HG * HD, h * HD)).astype(BF16)
        s_t = _dot_nt(cmp, q128) - slopes_ref[g * HG + h] * d_cf
        p_t = _masked_softmax(s_t, mask_c, axis=0).astype(BF16)
        oc_ref[0, 0, h] = _dot(cmpt, p_t).T
        imp_t = imp_t + _dot(c2st, p_t)
    blk = _iota((LANES, tq), 0)
    cur = qpos >> 6
    forced = (blk == 0) | (blk == cur) | (blk == cur - 1)
    valid = blk * SLC_BLOCK <= qpos
    val = jnp.where(valid, imp_t + jnp.where(forced, FORCE, 0.0), NEG)
    sel_t = jnp.where(valid, _topk_mask(val, n_sel, 0), 0.0)
    sel_ref[0, 0] = ((sel_t - 1.0) * MASK_BIG).T.astype(sel_ref.dtype)


NSA_TQ_SEL, NSA_TQ_ATT = 512, 256


def _nsa_select(slopes, qn, cmp, B, T):
    tq = min(NSA_TQ_SEL, T)
    nq = T // tq
    n_slots = cmp.shape[2]
    n_slc = -(-T // SLC_BLOCK)
    assert n_slc <= LANES and SLC_BLOCK == 64
    c2st = jnp.asarray(_cmp_to_slc_np(n_slots, 0).T, BF16)
    cmpt = jnp.swapaxes(cmp, 2, 3)
    return pl.pallas_call(
        functools.partial(_nsa_sel_kernel, tq=tq, n_sel=min(SLC_TOPK, n_slc)),
        grid=(B, NSA_G, nq),
        in_specs=[pl.BlockSpec(memory_space=pltpu.SMEM),
                  pl.BlockSpec((tq, HG * HD), lambda b, g, j: (b * nq + j, g)),
                  pl.BlockSpec((1, 1, n_slots, LANES), lambda b, g, j: (b, g, 0, 0)),
                  pl.BlockSpec((1, 1, LANES, n_slots), lambda b, g, j: (b, g, 0, 0)),
                  pl.BlockSpec((LANES, n_slots), lambda b, g, j: (0, 0))],
        out_specs=[pl.BlockSpec((1, 1, HG, tq, LANES), lambda b, g, j: (b, g, 0, j, 0)),
                   pl.BlockSpec((1, 1, tq, LANES), lambda b, g, j: (b, g, j, 0))],
        out_shape=[jax.ShapeDtypeStruct((B, NSA_G, HG, T, LANES), F32),
                   jax.ShapeDtypeStruct((B, NSA_G, T, LANES), BF16)],
        compiler_params=_cp("parallel", "parallel", "parallel"),
        name="nsa_select",
    )(slopes, qn, cmp, cmpt, c2st)


def _tile_bits(selneg, B, T, tq, tk):
    nq = T // tq
    n_tiles = T // tk
    bpt = tk // SLC_BLOCK
    assert n_tiles <= 32
    anyb = (selneg.reshape(B, NSA_G, nq, tq, LANES) == 0).any(axis=3)
    anyt = anyb[..., :n_tiles * bpt].reshape(B, NSA_G, nq, n_tiles, bpt).any(axis=-1)
    w = jnp.sum(anyt.astype(jnp.uint32) << jnp.arange(n_tiles, dtype=jnp.uint32), axis=-1, dtype=jnp.uint32)
    return lax.bitcast_convert_type(w, jnp.int32).reshape(-1)


def _nsa_attn_kernel(bits_ref, slopes_ref, q_ref, ks_ref, vs_ref, kw_ref, vw_ref, blk1h_ref, sel_ref, oc_ref,
                     gates_ref, o_ref, m_s, acc_s, *, tq, tk, wk):
    b, g, j = pl.program_id(0), pl.program_id(1), pl.program_id(2)
    t0 = j * tq
    q = q_ref[...]
    selneg = sel_ref[0, 0]
    lane = _iota((tq, LANES), 1)
    slopes = [slopes_ref[g * HG + h] for h in range(HG)]
    q_sel, q_win = [], []
    for h in range(HG):
        q128 = _dot(q, _pick_head(HG * HD, h * HD))
        q_win.append(q128.astype(BF16))
        q_sel.append(jnp.concatenate([jnp.where(lane == HD, slopes[h], q128).astype(BF16), selneg], axis=1))
    q4 = jnp.concatenate(q_sel, axis=0)
    qw4 = jnp.concatenate(q_win, axis=0)
    qpos = t0 + _iota((tq, 1), 0)

    m_s[...] = jnp.full(m_s.shape, NEG, F32)
    acc_s[...] = jnp.zeros(acc_s.shape, F32)
    word = bits_ref[(b * NSA_G + g) * pl.num_programs(2) + j]

    def tile(kt, diagonal):
        k0 = pl.multiple_of(kt * tk, tk)
        kop = jnp.concatenate([ks_ref[pl.ds(k0, tk), :], blk1h_ref[pl.ds(k0, tk), :]], axis=1)
        vop = vs_ref[pl.ds(k0, tk), :]
        s = _dot_nt(q4, kop)
        off = (k0 - t0).astype(F32)
        if diagonal:
            causal = (k0 + _iota((1, tk), 1)) <= qpos
        for h in range(HG):
            sh = s[h * tq:(h + 1) * tq] + slopes[h] * off
            if diagonal:
                sh = jnp.where(causal, sh, NEG)
            _flash_step(sh, vop, m_s.at[h], acc_s.at[h], False)

    def body(kt, carry):
        @pl.when(((word >> kt) & 1) == 1)
        def _():
            tile(kt, False)
        return carry

    n_before = t0 // tk
    lax.fori_loop(0, n_before, body, 0)
    for dd in range(max(1, tq // tk)):
        tile(n_before + dd, True)

    w0 = pl.multiple_of(jnp.maximum(t0 + tq - wk, 0), tq)
    kw = kw_ref[pl.ds(w0, wk), :]
    vw = vw_ref[pl.ds(w0, wk), :]
    d_w = qpos - (w0 + _iota((1, wk), 1))
    mask_w = (d_w >= 0) & (d_w < WINDOW)
    d_wf = d_w.astype(F32)
    s_w = _dot_nt(qw4, kw)
    gates = gates_ref[0]
    o_heads = []
    for h in range(HG):
        sw = jnp.where(mask_w, s_w[h * tq:(h + 1) * tq] - slopes[h] * d_wf, NEG)
        e_w = jnp.where(mask_w, jnp.exp(sw - jnp.max(sw, axis=1, keepdims=True)), 0.0)
        o_w = _hi_over_lo(_dot(e_w.astype(BF16), vw))
        o_s = _hi_over_lo(acc_s[h])
        o_heads.append(gates[:, 3 * h:3 * h + 1] * oc_ref[0, 0, h] + gates[:, 3 * h + 1:3 * h + 2] * o_s
                       + gates[:, 3 * h + 2:3 * h + 3] * o_w)
    o_ref[...] = jnp.concatenate([_pair_heads(o_heads[0], o_heads[1]), _pair_heads(o_heads[2], o_heads[3])],
                                 axis=1).astype(o_ref.dtype)


def _nsa_attend(slopes, qn, pr, selneg, oc, gates_g, B, T):
    tq = min(NSA_TQ_ATT, T)
    nq = T // tq
    tk = min(SEL_TK, T)
    assert tk == SEL_TK and HG == 4
    wk = min(WINDOW + tq, T)
    bits = _tile_bits(selneg, B, T, tq, tk)
    blk1h = jnp.asarray(np.arange(T)[:, None] // SLC_BLOCK == np.arange(LANES)[None, :], BF16)
    kv_spec = pl.BlockSpec((T, LANES), lambda b, g, j, bits: (b, g))
    return pl.pallas_call(
        functools.partial(_nsa_attn_kernel, tq=tq, tk=tk, wk=wk),
        grid_spec=pltpu.PrefetchScalarGridSpec(
            num_scalar_prefetch=1, grid=(B, NSA_G, nq),
            in_specs=[pl.BlockSpec(memory_space=pltpu.SMEM),
                      pl.BlockSpec((tq, HG * HD), lambda b, g, j, bits: (b * nq + j, g)),
                      kv_spec, kv_spec, kv_spec, kv_spec,
                      pl.BlockSpec((T, LANES), lambda b, g, j, bits: (0, 0)),
                      pl.BlockSpec((1, 1, tq, LANES), lambda b, g, j, bits: (b, g, j, 0)),
                      pl.BlockSpec((1, 1, HG, tq, LANES), lambda b, g, j, bits: (b, g, 0, j, 0)),
                      pl.BlockSpec((1, tq, LANES), lambda b, g, j, bits: (g, b * nq + j, 0))],
            out_specs=pl.BlockSpec((tq, HG * HD), lambda b, g, j, bits: (b * nq + j, g)),
            scratch_shapes=[pltpu.VMEM((HG, tq, LANES), F32), pltpu.VMEM((HG, tq, LANES), F32)]),
        out_shape=jax.ShapeDtypeStruct((B * T, QW), BF16),
        compiler_params=_cp("parallel", "parallel", "parallel"),
        name="nsa_attend",
    )(bits, slopes, qn, pr["ks_op"], pr["vs_op"], pr["kw_op"], pr["vw_op"], blk1h, selneg, oc, gates_g)


FOX_TQ, FOX_TK, FOX_ROWS = 1024, 512, 1024


def _fox_kernel(q_ref, k0_ref, k1_ref, v0_ref, v1_ref, d0_ref, d1_ref, o_ref, m_s, acc_s, *, tq, rows, tk):
    j = pl.program_id(2)
    n_part = tq // rows
    k_refs, v_refs, d_refs = (k0_ref, k1_ref), (v0_ref, v1_ref), (d0_ref, d1_ref)
    m_s[...] = jnp.full(m_s.shape, NEG, F32)
    acc_s[...] = jnp.zeros(acc_s.shape, F32)

    def step(kt, diagonal):
        k0 = pl.multiple_of(kt * tk, tk)
        for hh in range(2):
            k = k_refs[hh][pl.ds(k0, tk), :]
            v = v_refs[hh][pl.ds(k0, tk), :]
            bias = d_refs[hh][0, kt] * LOG2E
            for r in range(n_part):
                s = _dot_nt(q_ref[r * rows:(r + 1) * rows, :], k) - bias
                if diagonal:
                    qpos = j * tq + r * rows + _iota((rows, 1), 0)
                    s = jnp.where(k0 + _iota((1, tk), 1) <= qpos, s, NEG)
                _flash_step(s, v, m_s.at[hh * n_part + r], acc_s.at[hh * n_part + r], True)

    def body(kt, carry):
        step(kt, False)
        return carry

    n_full = (j * tq) // tk
    lax.fori_loop(0, n_full, body, 0)
    for dd in range(tq // tk):
        step(n_full + dd, True)
    for r in range(n_part):
        o_ref[r * rows:(r + 1) * rows, :] = _pair_heads(_hi_over_lo(acc_s[r]),
                                                         _hi_over_lo(acc_s[n_part + r])).astype(o_ref.dtype)


def _fox_attend(qf, kf_op, vf_op, d_t, B, T):
    tq = min(FOX_TQ, T)
    tk = min(FOX_TK, tq)
    rows = min(FOX_ROWS, tq)
    nq = T // tq
    d4 = d_t.reshape(B * FOX_H, T // tk, 1, tk)
    kv_spec = lambda hh: pl.BlockSpec((T, LANES), lambda b, hp, j: (b, 2 * hp + hh))
    d_spec = lambda hh: pl.BlockSpec((1, T // tk, 1, tk), lambda b, hp, j: (b * FOX_H + 2 * hp + hh, 0, 0, 0))
    n_chain = 2 * (tq // rows)
    return pl.pallas_call(
        functools.partial(_fox_kernel, tq=tq, rows=rows, tk=tk),
        grid=(B, FOX_H // 2, nq),
        in_specs=[pl.BlockSpec((tq, 2 * HD), lambda b, hp, j: (b * nq + j, hp)),
                  kv_spec(0), kv_spec(1), kv_spec(0), kv_spec(1), d_spec(0), d_spec(1)],
        out_specs=pl.BlockSpec((tq, 2 * HD), lambda b, hp, j: (b * nq + j, hp)),
        out_shape=jax.ShapeDtypeStruct((B * T, FOX_H * HD), BF16),
        scratch_shapes=[pltpu.VMEM((n_chain, rows, LANES), F32), pltpu.VMEM((n_chain, rows, LANES), F32)],
        compiler_params=_cp("parallel", "parallel", "parallel"),
        name="fox_attend",
    )(qf, kf_op, kf_op, vf_op, vf_op, d4, d4)


def _out_kernel(on_ref, of_ref, x_ref, g1_ref, sc2_ref, sh2_ref, gpm_ref, gpf_ref, wo_ref, wr_ref, br_ref,
                x1_ref, h2_ref, gd_ref):
    y = _dot(on_ref[...], wo_ref[:QW]) + _dot(of_ref[...], wo_ref[QW:])
    x1 = x_ref[...] + g1_ref[...] * _rms(y, gpm_ref[...])
    x1_ref[...] = x1
    h2 = (_rms(x1, gpf_ref[...]) * (1.0 + sc2_ref[...]) + sh2_ref[...]).astype(BF16)
    h2_ref[...] = h2
    scores = jax.nn.sigmoid(_dot(h2, wr_ref[...]))
    sel = _topk_mask(scores + br_ref[...], TOP_K, 1)
    gsel = sel * scores
    gd_ref[...] = gsel / jnp.sum(gsel, axis=1, keepdims=True) * ROUTED_SCALE


def _out_proj(o_n, o_f, x2, gate1, scale2, shift2, g_post_mix, g_pre_ffn, wo, wr, br, rows_per_mod):
    n, d = x2.shape
    tm = min(256, n)
    if rows_per_mod >= tm:
        per = rows_per_mod // tm
        mod_spec = pl.BlockSpec((None, 1, d), lambda i: (i // per, 0, 0))
    else:
        gate1, scale2, shift2 = (jnp.repeat(a[:, 0, :], rows_per_mod, axis=0) for a in (gate1, scale2, shift2))
        mod_spec = pl.BlockSpec((tm, d), lambda i: (i, 0))
    row = lambda w: pl.BlockSpec((tm, w), lambda i: (i, 0))
    full = lambda a: pl.BlockSpec(a.shape, lambda i: (0,) * a.ndim)
    gpm, gpf = g_post_mix.reshape(1, d), g_pre_ffn.reshape(1, d)
    return pl.pallas_call(
        _out_kernel,
        grid=(n // tm,),
        in_specs=[row(QW), row(QW), row(d), mod_spec, mod_spec, mod_spec, full(gpm), full(gpf),
                  full(wo), full(wr), full(br)],
        out_specs=[row(d), row(d), row(LANES)],
        out_shape=[jax.ShapeDtypeStruct((n, d), F32), jax.ShapeDtypeStruct((n, d), BF16),
                   jax.ShapeDtypeStruct((n, LANES), F32)],
        compiler_params=_cp("parallel"),
        name="out_proj_router",
    )(o_n, o_f, x2, gate1, scale2, shift2, gpm, gpf, wo, wr, br)


MOE_EB, MOE_TM = 8, 1024


def _moe_kernel(h_ref, gd_ref, wg_ref, wu_ref, wd_ref, o_ref, *, eb):
    blk = pl.program_id(1)

    @pl.when(blk == 0)
    def _():
        o_ref[...] = jnp.zeros_like(o_ref)

    h = h_ref[...]
    f = wg_ref.shape[2]
    gd = gd_ref[...]
    g_hi = gd.astype(BF16)
    g_lo = (gd - g_hi.astype(F32)).astype(BF16)
    expand = _onehot(_iota((LANES, eb * f), 0) == blk * eb + _iota((LANES, eb * f), 1) // f)
    gates = _dot(g_hi, expand) + _dot(g_lo, expand)
    parts = []
    for i in range(eb):
        a = _dot(h, wg_ref[i])
        parts.append((a * jax.nn.sigmoid(a) * _dot(h, wu_ref[i]) * gates[:, i * f:(i + 1) * f]).astype(BF16))
    hb = jnp.concatenate(parts, axis=1)
    o_ref[...] += _dot(hb, wd_ref[...].reshape(eb * f, wd_ref.shape[2]))


def _moe_routed(h2, gd, wg, wu, wd):
    n, d = h2.shape
    n_e, _, f = wg.shape
    tm = min(MOE_TM, n)
    eb = MOE_EB
    assert n_e % eb == 0
    return pl.pallas_call(
        functools.partial(_moe_kernel, eb=eb),
        grid=(n // tm, n_e // eb),
        in_specs=[pl.BlockSpec((tm, d), lambda i, e: (i, 0)),
                  pl.BlockSpec((tm, LANES), lambda i, e: (i, 0)),
                  pl.BlockSpec((eb, d, f), lambda i, e: (e, 0, 0)),
                  pl.BlockSpec((eb, d, f), lambda i, e: (e, 0, 0)),
                  pl.BlockSpec((eb, f, d), lambda i, e: (e, 0, 0))],
        out_specs=pl.BlockSpec((tm, d), lambda i, e: (i, 0)),
        out_shape=jax.ShapeDtypeStruct((n, d), F32),
        compiler_params=_cp("parallel", "arbitrary"),
        name="moe_routed",
    )(h2, gd, wg, wu, wd)


def _final_kernel(r_ref, h_ref, x1_ref, g2_ref, gpf_ref, wg_ref, wu_ref, wd_ref, y_ref):
    h = h_ref[...]
    a = _dot(h, wg_ref[...])
    hb = (a * jax.nn.sigmoid(a) * _dot(h, wu_ref[...])).astype(BF16)
    f = r_ref[...] + _dot(hb, wd_ref[...])
    y_ref[...] = x1_ref[...] + g2_ref[...] * _rms(f, gpf_ref[...])


def _final(routed, h2, x1, gate2, g_post_ffn, wsg, wsu, wsd, rows_per_mod):
    n, d = x1.shape
    tm = min(512, n)
    if rows_per_mod >= tm:
        per = rows_per_mod // tm
        mod_spec = pl.BlockSpec((None, 1, d), lambda i: (i // per, 0, 0))
    else:
        gate2 = jnp.repeat(gate2[:, 0, :], rows_per_mod, axis=0)
        mod_spec = pl.BlockSpec((tm, d), lambda i: (i, 0))
    row = pl.BlockSpec((tm, d), lambda i: (i, 0))
    full = lambda a: pl.BlockSpec(a.shape, lambda i: (0,) * a.ndim)
    gpf = g_post_ffn.reshape(1, d)
    return pl.pallas_call(
        _final_kernel,
        grid=(n // tm,),
        in_specs=[row, row, row, mod_spec, full(gpf), full(wsg), full(wsu), full(wsd)],
        out_specs=row,
        out_shape=jax.ShapeDtypeStruct((n, d), F32),
        compiler_params=_cp("parallel"),
        name="shared_ffn_final",
    )(routed, h2, x1, gate2, gpf, wsg, wsu, wsd)


PAGES_PER_STEP = 16


def _page_specs(block, n_pages, pp):
    def spec(jj):
        return pl.BlockSpec(block, lambda b, c, pt: (pt[b * n_pages + c * pp + jj],) + (0,) * (len(block) - 1))
    return [spec(jj) for jj in range(pp)]


def _cmp_sample_kernel(pt_ref, *refs, pp):
    pages = refs[:pp]
    w1_ref, bias_ref, w2_ref, o_ref, carry_ref, x_s = refs[pp:]
    c = pl.program_id(1)
    page = pages[0].shape[-1]
    rows = pp * page // CMP_STRIDE

    @pl.when(c == 0)
    def _():
        carry_ref[...] = jnp.zeros_like(carry_ref)

    for jj, p in enumerate(pages):
        for kv in range(2):
            x_s[kv, jj * page:(jj + 1) * page, :] = p[0, kv].reshape(NSA_G * HD, page).T

    acc = [jnp.zeros((rows, 4 * CMP_HID), F32) for _ in range(2)]
    for t in range(CMP_STRIDE):
        for kv in range(2):
            x_t = x_s[kv, pl.ds(t, rows, stride=CMP_STRIDE), :]
            acc[kv] = acc[kv] + _dot(x_t.astype(BF16), w1_ref[kv, t])
    out = [None, None]
    for kv in range(2):
        a = acc[kv][:, :2 * CMP_HID]
        b = acc[kv][:, 2 * CMP_HID:]
        a_prev = pltpu.roll(a, 1, 0)
        a_prev = jnp.where(_iota(a.shape, 0) == 0, carry_ref[kv][0:1], a_prev)
        carry_ref[kv] = jnp.broadcast_to(a[rows - 1:rows], carry_ref.shape[1:])
        hid = jax.nn.gelu(a_prev + b + bias_ref[kv]).astype(BF16)
        for g in range(NSA_G):
            term = _dot(hid[:, g * CMP_HID:(g + 1) * CMP_HID], w2_ref[kv, g])
            out[g] = term if out[g] is None else out[g] + term
    for g in range(NSA_G):
        o_ref[0, g] = out[g].astype(o_ref.dtype)


def _cmp_sample_weights(w_cmp1, b_cmp1, w_cmp2, pe_cmp):
    w1 = w_cmp1.reshape(2, 2, CMP_STRIDE, HD, CMP_HID)
    z = jnp.zeros_like(w1[:, 0])
    top = jnp.concatenate([w1[:, 0], z, w1[:, 1], z], axis=-1)
    bot = jnp.concatenate([z, w1[:, 0], z, w1[:, 1]], axis=-1)
    w1bd = jnp.concatenate([top, bot], axis=2).astype(BF16)
    bias = (jnp.einsum("kf,kfh->kh", pe_cmp.reshape(2, -1).astype(BF16), w_cmp1.astype(BF16),
                       preferred_element_type=F32) + b_cmp1)
    bias = jnp.concatenate([bias, bias], axis=-1).reshape(2, 1, 2 * CMP_HID)
    z2 = jnp.zeros((CMP_HID, HD), F32)
    lo = lambda w: jnp.concatenate([w, z2], axis=1)
    hi = lambda w: jnp.concatenate([z2, w], axis=1)
    w2 = jnp.stack([jnp.stack([lo(w_cmp2[0]), hi(w_cmp2[0])]),
                    jnp.stack([hi(w_cmp2[1]), lo(w_cmp2[1])])]).astype(BF16)
    return w1bd, bias, w2


def _compress_sample(cache, pt_flat, Bs, n_pages, csw):
    w1bd, bias, w2 = csw
    pp = min(2 * PAGES_PER_STEP, n_pages)
    page = cache.shape[-1]
    rows = pp * (page // CMP_STRIDE)
    n_slots = n_pages * (page // CMP_STRIDE)
    full = lambda a: pl.BlockSpec(a.shape, lambda b, c, pt: (0,) * a.ndim)
    return pl.pallas_call(
        functools.partial(_cmp_sample_kernel, pp=pp),
        grid_spec=pltpu.PrefetchScalarGridSpec(
            num_scalar_prefetch=1, grid=(Bs, n_pages // pp),
            in_specs=_page_specs((1,) + cache.shape[1:], n_pages, pp) + [full(w1bd), full(bias), full(w2)],
            out_specs=pl.BlockSpec((1, NSA_G, rows, LANES), lambda b, c, pt: (b, 0, c, 0)),
            scratch_shapes=[pltpu.VMEM((2, 8, 2 * CMP_HID), F32), pltpu.VMEM((2, pp * page, LANES), F32)]),
        out_shape=jax.ShapeDtypeStruct((Bs, NSA_G, n_slots, LANES), BF16),
        compiler_params=_cp("parallel", "arbitrary"),
        name="compress_sample",
    )(pt_flat, *([cache] * pp), w1bd, bias, w2)


def _logf_sample_kernel(pt_ref, *refs, pp):
    pages = refs[:pp]
    o_ref, carry_ref = refs[pp:]

    @pl.when(pl.program_id(1) == 0)
    def _():
        carry_ref[...] = jnp.zeros_like(carry_ref)

    page = pages[0].shape[2]
    per = max(1, 512 // page)
    for c0 in range(0, pp, per):
        x = jnp.concatenate([p[0] for p in pages[c0:c0 + per]], axis=1)
        d = _cumsum_block(x) + carry_ref[:, 0:1]
        o_ref[0, :, c0 * page:(c0 + per) * page] = d
        carry_ref[...] = jnp.broadcast_to(d[:, d.shape[1] - 1:], carry_ref.shape)


def _cumsum_sample(logf_cache_t, pt_flat, Bs, n_pages):
    pp = min(PAGES_PER_STEP, n_pages)
    page = logf_cache_t.shape[2]
    return pl.pallas_call(
        functools.partial(_logf_sample_kernel, pp=pp),
        grid_spec=pltpu.PrefetchScalarGridSpec(
            num_scalar_prefetch=1, grid=(Bs, n_pages // pp),
            in_specs=_page_specs((1, FOX_H, page), n_pages, pp),
            out_specs=pl.BlockSpec((1, FOX_H, pp * page), lambda b, c, pt: (b, 0, c)),
            scratch_shapes=[pltpu.VMEM((FOX_H, LANES), F32)]),
        out_shape=jax.ShapeDtypeStruct((Bs, FOX_H, n_pages * page), F32),
        compiler_params=_cp("parallel", "arbitrary"),
        name="cumsum_sample",
    )(pt_flat, *([logf_cache_t] * pp))


ROWS_S = NSA_G * HG * 8


def _sel_sample_kernel(q_ref, cmp_ref, c2s_ref, slope_ref, qidx_ref, oc_ref, sel_ref, *, past, n_take):
    n_slots = cmp_ref.shape[2]
    rg = ROWS_S // NSA_G
    slot = _iota((1, n_slots), 1)
    cend = slot * CMP_STRIDE + (CMP_STRIDE - 1)
    c2s = c2s_ref[...]
    blk = _iota((8, LANES), 1)
    n_blk = past // SLC_BLOCK
    for g in range(NSA_G):
        r = slice(g * rg, (g + 1) * rg)
        q = q_ref[0, r]
        cmp = cmp_ref[0, g]
        qpos = past + qidx_ref[r]
        d_c = qpos - cend
        mask = (d_c >= 0) & (slot >= 1)
        s = _dot_nt(q, cmp) - slope_ref[r] * d_c.astype(F32)
        pb = _masked_softmax(s, mask).astype(BF16)
        oc_ref[0, r] = pltpu.roll(_dot(pb, cmp), HD, 1)
        imp_h = _dot(pb, c2s)
        imp = imp_h[0:8] + imp_h[8:16] + imp_h[16:24] + imp_h[24:32]
        forced = (blk == 0) | (blk == n_blk - 1)
        val = jnp.where(blk < n_blk, imp + jnp.where(forced, FORCE, 0.0), NEG)
        sel8 = _topk_mask(val, n_take, 1)
        sel_ref[0, r] = jnp.concatenate([sel8] * HG, axis=0).astype(sel_ref.dtype)


def _select_sample(qs, cmp_s, slope_rows, qidx_rows, Bs, past):
    n_slots = cmp_s.shape[2]
    n_slc = past // SLC_BLOCK + 1
    assert past % SLC_BLOCK == 0 and n_slc - 1 <= LANES
    c2s = jnp.asarray(_cmp_to_slc_np(n_slots, -1), BF16)
    full = lambda a: pl.BlockSpec(a.shape, lambda b: (0,) * a.ndim)
    return pl.pallas_call(
        functools.partial(_sel_sample_kernel, past=past, n_take=min(SLC_TOPK, n_slc) - 1),
        grid=(Bs,),
        in_specs=[pl.BlockSpec((1, ROWS_S, LANES), lambda b: (b, 0, 0)),
                  pl.BlockSpec((1, NSA_G, n_slots, LANES), lambda b: (b, 0, 0, 0)),
                  full(c2s), full(slope_rows), full(qidx_rows)],
        out_specs=[pl.BlockSpec((1, ROWS_S, LANES), lambda b: (b, 0, 0)),
                   pl.BlockSpec((1, ROWS_S, LANES), lambda b: (b, 0, 0))],
        out_shape=[jax.ShapeDtypeStruct((Bs, ROWS_S, LANES), F32),
                   jax.ShapeDtypeStruct((Bs, ROWS_S, LANES), BF16)],
        compiler_params=_cp("parallel"),
        name="select_sample",
    )(qs, cmp_s, c2s, slope_rows, qidx_rows)


def _online_update(s, ok, v, m_s, l_s, acc_s, base2=False, v_t=False):
    ex = jnp.exp2 if base2 else jnp.exp
    s = jnp.where(ok, s, NEG)
    m_old = m_s[...]
    m_new = jnp.maximum(m_old, jnp.max(s, axis=1, keepdims=True))
    alpha = ex(m_old - m_new)
    e = jnp.where(ok, ex(s - m_new), 0.0)
    l_s[...] = alpha * l_s[...] + jnp.sum(e, axis=1, keepdims=True)
    acc_s[...] = alpha * acc_s[...] + (_dot_nt if v_t else _dot)(e.astype(BF16), v)
    m_s[...] = m_new


def _pages_t(pages, kv, width):
    return jnp.concatenate([p[0, kv].reshape(width, p.shape[-1]).astype(BF16) for p in pages], axis=1)


def _pad_rows8(x):
    return jnp.concatenate([x, jnp.zeros((8 - x.shape[0], x.shape[1]), x.dtype)], axis=0)


def _slc_sample_kernel(pt_ref, *refs, pp, past, t_new):
    pages = refs[:pp]
    q_ref, sel_ref, new_ref, slope_ref, qidx_ref, o_ref, m_s, l_s, acc_s = refs[pp:]
    c = pl.program_id(1)
    page = pages[0].shape[-1]
    nk = pp * page

    @pl.when(c == 0)
    def _():
        m_s[...] = jnp.full(m_s.shape, NEG, F32)
        l_s[...] = jnp.zeros(l_s.shape, F32)
        acc_s[...] = jnp.zeros(acc_s.shape, F32)

    q = q_ref[0]
    slope = slope_ref[...]
    qidx = qidx_ref[...]
    kpos = c * nk + _iota((1, nk), 1)
    expand = _onehot(_iota((LANES, 1), 0) == (kpos >> 6))
    ok = _dot(sel_ref[0], expand) > 0.5

    @pl.when(jnp.max(jnp.where(ok, 1.0, 0.0)) > 0.0)
    def _():
        dist = (past + qidx - kpos).astype(F32)
        s = _dot(q, _pages_t(pages, 0, NSA_G * HD)) - slope * dist
        _online_update(s, ok, _pages_t(pages, 1, NSA_G * HD), m_s, l_s, acc_s, v_t=True)

    @pl.when(c == pl.num_programs(1) - 1)
    def _():
        new = _pad_rows8(new_ref[0])
        kj = _iota((1, 8), 1)
        ok_new = (kj <= qidx) & (kj < t_new)
        s_new = _dot_nt(q, new[:, :LANES].astype(BF16)) - slope * (qidx - kj).astype(F32)
        _online_update(s_new, ok_new, new[:, LANES:].astype(BF16), m_s, l_s, acc_s)
        o_ref[0] = acc_s[...] / l_s[...]


def _slc_sample(cache, pt_flat, qs, sel_rows, kvs_new, slope_rows, qidx_rows, Bs, n_pages, past):
    pp = min(PAGES_PER_STEP, n_pages)
    t_new = kvs_new.shape[1]
    full = lambda a: pl.BlockSpec(a.shape, lambda b, c, pt: (0,) * a.ndim)
    per_seq = lambda a: pl.BlockSpec((1,) + a.shape[1:], lambda b, c, pt: (b,) + (0,) * (a.ndim - 1))
    return pl.pallas_call(
        functools.partial(_slc_sample_kernel, pp=pp, past=past, t_new=t_new),
        grid_spec=pltpu.PrefetchScalarGridSpec(
            num_scalar_prefetch=1, grid=(Bs, n_pages // pp),
            in_specs=_page_specs((1,) + cache.shape[1:], n_pages, pp)
            + [per_seq(qs), per_seq(sel_rows), per_seq(kvs_new), full(slope_rows), full(qidx_rows)],
            out_specs=pl.BlockSpec((1, ROWS_S, LANES), lambda b, c, pt: (b, 0, 0)),
            scratch_shapes=[pltpu.VMEM((ROWS_S, 1), F32), pltpu.VMEM((ROWS_S, 1), F32),
                            pltpu.VMEM((ROWS_S, LANES), F32)]),
        out_shape=jax.ShapeDtypeStruct((Bs, ROWS_S, LANES), F32),
        compiler_params=_cp("parallel", "arbitrary"),
        name="selected_sample",
    )(pt_flat, *([cache] * pp), qs, sel_rows, kvs_new, slope_rows, qidx_rows)


def _win_sample_kernel(q_ref, win_ref, new_ref, oc_ref, os_ref, gates_ref, slope_ref, qidx_ref, o_ref, *, t_new):
    q = q_ref[0]
    n_win = win_ref.shape[-1]
    k_t = win_ref[0, 0].reshape(NSA_G * HD, n_win).astype(BF16)
    v_t = win_ref[0, 1].reshape(NSA_G * HD, n_win).astype(BF16)
    new = _pad_rows8(new_ref[0])
    kj = _iota((1, n_win + 8), 1)
    d_w = n_win + qidx_ref[...] - kj
    mask = (d_w >= 0) & (d_w < WINDOW) & (kj < n_win + t_new)
    s = jnp.concatenate([_dot(q, k_t), _dot_nt(q, new[:, :LANES].astype(BF16))], axis=1)
    p = _masked_softmax(s - slope_ref[...] * d_w.astype(F32), mask).astype(BF16)
    o_w = _dot_nt(p[:, :n_win], v_t) + _dot(p[:, n_win:], new[:, LANES:].astype(BF16))
    gates = gates_ref[0]
    o_ref[0] = gates[:, 0:1] * oc_ref[0] + gates[:, 1:2] * os_ref[0] + gates[:, 2:3] * o_w


def _win_sample(qs, state_win, kvw_new, oc, o_s, gate_rows, slope_rows, qidx_rows, Bs):
    t_new = kvw_new.shape[1]
    full = lambda a: pl.BlockSpec(a.shape, lambda b: (0,) * a.ndim)
    per_seq = lambda a: pl.BlockSpec((1,) + a.shape[1:], lambda b: (b,) + (0,) * (a.ndim - 1))
    return pl.pallas_call(
        functools.partial(_win_sample_kernel, t_new=t_new),
        grid=(Bs,),
        in_specs=[per_seq(qs), per_seq(state_win), per_seq(kvw_new), per_seq(oc), per_seq(o_s),
                  per_seq(gate_rows), full(slope_rows), full(qidx_rows)],
        out_specs=pl.BlockSpec((1, ROWS_S, LANES), lambda b: (b, 0, 0)),
        out_shape=jax.ShapeDtypeStruct((Bs, ROWS_S, LANES), F32),
        compiler_params=_cp("parallel"),
        name="window_sample",
    )(qs, state_win, kvw_new, oc, o_s, gate_rows, slope_rows, qidx_rows)


ROWS_F = FOX_H * 8
FOX_K = FOX_H * HD


def _fox_sample_kernel(pt_ref, *refs, pp, t_new):
    pages = refs[:pp]
    q_ref, d_ref, new_ref, dnew_ref, qidx_ref, o_ref, m_s, l_s, acc_s = refs[pp:]
    c = pl.program_id(1)

    @pl.when(c == 0)
    def _():
        m_s[...] = jnp.full(m_s.shape, NEG, F32)
        l_s[...] = jnp.zeros(l_s.shape, F32)
        acc_s[...] = jnp.zeros(acc_s.shape, F32)

    q = q_ref[0]
    d = d_ref[0]
    nk = d.shape[1]
    bias = jnp.concatenate([jnp.broadcast_to(d[h:h + 1], (8, nk)) for h in range(FOX_H)], axis=0)
    s = _dot(q, _pages_t(pages, 0, FOX_K)) - bias * LOG2E
    _online_update(s, jnp.full((ROWS_F, nk), True), _pages_t(pages, 1, FOX_K), m_s, l_s, acc_s,
                   base2=True, v_t=True)

    @pl.when(c == pl.num_programs(1) - 1)
    def _():
        new = _pad_rows8(new_ref[0])
        kj = _iota((1, 8), 1)
        ok_new = (kj <= qidx_ref[...]) & (kj < t_new)
        s_new = _dot_nt(q, new[:, :FOX_K].astype(BF16)) - dnew_ref[0] * LOG2E
        _online_update(s_new, ok_new, new[:, FOX_K:].astype(BF16), m_s, l_s, acc_s, base2=True)
        o_ref[0] = acc_s[...] / l_s[...]


def _fox_sample(cache, pt_flat, qf_bd, d_past_t, kvf_new, dnew_rows, qidx_f, Bs, n_pages):
    pp = min(PAGES_PER_STEP, n_pages)
    page = cache.shape[-1]
    t_new = kvf_new.shape[1]
    full = lambda a: pl.BlockSpec(a.shape, lambda b, c, pt: (0,) * a.ndim)
    per_seq = lambda a: pl.BlockSpec((1,) + a.shape[1:], lambda b, c, pt: (b,) + (0,) * (a.ndim - 1))
    return pl.pallas_call(
        functools.partial(_fox_sample_kernel, pp=pp, t_new=t_new),
        grid_spec=pltpu.PrefetchScalarGridSpec(
            num_scalar_prefetch=1, grid=(Bs, n_pages // pp),
            in_specs=_page_specs((1,) + cache.shape[1:], n_pages, pp)
            + [per_seq(qf_bd), pl.BlockSpec((1, FOX_H, pp * page), lambda b, c, pt: (b, 0, c)),
               per_seq(kvf_new), per_seq(dnew_rows), full(qidx_f)],
            out_specs=pl.BlockSpec((1, ROWS_F, FOX_K), lambda b, c, pt: (b, 0, 0)),
            scratch_shapes=[pltpu.VMEM((ROWS_F, 1), F32), pltpu.VMEM((ROWS_F, 1), F32),
                            pltpu.VMEM((ROWS_F, FOX_K), F32)]),
        out_shape=jax.ShapeDtypeStruct((Bs, ROWS_F, FOX_K), F32),
        compiler_params=_cp("parallel", "arbitrary"),
        name="fox_sample",
    )(pt_flat, *([cache] * pp), qf_bd, d_past_t, kvf_new, dnew_rows, qidx_f)


def _alibi_slopes():
    return jnp.exp2(-8.0 * jnp.arange(1, NSA_H + 1, dtype=F32) / NSA_H)


def _ffn_tail(o_n, o_f, x2, mod, norms, lw, rows_per_mod):
    gate1, scale2, shift2, gate2 = mod
    g_post_mix, g_pre_ffn, g_post_ffn = norms
    x1, h2, gd = _out_proj(o_n, o_f, x2, gate1, scale2, shift2, g_post_mix, g_pre_ffn,
                           lw["wo"], lw["wr"], lw["br"], rows_per_mod)
    routed = _moe_routed(h2, gd, lw["weg"], lw["weu"], lw["wed"])
    return _final(routed, h2, x1, gate2, g_post_ffn, lw["wsg"], lw["wsu"], lw["wsd"], rows_per_mod)


def _prompt_layer(x, mod, norms, lw):
    B, T, d = x.shape
    n = B * T
    shift1, scale1, gate1, shift2, scale2, gate2 = mod
    g_pre_mix, g_post_mix, g_pre_ffn, g_post_ffn = norms
    x2 = x.reshape(n, d)
    pr = _in_proj(x2, scale1, shift1, g_pre_mix, lw["in_cols"], lw["b_small"], T, "prompt")
    small = pr["small"]
    logf = small[:, N_GATE:N_GATE + FOX_H]
    gates_g = small[:, :N_GATE].reshape(n, NSA_G, HG * 3).transpose(1, 0, 2)
    gates_g = jnp.pad(gates_g, ((0, 0), (0, 0), (0, LANES - HG * 3)))
    slopes = _alibi_slopes()

    cmp = _compress_prompt(pr["kvc"], B, T, lw["cmp_w"])
    oc, selneg = _nsa_select(slopes, pr["qn"], cmp, B, T)
    o_n = _nsa_attend(slopes, pr["qn"], pr, selneg, oc, gates_g, B, T)

    d_t = _cumsum_prompt(logf.reshape(B, T, FOX_H).transpose(0, 2, 1))
    o_f = _fox_attend(pr["qf"], pr["kf_op"], pr["vf_op"], d_t, B, T)

    y = _ffn_tail(o_n, o_f, x2, (gate1, scale2, shift2, gate2), (g_post_mix, g_pre_ffn, g_post_ffn), lw, T)
    win_rows = min(WINDOW, T)
    state = lambda a, heads: jnp.moveaxis(a.reshape(B, 2, heads, HD, -1), -1, 1)
    states = (state(pr["kvc_t"], NSA_G), state(pr["kvs_t"], NSA_G), state(pr["kvf_t"], FOX_H),
              logf.reshape(B, T, FOX_H), state(pr["kvw_t"][:, :, T - win_rows:], NSA_G))
    return y.reshape(B, T, d), states


def _sample_layer(x, mod, norms, lw, caches):
    Bs, Tn, d = x.shape
    n = Bs * Tn
    assert Tn <= 8
    cache_cmp, cache_slc, cache_fox, cache_logf, state_win, page_table = caches
    n_pages = page_table.shape[1]
    page = cache_cmp.shape[1]
    past = n_pages * page
    shift1, scale1, gate1, shift2, scale2, gate2 = mod
    g_pre_mix, g_post_mix, g_pre_ffn, g_post_ffn = norms
    x2 = x.reshape(n, d)
    pr = _in_proj(x2, scale1, shift1, g_pre_mix, lw["in_cols"], lw["b_small"], Tn, "sample")
    small = pr["small"]
    logf = small[:, N_GATE:N_GATE + FOX_H].reshape(Bs, Tn, FOX_H)
    pt_flat = page_table.reshape(-1)
    padq = ((0, 0), (0, 0), (0, 0), (0, 8 - Tn), (0, 0))

    qn = pr["qn"].reshape(Bs, Tn, NSA_G, HG, HD).transpose(0, 2, 3, 1, 4)
    qn = jnp.pad(qn, padq)
    z = jnp.zeros_like(qn[:, 0])
    qs = jnp.stack([jnp.concatenate([qn[:, 0], z], -1), jnp.concatenate([z, qn[:, 1]], -1)], axis=1)
    qs = qs.reshape(Bs, ROWS_S, LANES)
    slope_rows = jnp.broadcast_to(_alibi_slopes().reshape(NSA_H, 1, 1), (NSA_H, 8, 1)).reshape(ROWS_S, 1)
    qidx_rows = jnp.broadcast_to(jnp.arange(8, dtype=jnp.int32).reshape(1, 8, 1), (NSA_H, 8, 1)).reshape(ROWS_S, 1)
    gate_rows = small[:, :N_GATE].reshape(Bs, Tn, NSA_H, 3).transpose(0, 2, 1, 3)
    gate_rows = jnp.pad(gate_rows, ((0, 0), (0, 0), (0, 8 - Tn), (0, LANES - 3))).reshape(Bs, ROWS_S, LANES)

    tok_minor = lambda a: jnp.moveaxis(a, 1, -1)
    cmp_s = _compress_sample(tok_minor(cache_cmp), pt_flat, Bs, n_pages, lw["cmp_sw"])
    oc, sel_rows = _select_sample(qs, cmp_s, slope_rows, qidx_rows, Bs, past)
    o_s = _slc_sample(tok_minor(cache_slc), pt_flat, qs, sel_rows, pr["kvs"].reshape(Bs, Tn, KVW),
                      slope_rows, qidx_rows, Bs, n_pages, past)
    win = tok_minor(state_win)
    o_rows = _win_sample(qs, win, pr["kvw"].reshape(Bs, Tn, KVW), oc, o_s, gate_rows, slope_rows, qidx_rows, Bs)
    o_rows = o_rows.reshape(Bs, NSA_G, HG, 8, NSA_G, HD)[:, :, :, :Tn]
    o_n = jnp.stack([o_rows[:, 0, :, :, 0], o_rows[:, 1, :, :, 1]], axis=1)
    o_n = o_n.transpose(0, 3, 1, 2, 4).reshape(n, QW).astype(BF16)

    qf = jnp.pad(pr["qf"].reshape(Bs, Tn, FOX_H, HD).transpose(0, 2, 1, 3), ((0, 0), (0, 0), (0, 8 - Tn), (0, 0)))
    qf_bd = (qf[:, :, :, None, :] * jnp.eye(FOX_H, dtype=BF16)[None, :, None, :, None]).reshape(Bs, ROWS_F, FOX_K)
    d_past_t = _cumsum_sample(cache_logf.transpose(0, 2, 1), pt_flat, Bs, n_pages)
    d_new = d_past_t[:, :, past - 1:] + jnp.cumsum(logf, axis=1).transpose(0, 2, 1)
    dnew_rows = jnp.broadcast_to(jnp.pad(d_new, ((0, 0), (0, 0), (0, 8 - Tn)))[:, :, None, :],
                                 (Bs, FOX_H, 8, 8)).reshape(Bs, ROWS_F, 8)
    qidx_f = jnp.broadcast_to(jnp.arange(8, dtype=jnp.int32).reshape(1, 8, 1), (FOX_H, 8, 1)).reshape(ROWS_F, 1)
    o_frows = _fox_sample(tok_minor(cache_fox), pt_flat, qf_bd, d_past_t,
                          pr["kvf"].reshape(Bs, Tn, KVF), dnew_rows, qidx_f, Bs, n_pages)
    o_frows = o_frows.reshape(Bs, FOX_H, 8, FOX_H, HD)[:, :, :Tn]
    o_f = jnp.stack([o_frows[:, h, :, h] for h in range(FOX_H)], axis=2)
    o_f = o_f.reshape(n, FOX_H * HD).astype(BF16)

    y = _ffn_tail(o_n, o_f, x2, (gate1, scale2, shift2, gate2), (g_post_mix, g_pre_ffn, g_post_ffn), lw, Tn)
    kvw_new = pr["kvw"].reshape(Bs, Tn, 2, NSA_G, HD)
    states = (pr["kvc"].reshape(Bs, Tn, 2, NSA_G, HD), pr["kvs"].reshape(Bs, Tn, 2, NSA_G, HD),
              pr["kvf"].reshape(Bs, Tn, 2, FOX_H, HD), logf,
              jnp.concatenate([state_win, kvw_new], axis=1)[:, Tn:])
    return y.reshape(Bs, Tn, d), states


def _layer_weights(w_in, b_nsa_gate, b_forget, w_cmp1, b_cmp1, w_cmp2, pe_cmp, w_out, w_router, b_router,
                   w_exp_gate, w_exp_up, w_exp_down, w_sh_gate, w_sh_up, w_sh_down):
    in_cols, b_small = _in_weights(w_in, b_nsa_gate, b_forget)
    n_e = w_router.shape[1]
    wr = jnp.pad(w_router, ((0, 0), (0, LANES - n_e))).astype(BF16)
    br = jnp.concatenate([b_router.astype(F32), jnp.full((LANES - n_e,), NEG_INF, F32)]).reshape(1, LANES)
    return dict(in_cols=in_cols, b_small=b_small,
                cmp_w=_cmp_weights(w_cmp1, b_cmp1, w_cmp2, pe_cmp),
                cmp_sw=_cmp_sample_weights(w_cmp1, b_cmp1, w_cmp2, pe_cmp),
                wo=w_out.astype(BF16), wr=wr, br=br,
                weg=w_exp_gate.astype(BF16), weu=w_exp_up.astype(BF16), wed=w_exp_down.astype(BF16),
                wsg=w_sh_gate.astype(BF16), wsu=w_sh_up.astype(BF16), wsd=w_sh_down.astype(BF16))


def kernel(x_prompt, x_sample, c_prompt, c_sample, cache_cmp_kv, cache_slc_kv, cache_fox_kv, cache_fox_logf,
           state_win_kv, page_table, w_ada, b_ada, g_pre_mix, g_post_mix, g_pre_ffn, g_post_ffn, w_in,
           b_nsa_gate, b_forget, w_cmp1, b_cmp1, w_cmp2, pe_cmp, w_out, w_router, b_router, w_exp_gate,
           w_exp_up, w_exp_down, w_sh_gate, w_sh_up, w_sh_down):
    depth = w_in.shape[0]
    Bp = x_prompt.shape[0]
    xp, xs = x_prompt, x_sample
    st_p = [[] for _ in range(5)]
    st_s = [[] for _ in range(5)]
    for l in range(depth):
        lw = _layer_weights(w_in[l], b_nsa_gate[l], b_forget[l], w_cmp1[l], b_cmp1[l], w_cmp2[l], pe_cmp[l],
                            w_out[l], w_router[l], b_router[l], w_exp_gate[l], w_exp_up[l], w_exp_down[l],
                            w_sh_gate[l], w_sh_up[l], w_sh_down[l])
        m = _modulation(jnp.concatenate([c_prompt, c_sample], axis=0), w_ada[l], b_ada[l])
        mods = jnp.split(m[:, None, :], 6, axis=-1)
        norms = (g_pre_mix[l], g_post_mix[l], g_pre_ffn[l], g_post_ffn[l])
        xp, sp = _prompt_layer(xp, [a[:Bp] for a in mods], norms, lw)
        caches = (cache_cmp_kv[l], cache_slc_kv[l], cache_fox_kv[l], cache_fox_logf[l], state_win_kv[l], page_table)
        xs, ss = _sample_layer(xs, [a[Bp:] for a in mods], norms, lw, caches)
        for lst, a in zip(st_p, sp):
            lst.append(a)
        for lst, a in zip(st_s, ss):
            lst.append(a)
    return (xp, xs, *[jnp.stack(s) for s in st_p], *[jnp.stack(s) for s in st_s])
```

```python
import functools

import numpy as np
import jax
import jax.numpy as jnp
from jax import lax
from jax.experimental import pallas as pl
from jax.experimental.pallas import tpu as pltpu

F32 = jnp.float32
BF16 = jnp.bfloat16

HD = 64
NSA_H = 8
NSA_G = 2
HG = NSA_H // NSA_G
FOX_H = 8
CMP_STRIDE = 16
CMP_LEN = 32
CMP_HID = 128
SLC_BLOCK = 64
SLC_TOPK = 16
WINDOW = 512
TOP_K = 8
ROUTED_SCALE = 2.5
RMS_EPS = 1e-6
NEG = -1e30
TINY = 1e-30
FORCE = 1e4
NEG_INF = float("-inf")

LANES = 128
KVW = 2 * NSA_G * HD
KVF = 2 * FOX_H * HD
QW = NSA_H * HD
VMEM_LIMIT = 56 * 1024 * 1024


def _cp(*sem):
    return pltpu.CompilerParams(dimension_semantics=sem, vmem_limit_bytes=VMEM_LIMIT)


def _rms(x, g):
    return x * lax.rsqrt(jnp.mean(x * x, axis=-1, keepdims=True) + RMS_EPS) * g


def _dot(a, b):
    return jnp.dot(a, b, preferred_element_type=F32)


def _dot_nt(a, b):
    return lax.dot_general(a, b, (((1,), (1,)), ((), ())), preferred_element_type=F32)


def _iota(shape, dim):
    return lax.broadcasted_iota(jnp.int32, shape, dim)


def _onehot(cond):
    return jnp.where(cond, 1.0, 0.0).astype(BF16)


def _pick_head(width_in, h_off):
    r = _iota((width_in, LANES), 0)
    c = _iota((width_in, LANES), 1)
    return _onehot((r == c + h_off) & (c < HD))


def _place_head(width_out, off):
    r = _iota((LANES, width_out), 0)
    c = _iota((LANES, width_out), 1)
    return _onehot((r >= HD) & (c == r - HD + off))


def _split3(x):
    hi = x.astype(BF16)
    r1 = x - hi.astype(F32)
    mid = r1.astype(BF16)
    lo = (r1 - mid.astype(F32)).astype(BF16)
    return hi, mid, lo


def _mod_kernel(c_ref, w_ref, b_ref, o_ref):
    c = c_ref[...]
    a = (c * jax.nn.sigmoid(c)).astype(BF16)
    o_ref[...] = _dot(a, w_ref[...].astype(BF16)) + b_ref[...]


def _modulation(c_all, w_ada, b_ada):
    nb, d = c_all.shape
    n_chunk = w_ada.shape[1] // d
    return pl.pallas_call(
        _mod_kernel,
        grid=(n_chunk,),
        in_specs=[pl.BlockSpec((nb, d), lambda i: (0, 0)),
                  pl.BlockSpec((d, d), lambda i: (0, i)),
                  pl.BlockSpec((1, d), lambda i: (0, i))],
        out_specs=pl.BlockSpec((nb, d), lambda i: (0, i)),
        out_shape=jax.ShapeDtypeStruct((nb, w_ada.shape[1]), F32),
        compiler_params=_cp("parallel"),
        name="modulation",
    )(c_all, w_ada, b_ada.reshape(1, -1))


N_GATE = 3 * NSA_H
SEL_TK = 256
LOG2E = 1.4426950408889634
_IN_BASE = (("kvc", KVW), ("kvs", KVW), ("kvw", KVW), ("kvf", KVF), ("small", LANES), ("qn", QW), ("qf", QW))
_IN_OUTS = {
    "prompt": (("kvc", "f32", KVW), ("small", "f32", LANES), ("qn", "bf16", QW), ("qf", "bf16", QW),
               ("ks_op", "bf16", NSA_G * LANES), ("vs_op", "bf16", NSA_G * LANES),
               ("kw_op", "bf16", NSA_G * LANES), ("vw_op", "bf16", NSA_G * LANES),
               ("kf_op", "bf16", FOX_H * LANES), ("vf_op", "bf16", FOX_H * LANES),
               ("kvc_t", "f32_t", KVW), ("kvs_t", "f32_t", KVW), ("kvw_t", "f32_t", KVW), ("kvf_t", "f32_t", KVF)),
    "sample": (("kvc", "f32", KVW), ("kvs", "f32", KVW), ("kvw", "f32", KVW), ("kvf", "f32", KVF),
               ("small", "f32", LANES), ("qn", "bf16", QW), ("qf", "bf16", QW)),
}


def _k_operand(kk, fill):
    lo = _iota(kk.shape, 1) < HD
    return jnp.concatenate([jnp.where(lo, kk, fill), jnp.where(lo, pltpu.roll(kk, HD, 1), fill)], axis=1)


def _v_operand(vv):
    lo = _iota(vv.shape, 1) < HD
    return jnp.concatenate([jnp.where(lo, 1.0, pltpu.roll(vv, HD, 1)), jnp.where(lo, 1.0, vv)], axis=1)


def _in_kernel(x_ref, sc_ref, sh_ref, g_ref, w_ref, bias_ref, *out_refs, outs, seq_len):
    x = x_ref[...]
    tm = x.shape[0]
    h = _rms(x, g_ref[...]) * (1.0 + sc_ref[...]) + sh_ref[...]
    hb = h.astype(BF16)
    base, off = {}, 0
    for name, width in _IN_BASE:
        base[name] = _dot(hb, w_ref[:, off:off + width])
        off += width
    z = base["small"] + bias_ref[...]
    logsig = jnp.minimum(z, 0.0) - jnp.log1p(jnp.exp(-jnp.abs(z)))
    base["small"] = jnp.where(_iota(z.shape, 1) < N_GATE, jax.nn.sigmoid(z), logsig)
    lane = _iota((tm, LANES), 1)
    nk, fk = NSA_G * HD, FOX_H * HD
    for (name, kind, _), o_ref in zip(outs, out_refs):
        if kind == "f32_t":
            o_ref[0] = base[name[:-2]].T
            continue
        if name == "ks_op":
            t = (pl.program_id(0) * tm) % seq_len + _iota((tm, LANES), 0)
            pos = jnp.where(lane == HD, (t & (SEL_TK - 1)).astype(F32), 0.0)
            val = _k_operand(base["kvs"][:, :nk], pos)
        elif name == "kw_op":
            val = _k_operand(base["kvw"][:, :nk], 0.0)
        elif name == "vs_op":
            val = _v_operand(base["kvs"][:, nk:])
        elif name == "vw_op":
            val = _v_operand(base["kvw"][:, nk:])
        elif name == "kf_op":
            kf = base["kvf"][:, :fk]
            val = jnp.concatenate(
                [jnp.where((lane < HD) == (i % 2 == 0), kf[:, (i // 2) * LANES:(i // 2 + 1) * LANES], 0.0)
                 for i in range(FOX_H)], axis=1)
        elif name == "vf_op":
            vf = base["kvf"][:, fk:]
            val = jnp.concatenate([_v_operand(vf[:, j * LANES:(j + 1) * LANES]) for j in range(FOX_H // 2)], axis=1)
        else:
            val = base[name]
        o_ref[...] = val.astype(o_ref.dtype)


def _in_weights(w_in, b_nsa_gate, b_forget):
    d = w_in.shape[0]
    o_qn, o_kvc, o_kvs, o_kvw = 0, QW, QW + KVW, QW + 2 * KVW
    o_gn = QW + 3 * KVW
    o_qf = o_gn + N_GATE
    o_kvf = o_qf + QW
    o_ff = o_kvf + KVF
    scale = HD ** -0.5
    cols = dict(
        kvc=w_in[:, o_kvc:o_kvc + KVW], kvs=w_in[:, o_kvs:o_kvs + KVW], kvw=w_in[:, o_kvw:o_kvw + KVW],
        kvf=w_in[:, o_kvf:o_kvf + KVF],
        small=jnp.concatenate([w_in[:, o_gn:o_gn + N_GATE], w_in[:, o_ff:o_ff + FOX_H],
                               jnp.zeros((d, LANES - N_GATE - FOX_H), F32)], axis=1),
        qn=w_in[:, o_qn:o_qn + QW] * scale, qf=w_in[:, o_qf:o_qf + QW] * (scale * LOG2E))
    w = jnp.concatenate([cols[name] for name, _ in _IN_BASE], axis=1).astype(BF16)
    bias = jnp.concatenate([b_nsa_gate, b_forget, jnp.zeros((LANES - N_GATE - FOX_H,), F32)]).reshape(1, LANES)
    return w, bias


def _in_proj(x2, scale, shift, g_pre, w, bias, rows_per_mod, group):
    n, d = x2.shape
    tm = min(256, n)
    outs = _IN_OUTS[group]
    if rows_per_mod >= tm:
        per = rows_per_mod // tm
        mod_spec = pl.BlockSpec((None, 1, d), lambda i: (i // per, 0, 0))
    else:
        assert not any(kind == "f32_t" for _, kind, _ in outs)
        per = 1
        scale = jnp.repeat(scale[:, 0, :], rows_per_mod, axis=0)
        shift = jnp.repeat(shift[:, 0, :], rows_per_mod, axis=0)
        mod_spec = pl.BlockSpec((tm, d), lambda i: (i, 0))
    n_seq = n // rows_per_mod
    out_specs, out_shape = [], []
    for _, kind, width in outs:
        if kind == "f32_t":
            out_specs.append(pl.BlockSpec((1, width, tm), lambda i: (i // per, 0, i % per)))
            out_shape.append(jax.ShapeDtypeStruct((n_seq, width, rows_per_mod), F32))
        else:
            out_specs.append(pl.BlockSpec((tm, width), lambda i: (i, 0)))
            out_shape.append(jax.ShapeDtypeStruct((n, width), F32 if kind == "f32" else BF16))
    full = lambda a: pl.BlockSpec(a.shape, lambda i: (0,) * a.ndim)
    res = pl.pallas_call(
        functools.partial(_in_kernel, outs=outs, seq_len=rows_per_mod),
        grid=(n // tm,),
        in_specs=[pl.BlockSpec((tm, d), lambda i: (i, 0)), mod_spec, mod_spec,
                  pl.BlockSpec((1, d), lambda i: (0, 0)), full(w), full(bias)],
        out_specs=out_specs,
        out_shape=out_shape,
        compiler_params=_cp("parallel"),
        name="in_proj_" + group,
    )(x2, scale, shift, g_pre.reshape(1, d), w, bias)
    return dict(zip([name for name, _, _ in outs], res))


def _cmp_kernel(subk_ref, subv_ref, w1_ref, pe_ref, b1_ref, w2_ref, o_ref):
    half = CMP_STRIDE * HD
    out = None
    for kv, sub_ref in enumerate((subk_ref, subv_ref)):
        sub = sub_ref[0, 0]
        w1 = w1_ref[kv]
        a = _dot(sub, w1[:half])
        b = _dot(sub, w1[half:])
        n_sub = a.shape[0]
        b_next = pltpu.roll(b, n_sub - 1, 0)
        bias = _dot(pe_ref[kv], w1)[0:1] + b1_ref[kv]
        hid = jax.nn.gelu(a + b_next + bias).astype(BF16)
        term = _dot(hid, w2_ref[kv])
        out = term if out is None else out + term
    o_ref[0, 0] = out.astype(o_ref.dtype)


def _cmp_weights(w_cmp1, b_cmp1, w_cmp2, pe_cmp):
    w1 = w_cmp1.astype(BF16)
    pe = jnp.broadcast_to(pe_cmp.reshape(2, 1, CMP_LEN * HD), (2, 8, CMP_LEN * HD)).astype(BF16)
    b1 = b_cmp1.reshape(2, 1, CMP_HID)
    z = jnp.zeros((CMP_HID, HD), F32)
    w2 = jnp.stack([jnp.concatenate([w_cmp2[0], z], axis=1),
                    jnp.concatenate([z, w_cmp2[1]], axis=1)]).astype(BF16)
    return w1, pe, b1, w2


def _compress_prompt(kvc, B, T, cw):
    w1, pe, b1, w2 = cw
    n_sub = T // CMP_STRIDE
    sub = kvc.reshape(B, n_sub, CMP_STRIDE, 2 * NSA_G, HD).transpose(0, 3, 1, 2, 4)
    sub = sub.reshape(B, 2 * NSA_G, n_sub, CMP_STRIDE * HD).astype(BF16)
    feat = CMP_STRIDE * HD
    return pl.pallas_call(
        _cmp_kernel,
        grid=(B, NSA_G),
        in_specs=[pl.BlockSpec((1, 1, n_sub, feat), lambda b, g: (b, g, 0, 0)),
                  pl.BlockSpec((1, 1, n_sub, feat), lambda b, g: (b, NSA_G + g, 0, 0)),
                  pl.BlockSpec(w1.shape, lambda b, g: (0, 0, 0)),
                  pl.BlockSpec(pe.shape, lambda b, g: (0, 0, 0)),
                  pl.BlockSpec(b1.shape, lambda b, g: (0, 0, 0)),
                  pl.BlockSpec(w2.shape, lambda b, g: (0, 0, 0))],
        out_specs=pl.BlockSpec((1, 1, n_sub, LANES), lambda b, g: (b, g, 0, 0)),
        out_shape=jax.ShapeDtypeStruct((B, NSA_G, n_sub, LANES), BF16),
        compiler_params=_cp("parallel", "parallel"),
        name="compress_prompt",
    )(sub, sub, w1, pe, b1, w2)


def _cumsum_block(x):
    n = x.shape[1]
    u = _onehot(_iota((n, n), 0) <= _iota((n, n), 1))
    r = x.shape[0]
    d3 = _dot(jnp.concatenate(_split3(x), axis=0), u)
    return d3[:r] + d3[r:2 * r] + d3[2 * r:]


def _cumsum_kernel(x_ref, o_ref, carry_ref):
    @pl.when(pl.program_id(1) == 0)
    def _():
        carry_ref[...] = jnp.zeros_like(carry_ref)

    d = _cumsum_block(x_ref[0]) + carry_ref[:, 0:1]
    o_ref[0] = d
    carry_ref[...] = jnp.broadcast_to(d[:, d.shape[1] - 1:], carry_ref.shape)


def _cumsum_prompt(logf_t):
    B, H, T = logf_t.shape
    tc = min(512, T)
    return pl.pallas_call(
        _cumsum_kernel,
        grid=(B, T // tc),
        in_specs=[pl.BlockSpec((1, H, tc), lambda b, c: (b, 0, c))],
        out_specs=pl.BlockSpec((1, H, tc), lambda b, c: (b, 0, c)),
        out_shape=jax.ShapeDtypeStruct((B, H, T), F32),
        scratch_shapes=[pltpu.VMEM((H, LANES), F32)],
        compiler_params=_cp("parallel", "arbitrary"),
        name="cumsum_prompt",
    )(logf_t)


MASK_BIG = 2.0 ** 30


def _topk_mask(val, n_take, axis):
    idx_f = _iota(val.shape, axis).astype(F32)
    sel = jnp.zeros(val.shape, F32)
    for _ in range(n_take):
        m = jnp.max(val, axis=axis, keepdims=True)
        first = jnp.min(jnp.where(val == m, idx_f, float(val.shape[axis])), axis=axis, keepdims=True)
        pick = idx_f == first
        sel = jnp.where(pick, 1.0, sel)
        val = jnp.where(pick, NEG_INF, val)
    return sel


def _masked_softmax(s, mask, axis=1):
    s = jnp.where(mask, s, NEG)
    m = jnp.max(s, axis=axis, keepdims=True)
    e = jnp.where(mask, jnp.exp(s - m), 0.0)
    return e / jnp.maximum(jnp.sum(e, axis=axis, keepdims=True), TINY)


def _flash_step(s, v, m_ref, acc_ref, base2):
    nk = s.shape[1]
    ex = jnp.exp2 if base2 else jnp.exp
    m_old = m_ref[...]
    m_new = jnp.maximum(m_old, jnp.max(s, axis=1, keepdims=True))
    alpha = ex(m_old - m_new)
    e = jnp.concatenate([ex(s[:, c * LANES:(c + 1) * LANES] - m_new) for c in range(nk // LANES)], axis=1)
    acc_ref[...] = alpha * acc_ref[...] + _dot(e.astype(BF16), v)
    m_ref[...] = m_new


def _hi_over_lo(acc):
    return acc / jnp.maximum(pltpu.roll(acc, HD, 1), TINY)


def _pair_heads(o_even, o_odd):
    lane = _iota(o_even.shape, 1)
    return jnp.where(lane < HD, pltpu.roll(o_even, HD, 1), o_odd)


def _cmp_to_slc_np(n_slots, first_token):
    tok = np.arange(n_slots)[:, None] + first_token
    start = tok * CMP_STRIDE
    bstart = np.arange(LANES)[None, :] * SLC_BLOCK
    m = (start < bstart + SLC_BLOCK) & (start + CMP_LEN > bstart) & (tok >= 0)
    return m.astype(np.float32)


def _nsa_sel_kernel(slopes_ref, q_ref, cmp_ref, cmpt_ref, c2st_ref, oc_ref, sel_ref, *, tq, n_sel):
    g = pl.program_id(1)
    t0 = pl.program_id(2) * tq
    n_slots = cmp_ref.shape[2]
    qpos = t0 + _iota((1, tq), 1)
    cend = _iota((n_slots, 1), 0) * CMP_STRIDE + (CMP_LEN - 1)
    d_c = qpos - cend
    mask_c = d_c >= 0
    d_cf = d_c.astype(F32)
    q = q_ref[...]
    cmp = cmp_ref[0, 0]
    cmpt = cmpt_ref[0, 0]
    c2st = c2st_ref[...]
    imp_t = jnp.zeros((LANES, tq), F32)
    for h in range(HG):
        q128 = _dot(q, _pick_head(HG * HD, h * HD)).astype(BF16)
        s_t = _dot_nt(cmp, q128) - slopes_ref[g * HG + h] * d_cf
        p_t = _masked_softmax(s_t, mask_c, axis=0).astype(BF16)
        oc_ref[0, 0, h] = _dot(cmpt, p_t).T
        imp_t = imp_t + _dot(c2st, p_t)
    blk = _iota((LANES, tq), 0)
    cur = qpos >> 6
    forced = (blk == 0) | (blk == cur) | (blk == cur - 1)
    valid = blk * SLC_BLOCK <= qpos
    val = jnp.where(valid, imp_t + jnp.where(forced, FORCE, 0.0), NEG)
    sel_t = jnp.where(valid, _topk_mask(val, n_sel, 0), 0.0)
    sel_ref[0, 0] = ((sel_t - 1.0) * MASK_BIG).T.astype(sel_ref.dtype)


NSA_TQ_SEL, NSA_TQ_ATT = 512, 256


def _nsa_select(slopes, qn, cmp, B, T):
    tq = min(NSA_TQ_SEL, T)
    nq = T // tq
    n_slots = cmp.shape[2]
    n_slc = -(-T // SLC_BLOCK)
    assert n_slc <= LANES and SLC_BLOCK == 64
    c2st = jnp.asarray(_cmp_to_slc_np(n_slots, 0).T, BF16)
    cmpt = jnp.swapaxes(cmp, 2, 3)
    return pl.pallas_call(
        functools.partial(_nsa_sel_kernel, tq=tq, n_sel=min(SLC_TOPK, n_slc)),
        grid=(B, NSA_G, nq),
        in_specs=[pl.BlockSpec(memory_space=pltpu.SMEM),
                  pl.BlockSpec((tq, HG * HD), lambda b, g, j: (b * nq + j, g)),
                  pl.BlockSpec((1, 1, n_slots, LANES), lambda b, g, j: (b, g, 0, 0)),
                  pl.BlockSpec((1, 1, LANES, n_slots), lambda b, g, j: (b, g, 0, 0)),
                  pl.BlockSpec((LANES, n_slots), lambda b, g, j: (0, 0))],
        out_specs=[pl.BlockSpec((1, 1, HG, tq, LANES), lambda b, g, j: (b, g, 0, j, 0)),
                   pl.BlockSpec((1, 1, tq, LANES), lambda b, g, j: (b, g, j, 0))],
        out_shape=[jax.ShapeDtypeStruct((B, NSA_G, HG, T, LANES), F32),
                   jax.ShapeDtypeStruct((B, NSA_G, T, LANES), BF16)],
        compiler_params=_cp("parallel", "parallel", "parallel"),
        name="nsa_select",
    )(slopes, qn, cmp, cmpt, c2st)


def _tile_bits(selneg, B, T, tq, tk):
    nq = T // tq
    n_tiles = T // tk
    bpt = tk // SLC_BLOCK
    assert n_tiles <= 32
    anyb = (selneg.reshape(B, NSA_G, nq, tq, LANES) == 0).any(axis=3)
    anyt = anyb[..., :n_tiles * bpt].reshape(B, NSA_G, nq, n_tiles, bpt).any(axis=-1)
    w = jnp.sum(anyt.astype(jnp.uint32) << jnp.arange(n_tiles, dtype=jnp.uint32), axis=-1, dtype=jnp.uint32)
    return lax.bitcast_convert_type(w, jnp.int32).reshape(-1)


def _nsa_attn_kernel(bits_ref, slopes_ref, q_ref, ks_ref, vs_ref, kw_ref, vw_ref, blk1h_ref, sel_ref, oc_ref,
                     gates_ref, o_ref, m_s, acc_s, *, tq, tk, wk):
    b, g, j = pl.program_id(0), pl.program_id(1), pl.program_id(2)
    t0 = j * tq
    q = q_ref[...]
    selneg = sel_ref[0, 0]
    lane = _iota((tq, LANES), 1)
    slopes = [slopes_ref[g * HG + h] for h in range(HG)]
    q_sel, q_win = [], []
    for h in range(HG):
        q128 = _dot(q, _pick_head(HG * HD, h * HD))
        q_win.append(q128.astype(BF16))
        q_sel.append(jnp.concatenate([jnp.where(lane == HD, slopes[h], q128).astype(BF16), selneg], axis=1))
    q4 = jnp.concatenate(q_sel, axis=0)
    qw4 = jnp.concatenate(q_win, axis=0)
    qpos = t0 + _iota((tq, 1), 0)

    m_s[...] = jnp.full(m_s.shape, NEG, F32)
    acc_s[...] = jnp.zeros(acc_s.shape, F32)
    word = bits_ref[(b * NSA_G + g) * pl.num_programs(2) + j]

    def tile(kt, diagonal):
        k0 = pl.multiple_of(kt * tk, tk)
        kop = jnp.concatenate([ks_ref[pl.ds(k0, tk), :], blk1h_ref[pl.ds(k0, tk), :]], axis=1)
        vop = vs_ref[pl.ds(k0, tk), :]
        s = _dot_nt(q4, kop)
        off = (k0 - t0).astype(F32)
        if diagonal:
            causal = (k0 + _iota((1, tk), 1)) <= qpos
        for h in range(HG):
            sh = s[h * tq:(h + 1) * tq] + slopes[h] * off
            if diagonal:
                sh = jnp.where(causal, sh, NEG)
            _flash_step(sh, vop, m_s.at[h], acc_s.at[h], False)

    def body(kt, carry):
        @pl.when(((word >> kt) & 1) == 1)
        def _():
            tile(kt, False)
        return carry

    n_before = t0 // tk
    lax.fori_loop(0, n_before, body, 0)
    for dd in range(max(1, tq // tk)):
        tile(n_before + dd, True)

    w0 = pl.multiple_of(jnp.maximum(t0 + tq - wk, 0), tq)
    kw = kw_ref[pl.ds(w0, wk), :]
    vw = vw_ref[pl.ds(w0, wk), :]
    d_w = qpos - (w0 + _iota((1, wk), 1))
    mask_w = (d_w >= 0) & (d_w < WINDOW)
    d_wf = d_w.astype(F32)
    s_w = _dot_nt(qw4, kw)
    gates = gates_ref[0]
    o_heads = []
    for h in range(HG):
        sw = jnp.where(mask_w, s_w[h * tq:(h + 1) * tq] - slopes[h] * d_wf, NEG)
        e_w = jnp.where(mask_w, jnp.exp(sw - jnp.max(sw, axis=1, keepdims=True)), 0.0)
        o_w = _hi_over_lo(_dot(e_w.astype(BF16), vw))
        o_s = _hi_over_lo(acc_s[h])
        o_heads.append(gates[:, 3 * h:3 * h + 1] * oc_ref[0, 0, h] + gates[:, 3 * h + 1:3 * h + 2] * o_s
                       + gates[:, 3 * h + 2:3 * h + 3] * o_w)
    o_ref[...] = jnp.concatenate([_pair_heads(o_heads[0], o_heads[1]), _pair_heads(o_heads[2], o_heads[3])],
                                 axis=1).astype(o_ref.dtype)


def _nsa_attend(slopes, qn, pr, selneg, oc, gates_g, B, T):
    tq = min(NSA_TQ_ATT, T)
    nq = T // tq
    tk = min(SEL_TK, T)
    assert tk == SEL_TK and HG == 4
    wk = min(WINDOW + tq, T)
    bits = _tile_bits(selneg, B, T, tq, tk)
    blk1h = jnp.asarray(np.arange(T)[:, None] // SLC_BLOCK == np.arange(LANES)[None, :], BF16)
    kv_spec = pl.BlockSpec((T, LANES), lambda b, g, j, bits: (b, g))
    return pl.pallas_call(
        functools.partial(_nsa_attn_kernel, tq=tq, tk=tk, wk=wk),
        grid_spec=pltpu.PrefetchScalarGridSpec(
            num_scalar_prefetch=1, grid=(B, NSA_G, nq),
            in_specs=[pl.BlockSpec(memory_space=pltpu.SMEM),
                      pl.BlockSpec((tq, HG * HD), lambda b, g, j, bits: (b * nq + j, g)),
                      kv_spec, kv_spec, kv_spec, kv_spec,
                      pl.BlockSpec((T, LANES), lambda b, g, j, bits: (0, 0)),
                      pl.BlockSpec((1, 1, tq, LANES), lambda b, g, j, bits: (b, g, j, 0)),
                      pl.BlockSpec((1, 1, HG, tq, LANES), lambda b, g, j, bits: (b, g, 0, j, 0)),
                      pl.BlockSpec((1, tq, LANES), lambda b, g, j, bits: (g, b * nq + j, 0))],
            out_specs=pl.BlockSpec((tq, HG * HD), lambda b, g, j, bits: (b * nq + j, g)),
            scratch_shapes=[pltpu.VMEM((HG, tq, LANES), F32), pltpu.VMEM((HG, tq, LANES), F32)]),
        out_shape=jax.ShapeDtypeStruct((B * T, QW), BF16),
        compiler_params=_cp("parallel", "parallel", "parallel"),
        name="nsa_attend",
    )(bits, slopes, qn, pr["ks_op"], pr["vs_op"], pr["kw_op"], pr["vw_op"], blk1h, selneg, oc, gates_g)


FOX_TQ, FOX_TK, FOX_ROWS = 1024, 512, 1024


def _fox_kernel(q_ref, k0_ref, k1_ref, v0_ref, v1_ref, d0_ref, d1_ref, o_ref, m_s, acc_s, *, tq, rows, tk):
    j = pl.program_id(2)
    n_part = tq // rows
    k_refs, v_refs, d_refs = (k0_ref, k1_ref), (v0_ref, v1_ref), (d0_ref, d1_ref)
    m_s[...] = jnp.full(m_s.shape, NEG, F32)
    acc_s[...] = jnp.zeros(acc_s.shape, F32)

    def step(kt, diagonal):
        k0 = pl.multiple_of(kt * tk, tk)
        for hh in range(2):
            k = k_refs[hh][pl.ds(k0, tk), :]
            v = v_refs[hh][pl.ds(k0, tk), :]
            bias = d_refs[hh][0, kt] * LOG2E
            for r in range(n_part):
                s = _dot_nt(q_ref[r * rows:(r + 1) * rows, :], k) - bias
                if diagonal:
                    qpos = j * tq + r * rows + _iota((rows, 1), 0)
                    s = jnp.where(k0 + _iota((1, tk), 1) <= qpos, s, NEG)
                _flash_step(s, v, m_s.at[hh * n_part + r], acc_s.at[hh * n_part + r], True)

    def body(kt, carry):
        step(kt, False)
        return carry

    n_full = (j * tq) // tk
    lax.fori_loop(0, n_full, body, 0)
    for dd in range(tq // tk):
        step(n_full + dd, True)
    for r in range(n_part):
        o_ref[r * rows:(r + 1) * rows, :] = _pair_heads(_hi_over_lo(acc_s[r]),
                                                         _hi_over_lo(acc_s[n_part + r])).astype(o_ref.dtype)


def _fox_attend(qf, kf_op, vf_op, d_t, B, T):
    tq = min(FOX_TQ, T)
    tk = min(FOX_TK, tq)
    rows = min(FOX_ROWS, tq)
    nq = T // tq
    d4 = d_t.reshape(B * FOX_H, T // tk, 1, tk)
    kv_spec = lambda hh: pl.BlockSpec((T, LANES), lambda b, hp, j: (b, 2 * hp + hh))
    d_spec = lambda hh: pl.BlockSpec((1, T // tk, 1, tk), lambda b, hp, j: (b * FOX_H + 2 * hp + hh, 0, 0, 0))
    n_chain = 2 * (tq // rows)
    return pl.pallas_call(
        functools.partial(_fox_kernel, tq=tq, rows=rows, tk=tk),
        grid=(B, FOX_H // 2, nq),
        in_specs=[pl.BlockSpec((tq, 2 * HD), lambda b, hp, j: (b * nq + j, hp)),
                  kv_spec(0), kv_spec(1), kv_spec(0), kv_spec(1), d_spec(0), d_spec(1)],
        out_specs=pl.BlockSpec((tq, 2 * HD), lambda b, hp, j: (b * nq + j, hp)),
        out_shape=jax.ShapeDtypeStruct((B * T, FOX_H * HD), BF16),
        scratch_shapes=[pltpu.VMEM((n_chain, rows, LANES), F32), pltpu.VMEM((n_chain, rows, LANES), F32)],
        compiler_params=_cp("parallel", "parallel", "parallel"),
        name="fox_attend",
    )(qf, kf_op, kf_op, vf_op, vf_op, d4, d4)


def _out_kernel(on_ref, of_ref, x_ref, g1_ref, sc2_ref, sh2_ref, gpm_ref, gpf_ref, wo_ref, wr_ref, br_ref,
                x1_ref, h2_ref, gd_ref):
    y = _dot(on_ref[...], wo_ref[:QW]) + _dot(of_ref[...], wo_ref[QW:])
    x1 = x_ref[...] + g1_ref[...] * _rms(y, gpm_ref[...])
    x1_ref[...] = x1
    h2 = (_rms(x1, gpf_ref[...]) * (1.0 + sc2_ref[...]) + sh2_ref[...]).astype(BF16)
    h2_ref[...] = h2
    scores_t = jax.nn.sigmoid(_dot_nt(wr_ref[...], h2))
    sel_t = _topk_mask(scores_t + br_ref[...], TOP_K, 0)
    g_t = sel_t * scores_t
    gd_ref[...] = (g_t / jnp.sum(g_t, axis=0, keepdims=True) * ROUTED_SCALE).T


def _out_proj(o_n, o_f, x2, gate1, scale2, shift2, g_post_mix, g_pre_ffn, wo, wr, br, rows_per_mod):
    n, d = x2.shape
    tm = min(256, n)
    if rows_per_mod >= tm:
        per = rows_per_mod // tm
        mod_spec = pl.BlockSpec((None, 1, d), lambda i: (i // per, 0, 0))
    else:
        gate1, scale2, shift2 = (jnp.repeat(a[:, 0, :], rows_per_mod, axis=0) for a in (gate1, scale2, shift2))
        mod_spec = pl.BlockSpec((tm, d), lambda i: (i, 0))
    row = lambda w: pl.BlockSpec((tm, w), lambda i: (i, 0))
    full = lambda a: pl.BlockSpec(a.shape, lambda i: (0,) * a.ndim)
    gpm, gpf = g_post_mix.reshape(1, d), g_pre_ffn.reshape(1, d)
    return pl.pallas_call(
        _out_kernel,
        grid=(n // tm,),
        in_specs=[row(QW), row(QW), row(d), mod_spec, mod_spec, mod_spec, full(gpm), full(gpf),
                  full(wo), full(wr), full(br)],
        out_specs=[row(d), row(d), row(LANES)],
        out_shape=[jax.ShapeDtypeStruct((n, d), F32), jax.ShapeDtypeStruct((n, d), BF16),
                   jax.ShapeDtypeStruct((n, LANES), F32)],
        compiler_params=_cp("parallel"),
        name="out_proj_router",
    )(o_n, o_f, x2, gate1, scale2, shift2, gpm, gpf, wo, wr, br)


MOE_EB, MOE_TM = 8, 1024


def _moe_kernel(h_ref, gd_ref, wg_ref, wu_ref, wd_ref, o_ref, *, eb):
    blk = pl.program_id(1)

    @pl.when(blk == 0)
    def _():
        o_ref[...] = jnp.zeros_like(o_ref)

    h = h_ref[...]
    f = wg_ref.shape[2]
    gd = gd_ref[...]
    g_hi = gd.astype(BF16)
    g_lo = (gd - g_hi.astype(F32)).astype(BF16)
    expand = _onehot(_iota((LANES, eb * f), 0) == blk * eb + _iota((LANES, eb * f), 1) // f)
    gates = _dot(g_hi, expand) + _dot(g_lo, expand)
    parts = []
    for i in range(eb):
        a = _dot(h, wg_ref[i])
        parts.append((a * jax.nn.sigmoid(a) * _dot(h, wu_ref[i]) * gates[:, i * f:(i + 1) * f]).astype(BF16))
    hb = jnp.concatenate(parts, axis=1)
    o_ref[...] += _dot(hb, wd_ref[...].reshape(eb * f, wd_ref.shape[2]))


def _moe_routed(h2, gd, wg, wu, wd):
    n, d = h2.shape
    n_e, _, f = wg.shape
    tm = min(MOE_TM, n)
    eb = MOE_EB
    assert n_e % eb == 0
    return pl.pallas_call(
        functools.partial(_moe_kernel, eb=eb),
        grid=(n // tm, n_e // eb),
        in_specs=[pl.BlockSpec((tm, d), lambda i, e: (i, 0)),
                  pl.BlockSpec((tm, LANES), lambda i, e: (i, 0)),
                  pl.BlockSpec((eb, d, f), lambda i, e: (e, 0, 0)),
                  pl.BlockSpec((eb, d, f), lambda i, e: (e, 0, 0)),
                  pl.BlockSpec((eb, f, d), lambda i, e: (e, 0, 0))],
        out_specs=pl.BlockSpec((tm, d), lambda i, e: (i, 0)),
        out_shape=jax.ShapeDtypeStruct((n, d), F32),
        compiler_params=_cp("parallel", "arbitrary"),
        name="moe_routed",
    )(h2, gd, wg, wu, wd)


def _final_kernel(r_ref, h_ref, x1_ref, g2_ref, gpf_ref, wg_ref, wu_ref, wd_ref, y_ref):
    h = h_ref[...]
    a = _dot(h, wg_ref[...])
    hb = (a * jax.nn.sigmoid(a) * _dot(h, wu_ref[...])).astype(BF16)
    f = r_ref[...] + _dot(hb, wd_ref[...])
    y_ref[...] = x1_ref[...] + g2_ref[...] * _rms(f, gpf_ref[...])


def _final(routed, h2, x1, gate2, g_post_ffn, wsg, wsu, wsd, rows_per_mod):
    n, d = x1.shape
    tm = min(512, n)
    if rows_per_mod >= tm:
        per = rows_per_mod // tm
        mod_spec = pl.BlockSpec((None, 1, d), lambda i: (i // per, 0, 0))
    else:
        gate2 = jnp.repeat(gate2[:, 0, :], rows_per_mod, axis=0)
        mod_spec = pl.BlockSpec((tm, d), lambda i: (i, 0))
    row = pl.BlockSpec((tm, d), lambda i: (i, 0))
    full = lambda a: pl.BlockSpec(a.shape, lambda i: (0,) * a.ndim)
    gpf = g_post_ffn.reshape(1, d)
    return pl.pallas_call(
        _final_kernel,
        grid=(n // tm,),
        in_specs=[row, row, row, mod_spec, full(gpf), full(wsg), full(wsu), full(wsd)],
        out_specs=row,
        out_shape=jax.ShapeDtypeStruct((n, d), F32),
        compiler_params=_cp("parallel"),
        name="shared_ffn_final",
    )(routed, h2, x1, gate2, gpf, wsg, wsu, wsd)


PAGES_PER_STEP = 16


def _page_specs(block, n_pages, pp):
    def spec(jj):
        return pl.BlockSpec(block, lambda b, c, pt: (pt[b * n_pages + c * pp + jj],) + (0,) * (len(block) - 1))
    return [spec(jj) for jj in range(pp)]


def _cmp_sample_kernel(pt_ref, *refs, pp):
    pages = refs[:pp]
    w1_ref, bias_ref, w2_ref, o_ref, carry_ref, x_s = refs[pp:]
    c = pl.program_id(1)
    page = pages[0].shape[-1]
    rows = pp * page // CMP_STRIDE

    @pl.when(c == 0)
    def _():
        carry_ref[...] = jnp.zeros_like(carry_ref)

    for jj, p in enumerate(pages):
        for kv in range(2):
            x_s[kv, jj * page:(jj + 1) * page, :] = p[0, kv].reshape(NSA_G * HD, page).T

    acc = []
    for kv in range(2):
        taps = [x_s[kv, pl.ds(t, rows, stride=CMP_STRIDE), :].astype(BF16) for t in range(CMP_STRIDE)]
        acc.append(_dot(jnp.concatenate(taps, axis=1), w1_ref[kv]))
    out = [None, None]
    for kv in range(2):
        a = acc[kv][:, :2 * CMP_HID]
        b = acc[kv][:, 2 * CMP_HID:]
        a_prev = pltpu.roll(a, 1, 0)
        a_prev = jnp.where(_iota(a.shape, 0) == 0, carry_ref[kv][0:1], a_prev)
        carry_ref[kv] = jnp.broadcast_to(a[rows - 1:rows], carry_ref.shape[1:])
        hid = jax.nn.gelu(a_prev + b + bias_ref[kv]).astype(BF16)
        for g in range(NSA_G):
            term = _dot(hid[:, g * CMP_HID:(g + 1) * CMP_HID], w2_ref[kv, g])
            out[g] = term if out[g] is None else out[g] + term
    for g in range(NSA_G):
        o_ref[0, g] = out[g].astype(o_ref.dtype)


def _cmp_sample_weights(w_cmp1, b_cmp1, w_cmp2, pe_cmp):
    w1 = w_cmp1.reshape(2, 2, CMP_STRIDE, HD, CMP_HID)
    z = jnp.zeros_like(w1[:, 0])
    top = jnp.concatenate([w1[:, 0], z, w1[:, 1], z], axis=-1)
    bot = jnp.concatenate([z, w1[:, 0], z, w1[:, 1]], axis=-1)
    w1bd = jnp.concatenate([top, bot], axis=2).astype(BF16)
    w1bd = w1bd.reshape(2, CMP_STRIDE * LANES, 4 * CMP_HID)
    bias = (jnp.einsum("kf,kfh->kh", pe_cmp.reshape(2, -1).astype(BF16), w_cmp1.astype(BF16),
                       preferred_element_type=F32) + b_cmp1)
    bias = jnp.concatenate([bias, bias], axis=-1).reshape(2, 1, 2 * CMP_HID)
    z2 = jnp.zeros((CMP_HID, HD), F32)
    lo = lambda w: jnp.concatenate([w, z2], axis=1)
    hi = lambda w: jnp.concatenate([z2, w], axis=1)
    w2 = jnp.stack([jnp.stack([lo(w_cmp2[0]), hi(w_cmp2[0])]),
                    jnp.stack([hi(w_cmp2[1]), lo(w_cmp2[1])])]).astype(BF16)
    return w1bd, bias, w2


def _compress_sample(cache, pt_flat, Bs, n_pages, csw):
    w1bd, bias, w2 = csw
    pp = min(2 * PAGES_PER_STEP, n_pages)
    page = cache.shape[-1]
    rows = pp * (page // CMP_STRIDE)
    n_slots = n_pages * (page // CMP_STRIDE)
    full = lambda a: pl.BlockSpec(a.shape, lambda b, c, pt: (0,) * a.ndim)
    return pl.pallas_call(
        functools.partial(_cmp_sample_kernel, pp=pp),
        grid_spec=pltpu.PrefetchScalarGridSpec(
            num_scalar_prefetch=1, grid=(Bs, n_pages // pp),
            in_specs=_page_specs((1,) + cache.shape[1:], n_pages, pp) + [full(w1bd), full(bias), full(w2)],
            out_specs=pl.BlockSpec((1, NSA_G, rows, LANES), lambda b, c, pt: (b, 0, c, 0)),
            scratch_shapes=[pltpu.VMEM((2, 8, 2 * CMP_HID), F32), pltpu.VMEM((2, pp * page, LANES), F32)]),
        out_shape=jax.ShapeDtypeStruct((Bs, NSA_G, n_slots, LANES), BF16),
        compiler_params=_cp("parallel", "arbitrary"),
        name="compress_sample",
    )(pt_flat, *([cache] * pp), w1bd, bias, w2)


def _logf_sample_kernel(pt_ref, *refs, pp):
    pages = refs[:pp]
    o_ref, carry_ref = refs[pp:]

    @pl.when(pl.program_id(1) == 0)
    def _():
        carry_ref[...] = jnp.zeros_like(carry_ref)

    page = pages[0].shape[2]
    per = max(1, 512 // page)
    for c0 in range(0, pp, per):
        x = jnp.concatenate([p[0] for p in pages[c0:c0 + per]], axis=1)
        d = _cumsum_block(x) + carry_ref[:, 0:1]
        o_ref[0, :, c0 * page:(c0 + per) * page] = d
        carry_ref[...] = jnp.broadcast_to(d[:, d.shape[1] - 1:], carry_ref.shape)


def _cumsum_sample(logf_cache_t, pt_flat, Bs, n_pages):
    pp = min(PAGES_PER_STEP, n_pages)
    page = logf_cache_t.shape[2]
    return pl.pallas_call(
        functools.partial(_logf_sample_kernel, pp=pp),
        grid_spec=pltpu.PrefetchScalarGridSpec(
            num_scalar_prefetch=1, grid=(Bs, n_pages // pp),
            in_specs=_page_specs((1, FOX_H, page), n_pages, pp),
            out_specs=pl.BlockSpec((1, FOX_H, pp * page), lambda b, c, pt: (b, 0, c)),
            scratch_shapes=[pltpu.VMEM((FOX_H, LANES), F32)]),
        out_shape=jax.ShapeDtypeStruct((Bs, FOX_H, n_pages * page), F32),
        compiler_params=_cp("parallel", "arbitrary"),
        name="cumsum_sample",
    )(pt_flat, *([logf_cache_t] * pp))


ROWS_S = NSA_G * HG * 8


SEL_SEQS = 8


def _sel_sample_kernel(q_ref, cmp_ref, c2s_ref, slope_ref, qidx_ref, oc_ref, sel_ref, *, past, n_take):
    n_slots = cmp_ref.shape[2]
    rg = ROWS_S // NSA_G
    slot = _iota((1, n_slots), 1)
    cend = slot * CMP_STRIDE + (CMP_STRIDE - 1)
    c2s = c2s_ref[...]
    blk = _iota((8, LANES), 1)
    n_blk = past // SLC_BLOCK
    forced = (blk == 0) | (blk == n_blk - 1)
    for sq in range(q_ref.shape[0]):
        for g in range(NSA_G):
            r = slice(g * rg, (g + 1) * rg)
            cmp = cmp_ref[sq, g]
            d_c = past + qidx_ref[r] - cend
            mask = (d_c >= 0) & (slot >= 1)
            s = _dot_nt(q_ref[sq, r], cmp) - slope_ref[r] * d_c.astype(F32)
            pb = _masked_softmax(s, mask).astype(BF16)
            oc_ref[sq, r] = pltpu.roll(_dot(pb, cmp), HD, 1)
            imp_h = _dot(pb, c2s)
            imp = imp_h[0:8] + imp_h[8:16] + imp_h[16:24] + imp_h[24:32]
            val = jnp.where(blk < n_blk, imp + jnp.where(forced, FORCE, 0.0), NEG)
            sel8 = _topk_mask(val, n_take, 1)
            sel_ref[sq, r] = jnp.concatenate([sel8] * HG, axis=0).astype(sel_ref.dtype)


def _select_sample(qs, cmp_s, slope_rows, qidx_rows, Bs, past):
    n_slots = cmp_s.shape[2]
    n_slc = past // SLC_BLOCK + 1
    assert past % SLC_BLOCK == 0 and n_slc - 1 <= LANES and HG == 4
    nb = SEL_SEQS if Bs % SEL_SEQS == 0 else 1
    c2s = jnp.asarray(_cmp_to_slc_np(n_slots, -1), BF16)
    full = lambda a: pl.BlockSpec(a.shape, lambda b: (0,) * a.ndim)
    return pl.pallas_call(
        functools.partial(_sel_sample_kernel, past=past, n_take=min(SLC_TOPK, n_slc) - 1),
        grid=(Bs // nb,),
        in_specs=[pl.BlockSpec((nb, ROWS_S, LANES), lambda b: (b, 0, 0)),
                  pl.BlockSpec((nb, NSA_G, n_slots, LANES), lambda b: (b, 0, 0, 0)),
                  full(c2s), full(slope_rows), full(qidx_rows)],
        out_specs=[pl.BlockSpec((nb, ROWS_S, LANES), lambda b: (b, 0, 0)),
                   pl.BlockSpec((nb, ROWS_S, LANES), lambda b: (b, 0, 0))],
        out_shape=[jax.ShapeDtypeStruct((Bs, ROWS_S, LANES), F32),
                   jax.ShapeDtypeStruct((Bs, ROWS_S, LANES), BF16)],
        compiler_params=_cp("parallel"),
        name="select_sample",
    )(qs, cmp_s, c2s, slope_rows, qidx_rows)


def _online_update(s, ok, v, m_s, l_s, acc_s, base2=False, v_t=False):
    ex = jnp.exp2 if base2 else jnp.exp
    s = jnp.where(ok, s, NEG)
    m_old = m_s[...]
    m_new = jnp.maximum(m_old, jnp.max(s, axis=1, keepdims=True))
    alpha = ex(m_old - m_new)
    e = jnp.where(ok, ex(s - m_new), 0.0)
    l_s[...] = alpha * l_s[...] + jnp.sum(e, axis=1, keepdims=True)
    acc_s[...] = alpha * acc_s[...] + (_dot_nt if v_t else _dot)(e.astype(BF16), v)
    m_s[...] = m_new


def _pages_t(pages, kv, width):
    return jnp.concatenate([p[0, kv].reshape(width, p.shape[-1]).astype(BF16) for p in pages], axis=1)


def _pad_rows8(x):
    return jnp.concatenate([x, jnp.zeros((8 - x.shape[0], x.shape[1]), x.dtype)], axis=0)


def _slc_sample_kernel(pt_ref, *refs, pp, past, t_new):
    pages = refs[:pp]
    q_ref, sel_ref, new_ref, slope_ref, qidx_ref, o_ref, m_s, l_s, acc_s = refs[pp:]
    c = pl.program_id(1)
    page = pages[0].shape[-1]
    nk = pp * page

    @pl.when(c == 0)
    def _():
        m_s[...] = jnp.full(m_s.shape, NEG, F32)
        l_s[...] = jnp.zeros(l_s.shape, F32)
        acc_s[...] = jnp.zeros(acc_s.shape, F32)

    q = q_ref[0]
    slope = slope_ref[...]
    qidx = qidx_ref[...]
    kpos = c * nk + _iota((1, nk), 1)
    expand = _onehot(_iota((LANES, 1), 0) == (kpos >> 6))
    ok = _dot(sel_ref[0], expand) > 0.5

    @pl.when(jnp.max(jnp.where(ok, 1.0, 0.0)) > 0.0)
    def _():
        dist = (past + qidx - kpos).astype(F32)
        s = _dot(q, _pages_t(pages, 0, NSA_G * HD)) - slope * dist
        _online_update(s, ok, _pages_t(pages, 1, NSA_G * HD), m_s, l_s, acc_s, v_t=True)

    @pl.when(c == pl.num_programs(1) - 1)
    def _():
        new = _pad_rows8(new_ref[0])
        kj = _iota((1, 8), 1)
        ok_new = (kj <= qidx) & (kj < t_new)
        s_new = _dot_nt(q, new[:, :LANES].astype(BF16)) - slope * (qidx - kj).astype(F32)
        _online_update(s_new, ok_new, new[:, LANES:].astype(BF16), m_s, l_s, acc_s)
        o_ref[0] = acc_s[...] / l_s[...]


def _slc_sample(cache, pt_flat, qs, sel_rows, kvs_new, slope_rows, qidx_rows, Bs, n_pages, past):
    pp = min(PAGES_PER_STEP, n_pages)
    t_new = kvs_new.shape[1]
    full = lambda a: pl.BlockSpec(a.shape, lambda b, c, pt: (0,) * a.ndim)
    per_seq = lambda a: pl.BlockSpec((1,) + a.shape[1:], lambda b, c, pt: (b,) + (0,) * (a.ndim - 1))
    return pl.pallas_call(
        functools.partial(_slc_sample_kernel, pp=pp, past=past, t_new=t_new),
        grid_spec=pltpu.PrefetchScalarGridSpec(
            num_scalar_prefetch=1, grid=(Bs, n_pages // pp),
            in_specs=_page_specs((1,) + cache.shape[1:], n_pages, pp)
            + [per_seq(qs), per_seq(sel_rows), per_seq(kvs_new), full(slope_rows), full(qidx_rows)],
            out_specs=pl.BlockSpec((1, ROWS_S, LANES), lambda b, c, pt: (b, 0, 0)),
            scratch_shapes=[pltpu.VMEM((ROWS_S, 1), F32), pltpu.VMEM((ROWS_S, 1), F32),
                            pltpu.VMEM((ROWS_S, LANES), F32)]),
        out_shape=jax.ShapeDtypeStruct((Bs, ROWS_S, LANES), F32),
        compiler_params=_cp("parallel", "arbitrary"),
        name="selected_sample",
    )(pt_flat, *([cache] * pp), qs, sel_rows, kvs_new, slope_rows, qidx_rows)


def _win_sample_kernel(q_ref, win_ref, new_ref, oc_ref, os_ref, gates_ref, slope_ref, qidx_ref, o_ref, *, t_new):
    q = q_ref[0]
    n_win = win_ref.shape[-1]
    k_t = win_ref[0, 0].reshape(NSA_G * HD, n_win).astype(BF16)
    v_t = win_ref[0, 1].reshape(NSA_G * HD, n_win).astype(BF16)
    new = _pad_rows8(new_ref[0])
    kj = _iota((1, n_win + 8), 1)
    d_w = n_win + qidx_ref[...] - kj
    mask = (d_w >= 0) & (d_w < WINDOW) & (kj < n_win + t_new)
    s = jnp.concatenate([_dot(q, k_t), _dot_nt(q, new[:, :LANES].astype(BF16))], axis=1)
    p = _masked_softmax(s - slope_ref[...] * d_w.astype(F32), mask).astype(BF16)
    o_w = _dot_nt(p[:, :n_win], v_t) + _dot(p[:, n_win:], new[:, LANES:].astype(BF16))
    gates = gates_ref[0]
    o_ref[0] = gates[:, 0:1] * oc_ref[0] + gates[:, 1:2] * os_ref[0] + gates[:, 2:3] * o_w


def _win_sample(qs, state_win, kvw_new, oc, o_s, gate_rows, slope_rows, qidx_rows, Bs):
    t_new = kvw_new.shape[1]
    full = lambda a: pl.BlockSpec(a.shape, lambda b: (0,) * a.ndim)
    per_seq = lambda a: pl.BlockSpec((1,) + a.shape[1:], lambda b: (b,) + (0,) * (a.ndim - 1))
    return pl.pallas_call(
        functools.partial(_win_sample_kernel, t_new=t_new),
        grid=(Bs,),
        in_specs=[per_seq(qs), per_seq(state_win), per_seq(kvw_new), per_seq(oc), per_seq(o_s),
                  per_seq(gate_rows), full(slope_rows), full(qidx_rows)],
        out_specs=pl.BlockSpec((1, ROWS_S, LANES), lambda b: (b, 0, 0)),
        out_shape=jax.ShapeDtypeStruct((Bs, ROWS_S, LANES), F32),
        compiler_params=_cp("parallel"),
        name="window_sample",
    )(qs, state_win, kvw_new, oc, o_s, gate_rows, slope_rows, qidx_rows)


ROWS_F = FOX_H * 8
FOX_K = FOX_H * HD


def _fox_sample_kernel(pt_ref, *refs, pp, t_new):
    pages = refs[:pp]
    q_ref, d_ref, new_ref, dnew_ref, qidx_ref, o_ref, m_s, l_s, acc_s = refs[pp:]
    c = pl.program_id(1)

    @pl.when(c == 0)
    def _():
        m_s[...] = jnp.full(m_s.shape, NEG, F32)
        l_s[...] = jnp.zeros(l_s.shape, F32)
        acc_s[...] = jnp.zeros(acc_s.shape, F32)

    q = q_ref[0]
    d = d_ref[0]
    nk = d.shape[1]
    bias = jnp.concatenate([jnp.broadcast_to(d[h:h + 1], (8, nk)) for h in range(FOX_H)], axis=0)
    s = _dot(q, _pages_t(pages, 0, FOX_K)) - bias * LOG2E
    _online_update(s, jnp.full((ROWS_F, nk), True), _pages_t(pages, 1, FOX_K), m_s, l_s, acc_s,
                   base2=True, v_t=True)

    @pl.when(c == pl.num_programs(1) - 1)
    def _():
        new = _pad_rows8(new_ref[0])
        kj = _iota((1, 8), 1)
        ok_new = (kj <= qidx_ref[...]) & (kj < t_new)
        s_new = _dot_nt(q, new[:, :FOX_K].astype(BF16)) - dnew_ref[0] * LOG2E
        _online_update(s_new, ok_new, new[:, FOX_K:].astype(BF16), m_s, l_s, acc_s, base2=True)
        o_ref[0] = acc_s[...] / l_s[...]


def _fox_sample(cache, pt_flat, qf_bd, d_past_t, kvf_new, dnew_rows, qidx_f, Bs, n_pages):
    pp = min(PAGES_PER_STEP, n_pages)
    page = cache.shape[-1]
    t_new = kvf_new.shape[1]
    full = lambda a: pl.BlockSpec(a.shape, lambda b, c, pt: (0,) * a.ndim)
    per_seq = lambda a: pl.BlockSpec((1,) + a.shape[1:], lambda b, c, pt: (b,) + (0,) * (a.ndim - 1))
    return pl.pallas_call(
        functools.partial(_fox_sample_kernel, pp=pp, t_new=t_new),
        grid_spec=pltpu.PrefetchScalarGridSpec(
            num_scalar_prefetch=1, grid=(Bs, n_pages // pp),
            in_specs=_page_specs((1,) + cache.shape[1:], n_pages, pp)
            + [per_seq(qf_bd), pl.BlockSpec((1, FOX_H, pp * page), lambda b, c, pt: (b, 0, c)),
               per_seq(kvf_new), per_seq(dnew_rows), full(qidx_f)],
            out_specs=pl.BlockSpec((1, ROWS_F, FOX_K), lambda b, c, pt: (b, 0, 0)),
            scratch_shapes=[pltpu.VMEM((ROWS_F, 1), F32), pltpu.VMEM((ROWS_F, 1), F32),
                            pltpu.VMEM((ROWS_F, FOX_K), F32)]),
        out_shape=jax.ShapeDtypeStruct((Bs, ROWS_F, FOX_K), F32),
        compiler_params=_cp("parallel", "arbitrary"),
        name="fox_sample",
    )(pt_flat, *([cache] * pp), qf_bd, d_past_t, kvf_new, dnew_rows, qidx_f)


def _alibi_slopes():
    return jnp.exp2(-8.0 * jnp.arange(1, NSA_H + 1, dtype=F32) / NSA_H)


def _ffn_tail(o_n, o_f, x2, mod, norms, lw, rows_per_mod):
    gate1, scale2, shift2, gate2 = mod
    g_post_mix, g_pre_ffn, g_post_ffn = norms
    x1, h2, gd = _out_proj(o_n, o_f, x2, gate1, scale2, shift2, g_post_mix, g_pre_ffn,
                           lw["wo"], lw["wr"], lw["br"], rows_per_mod)
    routed = _moe_routed(h2, gd, lw["weg"], lw["weu"], lw["wed"])
    return _final(routed, h2, x1, gate2, g_post_ffn, lw["wsg"], lw["wsu"], lw["wsd"], rows_per_mod)


def _prompt_layer(x, mod, norms, lw):
    B, T, d = x.shape
    n = B * T
    shift1, scale1, gate1, shift2, scale2, gate2 = mod
    g_pre_mix, g_post_mix, g_pre_ffn, g_post_ffn = norms
    x2 = x.reshape(n, d)
    pr = _in_proj(x2, scale1, shift1, g_pre_mix, lw["w_in"], lw["b_small"], T, "prompt")
    small = pr["small"]
    logf = small[:, N_GATE:N_GATE + FOX_H]
    gates_g = small[:, :N_GATE].reshape(n, NSA_G, HG * 3).transpose(1, 0, 2)
    gates_g = jnp.pad(gates_g, ((0, 0), (0, 0), (0, LANES - HG * 3)))
    slopes = _alibi_slopes()

    cmp = _compress_prompt(pr["kvc"], B, T, lw["cmp_w"])
    oc, selneg = _nsa_select(slopes, pr["qn"], cmp, B, T)
    o_n = _nsa_attend(slopes, pr["qn"], pr, selneg, oc, gates_g, B, T)

    d_t = _cumsum_prompt(logf.reshape(B, T, FOX_H).transpose(0, 2, 1))
    o_f = _fox_attend(pr["qf"], pr["kf_op"], pr["vf_op"], d_t, B, T)

    y = _ffn_tail(o_n, o_f, x2, (gate1, scale2, shift2, gate2), (g_post_mix, g_pre_ffn, g_post_ffn), lw, T)
    win_rows = min(WINDOW, T)
    state = lambda a, heads: jnp.moveaxis(a.reshape(B, 2, heads, HD, -1), -1, 1)
    states = (state(pr["kvc_t"], NSA_G), state(pr["kvs_t"], NSA_G), state(pr["kvf_t"], FOX_H),
              logf.reshape(B, T, FOX_H), state(pr["kvw_t"][:, :, T - win_rows:], NSA_G))
    return y.reshape(B, T, d), states


def _sample_layer(x, mod, norms, lw, caches):
    Bs, Tn, d = x.shape
    n = Bs * Tn
    assert Tn <= 8
    cache_cmp, cache_slc, cache_fox, cache_logf, state_win, page_table = caches
    n_pages = page_table.shape[1]
    page = cache_cmp.shape[1]
    past = n_pages * page
    shift1, scale1, gate1, shift2, scale2, gate2 = mod
    g_pre_mix, g_post_mix, g_pre_ffn, g_post_ffn = norms
    x2 = x.reshape(n, d)
    pr = _in_proj(x2, scale1, shift1, g_pre_mix, lw["w_in"], lw["b_small"], Tn, "sample")
    small = pr["small"]
    logf = small[:, N_GATE:N_GATE + FOX_H].reshape(Bs, Tn, FOX_H)
    pt_flat = page_table.reshape(-1)
    padq = ((0, 0), (0, 0), (0, 0), (0, 8 - Tn), (0, 0))

    qn = pr["qn"].reshape(Bs, Tn, NSA_G, HG, HD).transpose(0, 2, 3, 1, 4)
    qn = jnp.pad(qn, padq)
    z = jnp.zeros_like(qn[:, 0])
    qs = jnp.stack([jnp.concatenate([qn[:, 0], z], -1), jnp.concatenate([z, qn[:, 1]], -1)], axis=1)
    qs = qs.reshape(Bs, ROWS_S, LANES)
    slope_rows = jnp.broadcast_to(_alibi_slopes().reshape(NSA_H, 1, 1), (NSA_H, 8, 1)).reshape(ROWS_S, 1)
    qidx_rows = jnp.broadcast_to(jnp.arange(8, dtype=jnp.int32).reshape(1, 8, 1), (NSA_H, 8, 1)).reshape(ROWS_S, 1)
    gate_rows = small[:, :N_GATE].reshape(Bs, Tn, NSA_H, 3).transpose(0, 2, 1, 3)
    gate_rows = jnp.pad(gate_rows, ((0, 0), (0, 0), (0, 8 - Tn), (0, LANES - 3))).reshape(Bs, ROWS_S, LANES)

    tok_minor = lambda a: jnp.moveaxis(a, 1, -1)
    cmp_s = _compress_sample(tok_minor(cache_cmp), pt_flat, Bs, n_pages, lw["cmp_sw"])
    oc, sel_rows = _select_sample(qs, cmp_s, slope_rows, qidx_rows, Bs, past)
    o_s = _slc_sample(tok_minor(cache_slc), pt_flat, qs, sel_rows, pr["kvs"].reshape(Bs, Tn, KVW),
                      slope_rows, qidx_rows, Bs, n_pages, past)
    win = tok_minor(state_win)
    o_rows = _win_sample(qs, win, pr["kvw"].reshape(Bs, Tn, KVW), oc, o_s, gate_rows, slope_rows, qidx_rows, Bs)
    o_rows = o_rows.reshape(Bs, NSA_G, HG, 8, NSA_G, HD)[:, :, :, :Tn]
    o_n = jnp.stack([o_rows[:, 0, :, :, 0], o_rows[:, 1, :, :, 1]], axis=1)
    o_n = o_n.transpose(0, 3, 1, 2, 4).reshape(n, QW).astype(BF16)

    qf = jnp.pad(pr["qf"].reshape(Bs, Tn, FOX_H, HD).transpose(0, 2, 1, 3), ((0, 0), (0, 0), (0, 8 - Tn), (0, 0)))
    qf_bd = (qf[:, :, :, None, :] * jnp.eye(FOX_H, dtype=BF16)[None, :, None, :, None]).reshape(Bs, ROWS_F, FOX_K)
    d_past_t = _cumsum_sample(cache_logf.transpose(0, 2, 1), pt_flat, Bs, n_pages)
    d_new = d_past_t[:, :, past - 1:] + jnp.cumsum(logf, axis=1).transpose(0, 2, 1)
    dnew_rows = jnp.broadcast_to(jnp.pad(d_new, ((0, 0), (0, 0), (0, 8 - Tn)))[:, :, None, :],
                                 (Bs, FOX_H, 8, 8)).reshape(Bs, ROWS_F, 8)
    qidx_f = jnp.broadcast_to(jnp.arange(8, dtype=jnp.int32).reshape(1, 8, 1), (FOX_H, 8, 1)).reshape(ROWS_F, 1)
    o_frows = _fox_sample(tok_minor(cache_fox), pt_flat, qf_bd, d_past_t,
                          pr["kvf"].reshape(Bs, Tn, KVF), dnew_rows, qidx_f, Bs, n_pages)
    o_frows = o_frows.reshape(Bs, FOX_H, 8, FOX_H, HD)[:, :, :Tn]
    o_f = jnp.stack([o_frows[:, h, :, h] for h in range(FOX_H)], axis=2)
    o_f = o_f.reshape(n, FOX_H * HD).astype(BF16)

    y = _ffn_tail(o_n, o_f, x2, (gate1, scale2, shift2, gate2), (g_post_mix, g_pre_ffn, g_post_ffn), lw, Tn)
    kvw_new = pr["kvw"].reshape(Bs, Tn, 2, NSA_G, HD)
    states = (pr["kvc"].reshape(Bs, Tn, 2, NSA_G, HD), pr["kvs"].reshape(Bs, Tn, 2, NSA_G, HD),
              pr["kvf"].reshape(Bs, Tn, 2, FOX_H, HD), logf,
              jnp.concatenate([state_win, kvw_new], axis=1)[:, Tn:])
    return y.reshape(Bs, Tn, d), states


def _layer_weights(w_in, b_nsa_gate, b_forget, w_cmp1, b_cmp1, w_cmp2, pe_cmp, w_out, w_router, b_router,
                   w_exp_gate, w_exp_up, w_exp_down, w_sh_gate, w_sh_up, w_sh_down):
    w_in_b, b_small = _in_weights(w_in, b_nsa_gate, b_forget)
    n_e = w_router.shape[1]
    wr = jnp.pad(w_router.T, ((0, LANES - n_e), (0, 0))).astype(BF16)
    br = jnp.concatenate([b_router.astype(F32), jnp.full((LANES - n_e,), NEG_INF, F32)]).reshape(LANES, 1)
    return dict(w_in=w_in_b, b_small=b_small,
                cmp_w=_cmp_weights(w_cmp1, b_cmp1, w_cmp2, pe_cmp),
                cmp_sw=_cmp_sample_weights(w_cmp1, b_cmp1, w_cmp2, pe_cmp),
                wo=w_out.astype(BF16), wr=wr, br=br,
                weg=w_exp_gate.astype(BF16), weu=w_exp_up.astype(BF16), wed=w_exp_down.astype(BF16),
                wsg=w_sh_gate.astype(BF16), wsu=w_sh_up.astype(BF16), wsd=w_sh_down.astype(BF16))


def kernel(x_prompt, x_sample, c_prompt, c_sample, cache_cmp_kv, cache_slc_kv, cache_fox_kv, cache_fox_logf,
           state_win_kv, page_table, w_ada, b_ada, g_pre_mix, g_post_mix, g_pre_ffn, g_post_ffn, w_in,
           b_nsa_gate, b_forget, w_cmp1, b_cmp1, w_cmp2, pe_cmp, w_out, w_router, b_router, w_exp_gate,
           w_exp_up, w_exp_down, w_sh_gate, w_sh_up, w_sh_down):
    depth = w_in.shape[0]
    Bp = x_prompt.shape[0]
    xp, xs = x_prompt, x_sample
    st_p = [[] for _ in range(5)]
    st_s = [[] for _ in range(5)]
    for l in range(depth):
        lw = _layer_weights(w_in[l], b_nsa_gate[l], b_forget[l], w_cmp1[l], b_cmp1[l], w_cmp2[l], pe_cmp[l],
                            w_out[l], w_router[l], b_router[l], w_exp_gate[l], w_exp_up[l], w_exp_down[l],
                            w_sh_gate[l], w_sh_up[l], w_sh_down[l])
        m = _modulation(jnp.concatenate([c_prompt, c_sample], axis=0), w_ada[l], b_ada[l])
        mods = jnp.split(m[:, None, :], 6, axis=-1)
        norms = (g_pre_mix[l], g_post_mix[l], g_pre_ffn[l], g_post_ffn[l])
        xp, sp = _prompt_layer(xp, [a[:Bp] for a in mods], norms, lw)
        caches = (cache_cmp_kv[l], cache_slc_kv[l], cache_fox_kv[l], cache_fox_logf[l], state_win_kv[l], page_table)
        xs, ss = _sample_layer(xs, [a[Bp:] for a in mods], norms, lw, caches)
        for lst, a in zip(st_p, sp):
            lst.append(a)
        for lst, a in zip(st_s, ss):
            lst.append(a)
    return (xp, xs, *[jnp.stack(s) for s in st_p], *[jnp.stack(s) for s in st_s])
```

```python
import functools

import numpy as np
import jax
import jax.numpy as jnp
from jax import lax
from jax.experimental import pallas as pl
from jax.experimental.pallas import tpu as pltpu

F32 = jnp.float32
BF16 = jnp.bfloat16

HD = 64
NSA_H = 8
NSA_G = 2
HG = NSA_H // NSA_G
FOX_H = 8
CMP_STRIDE = 16
CMP_LEN = 32
CMP_HID = 128
SLC_BLOCK = 64
SLC_TOPK = 16
WINDOW = 512
TOP_K = 8
ROUTED_SCALE = 2.5
RMS_EPS = 1e-6
NEG = -1e30
TINY = 1e-30
FORCE = 1e4
NEG_INF = float("-inf")

LANES = 128
KVW = 2 * NSA_G * HD
KVF = 2 * FOX_H * HD
QW = NSA_H * HD
VMEM_LIMIT = 56 * 1024 * 1024


def _cp(*sem):
    return pltpu.CompilerParams(dimension_semantics=sem, vmem_limit_bytes=VMEM_LIMIT)


def _rms(x, g):
    return x * lax.rsqrt(jnp.mean(x * x, axis=-1, keepdims=True) + RMS_EPS) * g


def _dot(a, b):
    return jnp.dot(a, b, preferred_element_type=F32)


def _dot_nt(a, b):
    return lax.dot_general(a, b, (((1,), (1,)), ((), ())), preferred_element_type=F32)


def _iota(shape, dim):
    return lax.broadcasted_iota(jnp.int32, shape, dim)


def _onehot(cond):
    return jnp.where(cond, 1.0, 0.0).astype(BF16)


def _pick_head(width_in, h_off):
    r = _iota((width_in, LANES), 0)
    c = _iota((width_in, LANES), 1)
    return _onehot((r == c + h_off) & (c < HD))


def _place_head(width_out, off):
    r = _iota((LANES, width_out), 0)
    c = _iota((LANES, width_out), 1)
    return _onehot((r >= HD) & (c == r - HD + off))


def _split3(x):
    hi = x.astype(BF16)
    r1 = x - hi.astype(F32)
    mid = r1.astype(BF16)
    lo = (r1 - mid.astype(F32)).astype(BF16)
    return hi, mid, lo


def _mod_kernel(c_ref, w_ref, b_ref, o_ref):
    c = c_ref[...]
    a = (c * jax.nn.sigmoid(c)).astype(BF16)
    o_ref[...] = _dot(a, w_ref[...].astype(BF16)) + b_ref[...]


def _modulation(c_all, w_ada, b_ada):
    nb, d = c_all.shape
    n_chunk = w_ada.shape[1] // d
    return pl.pallas_call(
        _mod_kernel,
        grid=(n_chunk,),
        in_specs=[pl.BlockSpec((nb, d), lambda i: (0, 0)),
                  pl.BlockSpec((d, d), lambda i: (0, i)),
                  pl.BlockSpec((1, d), lambda i: (0, i))],
        out_specs=pl.BlockSpec((nb, d), lambda i: (0, i)),
        out_shape=jax.ShapeDtypeStruct((nb, w_ada.shape[1]), F32),
        compiler_params=_cp("parallel"),
        name="modulation",
    )(c_all, w_ada, b_ada.reshape(1, -1))


N_GATE = 3 * NSA_H
SEL_TK = 256
LOG2E = 1.4426950408889634
_IN_BASE = (("kvc", KVW), ("kvs", KVW), ("kvw", KVW), ("kvf", KVF), ("small", LANES), ("qn", QW), ("qf", QW))
_IN_OUTS = {
    "prompt": (("kvc", "f32", KVW), ("small", "f32", LANES), ("qn", "bf16", QW), ("qf", "bf16", QW),
               ("ks_op", "bf16", NSA_G * LANES), ("vs_op", "bf16", NSA_G * LANES),
               ("kw_op", "bf16", NSA_G * LANES), ("vw_op", "bf16", NSA_G * LANES),
               ("kf_op", "bf16", FOX_H * LANES), ("vf_op", "bf16", FOX_H * LANES),
               ("kvc_t", "f32_t", KVW), ("kvs_t", "f32_t", KVW), ("kvw_t", "f32_t", KVW), ("kvf_t", "f32_t", KVF)),
    "sample": (("kvc", "f32", KVW), ("kvs", "f32", KVW), ("kvw", "f32", KVW), ("kvf", "f32", KVF),
               ("small", "f32", LANES), ("qn", "bf16", QW), ("qf", "bf16", QW)),
}


def _k_operand(kk, fill):
    lo = _iota(kk.shape, 1) < HD
    return jnp.concatenate([jnp.where(lo, kk, fill), jnp.where(lo, pltpu.roll(kk, HD, 1), fill)], axis=1)


def _v_operand(vv):
    lo = _iota(vv.shape, 1) < HD
    return jnp.concatenate([jnp.where(lo, 1.0, pltpu.roll(vv, HD, 1)), jnp.where(lo, 1.0, vv)], axis=1)


def _in_kernel(x_ref, sc_ref, sh_ref, g_ref, w_ref, bias_ref, *out_refs, outs, seq_len):
    x = x_ref[...]
    tm = x.shape[0]
    h = _rms(x, g_ref[...]) * (1.0 + sc_ref[...]) + sh_ref[...]
    hb = h.astype(BF16)
    base, off = {}, 0
    for name, width in _IN_BASE:
        base[name] = _dot(hb, w_ref[:, off:off + width])
        off += width
    z = base["small"] + bias_ref[...]
    logsig = jnp.minimum(z, 0.0) - jnp.log1p(jnp.exp(-jnp.abs(z)))
    base["small"] = jnp.where(_iota(z.shape, 1) < N_GATE, jax.nn.sigmoid(z), logsig)
    lane = _iota((tm, LANES), 1)
    nk, fk = NSA_G * HD, FOX_H * HD
    for (name, kind, _), o_ref in zip(outs, out_refs):
        if kind == "f32_t":
            o_ref[0] = base[name[:-2]].T
            continue
        if name == "ks_op":
            t = (pl.program_id(0) * tm) % seq_len + _iota((tm, LANES), 0)
            pos = jnp.where(lane == HD, (t & (SEL_TK - 1)).astype(F32), 0.0)
            val = _k_operand(base["kvs"][:, :nk], pos)
        elif name == "kw_op":
            val = _k_operand(base["kvw"][:, :nk], 0.0)
        elif name == "vs_op":
            val = _v_operand(base["kvs"][:, nk:])
        elif name == "vw_op":
            val = _v_operand(base["kvw"][:, nk:])
        elif name == "kf_op":
            kf = base["kvf"][:, :fk]
            val = jnp.concatenate(
                [jnp.where((lane < HD) == (i % 2 == 0), kf[:, (i // 2) * LANES:(i // 2 + 1) * LANES], 0.0)
                 for i in range(FOX_H)], axis=1)
        elif name == "vf_op":
            vf = base["kvf"][:, fk:]
            val = jnp.concatenate([_v_operand(vf[:, j * LANES:(j + 1) * LANES]) for j in range(FOX_H // 2)], axis=1)
        else:
            val = base[name]
        o_ref[...] = val.astype(o_ref.dtype)


def _in_weights(w_in, b_nsa_gate, b_forget):
    d = w_in.shape[0]
    o_qn, o_kvc, o_kvs, o_kvw = 0, QW, QW + KVW, QW + 2 * KVW
    o_gn = QW + 3 * KVW
    o_qf = o_gn + N_GATE
    o_kvf = o_qf + QW
    o_ff = o_kvf + KVF
    scale = HD ** -0.5
    cols = dict(
        kvc=w_in[:, o_kvc:o_kvc + KVW], kvs=w_in[:, o_kvs:o_kvs + KVW], kvw=w_in[:, o_kvw:o_kvw + KVW],
        kvf=w_in[:, o_kvf:o_kvf + KVF],
        small=jnp.concatenate([w_in[:, o_gn:o_gn + N_GATE], w_in[:, o_ff:o_ff + FOX_H],
                               jnp.zeros((d, LANES - N_GATE - FOX_H), F32)], axis=1),
        qn=w_in[:, o_qn:o_qn + QW] * scale, qf=w_in[:, o_qf:o_qf + QW] * (scale * LOG2E))
    w = jnp.concatenate([cols[name] for name, _ in _IN_BASE], axis=1).astype(BF16)
    bias = jnp.concatenate([b_nsa_gate, b_forget, jnp.zeros((LANES - N_GATE - FOX_H,), F32)]).reshape(1, LANES)
    return w, bias


def _in_proj(x2, scale, shift, g_pre, w, bias, rows_per_mod, group):
    n, d = x2.shape
    tm = min(256, n)
    outs = _IN_OUTS[group]
    if rows_per_mod >= tm:
        per = rows_per_mod // tm
        mod_spec = pl.BlockSpec((None, 1, d), lambda i: (i // per, 0, 0))
    else:
        assert not any(kind == "f32_t" for _, kind, _ in outs)
        per = 1
        scale = jnp.repeat(scale[:, 0, :], rows_per_mod, axis=0)
        shift = jnp.repeat(shift[:, 0, :], rows_per_mod, axis=0)
        mod_spec = pl.BlockSpec((tm, d), lambda i: (i, 0))
    n_seq = n // rows_per_mod
    out_specs, out_shape = [], []
    for _, kind, width in outs:
        if kind == "f32_t":
            out_specs.append(pl.BlockSpec((1, width, tm), lambda i: (i // per, 0, i % per)))
            out_shape.append(jax.ShapeDtypeStruct((n_seq, width, rows_per_mod), F32))
        else:
            out_specs.append(pl.BlockSpec((tm, width), lambda i: (i, 0)))
            out_shape.append(jax.ShapeDtypeStruct((n, width), F32 if kind == "f32" else BF16))
    full = lambda a: pl.BlockSpec(a.shape, lambda i: (0,) * a.ndim)
    res = pl.pallas_call(
        functools.partial(_in_kernel, outs=outs, seq_len=rows_per_mod),
        grid=(n // tm,),
        in_specs=[pl.BlockSpec((tm, d), lambda i: (i, 0)), mod_spec, mod_spec,
                  pl.BlockSpec((1, d), lambda i: (0, 0)), full(w), full(bias)],
        out_specs=out_specs,
        out_shape=out_shape,
        compiler_params=_cp("parallel"),
        name="in_proj_" + group,
    )(x2, scale, shift, g_pre.reshape(1, d), w, bias)
    return dict(zip([name for name, _, _ in outs], res))


def _cmp_kernel(subk_ref, subv_ref, w1_ref, pe_ref, b1_ref, w2_ref, o_ref):
    half = CMP_STRIDE * HD
    out = None
    for kv, sub_ref in enumerate((subk_ref, subv_ref)):
        sub = sub_ref[0, 0]
        w1 = w1_ref[kv]
        a = _dot(sub, w1[:half])
        b = _dot(sub, w1[half:])
        n_sub = a.shape[0]
        b_next = pltpu.roll(b, n_sub - 1, 0)
        bias = _dot(pe_ref[kv], w1)[0:1] + b1_ref[kv]
        hid = jax.nn.gelu(a + b_next + bias).astype(BF16)
        term = _dot(hid, w2_ref[kv])
        out = term if out is None else out + term
    o_ref[0, 0] = out.astype(o_ref.dtype)


def _cmp_weights(w_cmp1, b_cmp1, w_cmp2, pe_cmp):
    w1 = w_cmp1.astype(BF16)
    pe = jnp.broadcast_to(pe_cmp.reshape(2, 1, CMP_LEN * HD), (2, 8, CMP_LEN * HD)).astype(BF16)
    b1 = b_cmp1.reshape(2, 1, CMP_HID)
    z = jnp.zeros((CMP_HID, HD), F32)
    w2 = jnp.stack([jnp.concatenate([w_cmp2[0], z], axis=1),
                    jnp.concatenate([z, w_cmp2[1]], axis=1)]).astype(BF16)
    return w1, pe, b1, w2


def _compress_prompt(kvc, B, T, cw):
    w1, pe, b1, w2 = cw
    n_sub = T // CMP_STRIDE
    sub = kvc.reshape(B, n_sub, CMP_STRIDE, 2 * NSA_G, HD).transpose(0, 3, 1, 2, 4)
    sub = sub.reshape(B, 2 * NSA_G, n_sub, CMP_STRIDE * HD).astype(BF16)
    feat = CMP_STRIDE * HD
    return pl.pallas_call(
        _cmp_kernel,
        grid=(B, NSA_G),
        in_specs=[pl.BlockSpec((1, 1, n_sub, feat), lambda b, g: (b, g, 0, 0)),
                  pl.BlockSpec((1, 1, n_sub, feat), lambda b, g: (b, NSA_G + g, 0, 0)),
                  pl.BlockSpec(w1.shape, lambda b, g: (0, 0, 0)),
                  pl.BlockSpec(pe.shape, lambda b, g: (0, 0, 0)),
                  pl.BlockSpec(b1.shape, lambda b, g: (0, 0, 0)),
                  pl.BlockSpec(w2.shape, lambda b, g: (0, 0, 0))],
        out_specs=pl.BlockSpec((1, 1, n_sub, LANES), lambda b, g: (b, g, 0, 0)),
        out_shape=jax.ShapeDtypeStruct((B, NSA_G, n_sub, LANES), BF16),
        compiler_params=_cp("parallel", "parallel"),
        name="compress_prompt",
    )(sub, sub, w1, pe, b1, w2)


def _cumsum_block(x):
    n = x.shape[1]
    u = _onehot(_iota((n, n), 0) <= _iota((n, n), 1))
    r = x.shape[0]
    d3 = _dot(jnp.concatenate(_split3(x), axis=0), u)
    return d3[:r] + d3[r:2 * r] + d3[2 * r:]


def _cumsum_kernel(x_ref, o_ref, carry_ref):
    @pl.when(pl.program_id(1) == 0)
    def _():
        carry_ref[...] = jnp.zeros_like(carry_ref)

    d = _cumsum_block(x_ref[0]) + carry_ref[:, 0:1]
    o_ref[0] = d
    carry_ref[...] = jnp.broadcast_to(d[:, d.shape[1] - 1:], carry_ref.shape)


def _cumsum_prompt(logf_t):
    B, H, T = logf_t.shape
    tc = min(512, T)
    return pl.pallas_call(
        _cumsum_kernel,
        grid=(B, T // tc),
        in_specs=[pl.BlockSpec((1, H, tc), lambda b, c: (b, 0, c))],
        out_specs=pl.BlockSpec((1, H, tc), lambda b, c: (b, 0, c)),
        out_shape=jax.ShapeDtypeStruct((B, H, T), F32),
        scratch_shapes=[pltpu.VMEM((H, LANES), F32)],
        compiler_params=_cp("parallel", "arbitrary"),
        name="cumsum_prompt",
    )(logf_t)


MASK_BIG = 2.0 ** 30


def _topk_mask(val, n_take, axis):
    idx_f = _iota(val.shape, axis).astype(F32)
    sel = jnp.zeros(val.shape, F32)
    for _ in range(n_take):
        m = jnp.max(val, axis=axis, keepdims=True)
        first = jnp.min(jnp.where(val == m, idx_f, float(val.shape[axis])), axis=axis, keepdims=True)
        pick = idx_f == first
        sel = jnp.where(pick, 1.0, sel)
        val = jnp.where(pick, NEG_INF, val)
    return sel


def _masked_softmax(s, mask, axis=1):
    s = jnp.where(mask, s, NEG)
    m = jnp.max(s, axis=axis, keepdims=True)
    e = jnp.where(mask, jnp.exp(s - m), 0.0)
    return e / jnp.maximum(jnp.sum(e, axis=axis, keepdims=True), TINY)


def _flash_step(s, v, m_ref, acc_ref, base2):
    nk = s.shape[1]
    ex = jnp.exp2 if base2 else jnp.exp
    m_old = m_ref[...]
    m_new = jnp.maximum(m_old, jnp.max(s, axis=1, keepdims=True))
    alpha = ex(m_old - m_new)
    e = jnp.concatenate([ex(s[:, c * LANES:(c + 1) * LANES] - m_new) for c in range(nk // LANES)], axis=1)
    acc_ref[...] = alpha * acc_ref[...] + _dot(e.astype(BF16), v)
    m_ref[...] = m_new


def _hi_over_lo(acc):
    return acc / jnp.maximum(pltpu.roll(acc, HD, 1), TINY)


def _pair_heads(o_even, o_odd):
    lane = _iota(o_even.shape, 1)
    return jnp.where(lane < HD, pltpu.roll(o_even, HD, 1), o_odd)


def _cmp_to_slc_np(n_slots, first_token):
    tok = np.arange(n_slots)[:, None] + first_token
    start = tok * CMP_STRIDE
    bstart = np.arange(LANES)[None, :] * SLC_BLOCK
    m = (start < bstart + SLC_BLOCK) & (start + CMP_LEN > bstart) & (tok >= 0)
    return m.astype(np.float32)


def _nsa_sel_kernel(slopes_ref, q_ref, cmp_ref, cmpt_ref, c2st_ref, oc_ref, sel_ref, *, tq, n_sel, n_case):
    g = pl.program_id(1)
    t0 = pl.program_id(2) * tq
    n_slots = cmp_ref.shape[2]
    qpos = t0 + _iota((1, tq), 1)
    q = q_ref[...]

    def run(ns):
        cend = _iota((ns, 1), 0) * CMP_STRIDE + (CMP_LEN - 1)
        d_c = qpos - cend
        mask_c = d_c >= 0
        d_cf = d_c.astype(F32)
        cmp = cmp_ref[0, 0, :ns, :]
        cmpt = cmpt_ref[0, 0, :, :ns]
        nb = min(LANES, ns * CMP_STRIDE // SLC_BLOCK)
        c2st = c2st_ref[:nb, :ns]
        imp_t = jnp.zeros((nb, tq), F32)
        for h in range(HG):
            q128 = _dot(q, _pick_head(HG * HD, h * HD)).astype(BF16)
            s_t = _dot_nt(cmp, q128) - slopes_ref[g * HG + h] * d_cf
            p_t = _masked_softmax(s_t, mask_c, axis=0).astype(BF16)
            oc_ref[0, 0, h] = _dot(cmpt, p_t).T
            imp_t = imp_t + _dot(c2st, p_t)
        blk = _iota((nb, tq), 0)
        cur = qpos >> 6
        forced = (blk == 0) | (blk == cur) | (blk == cur - 1)
        valid = blk * SLC_BLOCK <= qpos
        val = jnp.where(valid, imp_t + jnp.where(forced, FORCE, 0.0), NEG)
        sel_t = jnp.where(valid, _topk_mask(val, n_sel, 0), 0.0)
        sel_t = jnp.concatenate([sel_t, jnp.zeros((LANES - nb, tq), F32)], axis=0) if nb < LANES else sel_t
        sel_ref[0, 0] = ((sel_t - 1.0) * MASK_BIG).T.astype(sel_ref.dtype)

    per = n_slots // n_case
    need = (t0 + tq + CMP_STRIDE - 1) // CMP_STRIDE
    case = jnp.minimum((need - 1) // per, n_case - 1)
    for k in range(n_case):
        pl.when(case == k)(functools.partial(run, (k + 1) * per))


NSA_TQ_SEL, NSA_TQ_ATT = 512, 256


def _nsa_select(slopes, qn, cmp, B, T):
    tq = min(NSA_TQ_SEL, T)
    nq = T // tq
    n_slots = cmp.shape[2]
    n_slc = -(-T // SLC_BLOCK)
    assert n_slc <= LANES and SLC_BLOCK == 64
    c2st = jnp.asarray(_cmp_to_slc_np(n_slots, 0).T, BF16)
    cmpt = jnp.swapaxes(cmp, 2, 3)
    return pl.pallas_call(
        functools.partial(_nsa_sel_kernel, tq=tq, n_sel=min(SLC_TOPK, n_slc),
                          n_case=4 if n_slots % (4 * LANES) == 0 else 1),
        grid=(B, NSA_G, nq),
        in_specs=[pl.BlockSpec(memory_space=pltpu.SMEM),
                  pl.BlockSpec((tq, HG * HD), lambda b, g, j: (b * nq + j, g)),
                  pl.BlockSpec((1, 1, n_slots, LANES), lambda b, g, j: (b, g, 0, 0)),
                  pl.BlockSpec((1, 1, LANES, n_slots), lambda b, g, j: (b, g, 0, 0)),
                  pl.BlockSpec((LANES, n_slots), lambda b, g, j: (0, 0))],
        out_specs=[pl.BlockSpec((1, 1, HG, tq, LANES), lambda b, g, j: (b, g, 0, j, 0)),
                   pl.BlockSpec((1, 1, tq, LANES), lambda b, g, j: (b, g, j, 0))],
        out_shape=[jax.ShapeDtypeStruct((B, NSA_G, HG, T, LANES), F32),
                   jax.ShapeDtypeStruct((B, NSA_G, T, LANES), BF16)],
        compiler_params=_cp("parallel", "parallel", "parallel"),
        name="nsa_select",
    )(slopes, qn, cmp, cmpt, c2st)


def _tile_bits(selneg, B, T, tq, tk):
    nq = T // tq
    n_tiles = T // tk
    bpt = tk // SLC_BLOCK
    assert n_tiles <= 32
    anyb = (selneg.reshape(B, NSA_G, nq, tq, LANES) == 0).any(axis=3)
    anyt = anyb[..., :n_tiles * bpt].reshape(B, NSA_G, nq, n_tiles, bpt).any(axis=-1)
    w = jnp.sum(anyt.astype(jnp.uint32) << jnp.arange(n_tiles, dtype=jnp.uint32), axis=-1, dtype=jnp.uint32)
    return lax.bitcast_convert_type(w, jnp.int32).reshape(-1)


def _nsa_attn_kernel(bits_ref, slopes_ref, q_ref, ks_ref, vs_ref, kw_ref, vw_ref, blk1h_ref, sel_ref, oc_ref,
                     gates_ref, o_ref, m_s, acc_s, *, tq, tk, wk):
    b, g, j = pl.program_id(0), pl.program_id(1), pl.program_id(2)
    t0 = j * tq
    q = q_ref[...]
    selneg = sel_ref[0, 0]
    lane = _iota((tq, LANES), 1)
    slopes = [slopes_ref[g * HG + h] for h in range(HG)]
    q_sel, q_win = [], []
    for h in range(HG):
        q128 = _dot(q, _pick_head(HG * HD, h * HD))
        q_win.append(q128.astype(BF16))
        q_sel.append(jnp.concatenate([jnp.where(lane == HD, slopes[h], q128).astype(BF16), selneg], axis=1))
    q4 = jnp.concatenate(q_sel, axis=0)
    qw4 = jnp.concatenate(q_win, axis=0)
    qpos = t0 + _iota((tq, 1), 0)

    m_s[...] = jnp.full(m_s.shape, NEG, F32)
    acc_s[...] = jnp.zeros(acc_s.shape, F32)
    word = bits_ref[(b * NSA_G + g) * pl.num_programs(2) + j]

    def tile(kt, diagonal):
        k0 = pl.multiple_of(kt * tk, tk)
        kop = jnp.concatenate([ks_ref[pl.ds(k0, tk), :], blk1h_ref[pl.ds(k0, tk), :]], axis=1)
        vop = vs_ref[pl.ds(k0, tk), :]
        s = _dot_nt(q4, kop)
        off = (k0 - t0).astype(F32)
        if diagonal:
            causal = (k0 + _iota((1, tk), 1)) <= qpos
        for h in range(HG):
            sh = s[h * tq:(h + 1) * tq] + slopes[h] * off
            if diagonal:
                sh = jnp.where(causal, sh, NEG)
            _flash_step(sh, vop, m_s.at[h], acc_s.at[h], False)

    def body(kt, carry):
        @pl.when(((word >> kt) & 1) == 1)
        def _():
            tile(kt, False)
        return carry

    n_before = t0 // tk
    lax.fori_loop(0, n_before, body, 0)
    for dd in range(max(1, tq // tk)):
        tile(n_before + dd, True)

    w0 = pl.multiple_of(jnp.maximum(t0 + tq - wk, 0), tq)
    kw = kw_ref[pl.ds(w0, wk), :]
    vw = vw_ref[pl.ds(w0, wk), :]
    d_w = qpos - (w0 + _iota((1, wk), 1))
    mask_w = (d_w >= 0) & (d_w < WINDOW)
    d_wf = d_w.astype(F32)
    s_w = _dot_nt(qw4, kw)
    gates = gates_ref[0]
    o_heads = []
    for h in range(HG):
        sw = jnp.where(mask_w, s_w[h * tq:(h + 1) * tq] - slopes[h] * d_wf, NEG)
        e_w = jnp.where(mask_w, jnp.exp(sw - jnp.max(sw, axis=1, keepdims=True)), 0.0)
        o_w = _hi_over_lo(_dot(e_w.astype(BF16), vw))
        o_s = _hi_over_lo(acc_s[h])
        o_heads.append(gates[:, 3 * h:3 * h + 1] * oc_ref[0, 0, h] + gates[:, 3 * h + 1:3 * h + 2] * o_s
                       + gates[:, 3 * h + 2:3 * h + 3] * o_w)
    o_ref[...] = jnp.concatenate([_pair_heads(o_heads[0], o_heads[1]), _pair_heads(o_heads[2], o_heads[3])],
                                 axis=1).astype(o_ref.dtype)


def _nsa_attend(slopes, qn, pr, selneg, oc, gates_g, B, T):
    tq = min(NSA_TQ_ATT, T)
    nq = T // tq
    tk = min(SEL_TK, T)
    assert tk == SEL_TK and HG == 4
    wk = min(WINDOW + tq, T)
    bits = _tile_bits(selneg, B, T, tq, tk)
    blk1h = jnp.asarray(np.arange(T)[:, None] // SLC_BLOCK == np.arange(LANES)[None, :], BF16)
    kv_spec = pl.BlockSpec((T, LANES), lambda b, g, j, bits: (b, g))
    return pl.pallas_call(
        functools.partial(_nsa_attn_kernel, tq=tq, tk=tk, wk=wk),
        grid_spec=pltpu.PrefetchScalarGridSpec(
            num_scalar_prefetch=1, grid=(B, NSA_G, nq),
            in_specs=[pl.BlockSpec(memory_space=pltpu.SMEM),
                      pl.BlockSpec((tq, HG * HD), lambda b, g, j, bits: (b * nq + j, g)),
                      kv_spec, kv_spec, kv_spec, kv_spec,
                      pl.BlockSpec((T, LANES), lambda b, g, j, bits: (0, 0)),
                      pl.BlockSpec((1, 1, tq, LANES), lambda b, g, j, bits: (b, g, j, 0)),
                      pl.BlockSpec((1, 1, HG, tq, LANES), lambda b, g, j, bits: (b, g, 0, j, 0)),
                      pl.BlockSpec((1, tq, LANES), lambda b, g, j, bits: (g, b * nq + j, 0))],
            out_specs=pl.BlockSpec((tq, HG * HD), lambda b, g, j, bits: (b * nq + j, g)),
            scratch_shapes=[pltpu.VMEM((HG, tq, LANES), F32), pltpu.VMEM((HG, tq, LANES), F32)]),
        out_shape=jax.ShapeDtypeStruct((B * T, QW), BF16),
        compiler_params=_cp("parallel", "parallel", "parallel"),
        name="nsa_attend",
    )(bits, slopes, qn, pr["ks_op"], pr["vs_op"], pr["kw_op"], pr["vw_op"], blk1h, selneg, oc, gates_g)


FOX_TQ, FOX_TK, FOX_ROWS = 1024, 512, 1024


def _fox_kernel(q_ref, k0_ref, k1_ref, v0_ref, v1_ref, d0_ref, d1_ref, o_ref, m_s, acc_s, *, tq, rows, tk):
    j = pl.program_id(2)
    n_part = tq // rows
    k_refs, v_refs, d_refs = (k0_ref, k1_ref), (v0_ref, v1_ref), (d0_ref, d1_ref)
    m_s[...] = jnp.full(m_s.shape, NEG, F32)
    acc_s[...] = jnp.zeros(acc_s.shape, F32)

    def step(kt, diagonal):
        k0 = pl.multiple_of(kt * tk, tk)
        for hh in range(2):
            k = k_refs[hh][pl.ds(k0, tk), :]
            v = v_refs[hh][pl.ds(k0, tk), :]
            bias = d_refs[hh][0, kt] * LOG2E
            for r in range(n_part):
                s = _dot_nt(q_ref[r * rows:(r + 1) * rows, :], k) - bias
                if diagonal:
                    qpos = j * tq + r * rows + _iota((rows, 1), 0)
                    s = jnp.where(k0 + _iota((1, tk), 1) <= qpos, s, NEG)
                _flash_step(s, v, m_s.at[hh * n_part + r], acc_s.at[hh * n_part + r], True)

    def body(kt, carry):
        step(kt, False)
        return carry

    n_full = (j * tq) // tk
    lax.fori_loop(0, n_full, body, 0)
    for dd in range(tq // tk):
        step(n_full + dd, True)
    for r in range(n_part):
        o_ref[r * rows:(r + 1) * rows, :] = _pair_heads(_hi_over_lo(acc_s[r]),
                                                         _hi_over_lo(acc_s[n_part + r])).astype(o_ref.dtype)


def _fox_attend(qf, kf_op, vf_op, d_t, B, T):
    tq = min(FOX_TQ, T)
    tk = min(FOX_TK, tq)
    rows = min(FOX_ROWS, tq)
    nq = T // tq
    d4 = d_t.reshape(B * FOX_H, T // tk, 1, tk)
    kv_spec = lambda hh: pl.BlockSpec((T, LANES), lambda b, hp, j: (b, 2 * hp + hh))
    d_spec = lambda hh: pl.BlockSpec((1, T // tk, 1, tk), lambda b, hp, j: (b * FOX_H + 2 * hp + hh, 0, 0, 0))
    n_chain = 2 * (tq // rows)
    return pl.pallas_call(
        functools.partial(_fox_kernel, tq=tq, rows=rows, tk=tk),
        grid=(B, FOX_H // 2, nq),
        in_specs=[pl.BlockSpec((tq, 2 * HD), lambda b, hp, j: (b * nq + j, hp)),
                  kv_spec(0), kv_spec(1), kv_spec(0), kv_spec(1), d_spec(0), d_spec(1)],
        out_specs=pl.BlockSpec((tq, 2 * HD), lambda b, hp, j: (b * nq + j, hp)),
        out_shape=jax.ShapeDtypeStruct((B * T, FOX_H * HD), BF16),
        scratch_shapes=[pltpu.VMEM((n_chain, rows, LANES), F32), pltpu.VMEM((n_chain, rows, LANES), F32)],
        compiler_params=_cp("parallel", "parallel", "parallel"),
        name="fox_attend",
    )(qf, kf_op, kf_op, vf_op, vf_op, d4, d4)


def _out_kernel(on_ref, of_ref, x_ref, g1_ref, sc2_ref, sh2_ref, gpm_ref, gpf_ref, wo_ref, wr_ref, br_ref,
                x1_ref, h2_ref, gd_ref):
    y = _dot(on_ref[...], wo_ref[:QW]) + _dot(of_ref[...], wo_ref[QW:])
    x1 = x_ref[...] + g1_ref[...] * _rms(y, gpm_ref[...])
    x1_ref[...] = x1
    h2 = (_rms(x1, gpf_ref[...]) * (1.0 + sc2_ref[...]) + sh2_ref[...]).astype(BF16)
    h2_ref[...] = h2
    scores_t = jax.nn.sigmoid(_dot_nt(wr_ref[...], h2))
    sel_t = _topk_mask(scores_t + br_ref[...], TOP_K, 0)
    g_t = sel_t * scores_t
    gd_ref[...] = (g_t / jnp.sum(g_t, axis=0, keepdims=True) * ROUTED_SCALE).T


def _out_proj(o_n, o_f, x2, gate1, scale2, shift2, g_post_mix, g_pre_ffn, wo, wr, br, rows_per_mod):
    n, d = x2.shape
    tm = min(256, n)
    if rows_per_mod >= tm:
        per = rows_per_mod // tm
        mod_spec = pl.BlockSpec((None, 1, d), lambda i: (i // per, 0, 0))
    else:
        gate1, scale2, shift2 = (jnp.repeat(a[:, 0, :], rows_per_mod, axis=0) for a in (gate1, scale2, shift2))
        mod_spec = pl.BlockSpec((tm, d), lambda i: (i, 0))
    row = lambda w: pl.BlockSpec((tm, w), lambda i: (i, 0))
    full = lambda a: pl.BlockSpec(a.shape, lambda i: (0,) * a.ndim)
    gpm, gpf = g_post_mix.reshape(1, d), g_pre_ffn.reshape(1, d)
    return pl.pallas_call(
        _out_kernel,
        grid=(n // tm,),
        in_specs=[row(QW), row(QW), row(d), mod_spec, mod_spec, mod_spec, full(gpm), full(gpf),
                  full(wo), full(wr), full(br)],
        out_specs=[row(d), row(d), row(LANES)],
        out_shape=[jax.ShapeDtypeStruct((n, d), F32), jax.ShapeDtypeStruct((n, d), BF16),
                   jax.ShapeDtypeStruct((n, LANES), F32)],
        compiler_params=_cp("parallel"),
        name="out_proj_router",
    )(o_n, o_f, x2, gate1, scale2, shift2, gpm, gpf, wo, wr, br)


MOE_EB, MOE_TM = 8, 1024


def _moe_kernel(h_ref, gd_ref, wg_ref, wu_ref, wd_ref, o_ref, *, eb):
    blk = pl.program_id(1)

    @pl.when(blk == 0)
    def _():
        o_ref[...] = jnp.zeros_like(o_ref)

    h = h_ref[...]
    f = wg_ref.shape[2]
    gd = gd_ref[...]
    g_hi = gd.astype(BF16)
    g_lo = (gd - g_hi.astype(F32)).astype(BF16)
    expand = _onehot(_iota((LANES, eb * f), 0) == blk * eb + _iota((LANES, eb * f), 1) // f)
    gates = _dot(g_hi, expand) + _dot(g_lo, expand)
    parts = []
    for i in range(eb):
        a = _dot(h, wg_ref[i])
        parts.append((a * jax.nn.sigmoid(a) * _dot(h, wu_ref[i]) * gates[:, i * f:(i + 1) * f]).astype(BF16))
    hb = jnp.concatenate(parts, axis=1)
    o_ref[...] += _dot(hb, wd_ref[...].reshape(eb * f, wd_ref.shape[2]))


def _moe_routed(h2, gd, wg, wu, wd):
    n, d = h2.shape
    n_e, _, f = wg.shape
    tm = min(MOE_TM, n)
    eb = MOE_EB
    assert n_e % eb == 0
    return pl.pallas_call(
        functools.partial(_moe_kernel, eb=eb),
        grid=(n // tm, n_e // eb),
        in_specs=[pl.BlockSpec((tm, d), lambda i, e: (i, 0)),
                  pl.BlockSpec((tm, LANES), lambda i, e: (i, 0)),
                  pl.BlockSpec((eb, d, f), lambda i, e: (e, 0, 0)),
                  pl.BlockSpec((eb, d, f), lambda i, e: (e, 0, 0)),
                  pl.BlockSpec((eb, f, d), lambda i, e: (e, 0, 0))],
        out_specs=pl.BlockSpec((tm, d), lambda i, e: (i, 0)),
        out_shape=jax.ShapeDtypeStruct((n, d), F32),
        compiler_params=_cp("parallel", "arbitrary"),
        name="moe_routed",
    )(h2, gd, wg, wu, wd)


def _final_kernel(r_ref, h_ref, x1_ref, g2_ref, gpf_ref, wg_ref, wu_ref, wd_ref, y_ref):
    h = h_ref[...]
    a = _dot(h, wg_ref[...])
    hb = (a * jax.nn.sigmoid(a) * _dot(h, wu_ref[...])).astype(BF16)
    f = r_ref[...] + _dot(hb, wd_ref[...])
    y_ref[...] = x1_ref[...] + g2_ref[...] * _rms(f, gpf_ref[...])


def _final(routed, h2, x1, gate2, g_post_ffn, wsg, wsu, wsd, rows_per_mod):
    n, d = x1.shape
    tm = min(512, n)
    if rows_per_mod >= tm:
        per = rows_per_mod // tm
        mod_spec = pl.BlockSpec((None, 1, d), lambda i: (i // per, 0, 0))
    else:
        gate2 = jnp.repeat(gate2[:, 0, :], rows_per_mod, axis=0)
        mod_spec = pl.BlockSpec((tm, d), lambda i: (i, 0))
    row = pl.BlockSpec((tm, d), lambda i: (i, 0))
    full = lambda a: pl.BlockSpec(a.shape, lambda i: (0,) * a.ndim)
    gpf = g_post_ffn.reshape(1, d)
    return pl.pallas_call(
        _final_kernel,
        grid=(n // tm,),
        in_specs=[row, row, row, mod_spec, full(gpf), full(wsg), full(wsu), full(wsd)],
        out_specs=row,
        out_shape=jax.ShapeDtypeStruct((n, d), F32),
        compiler_params=_cp("parallel"),
        name="shared_ffn_final",
    )(routed, h2, x1, gate2, gpf, wsg, wsu, wsd)


PAGES_PER_STEP = 16


def _page_specs(block, n_pages, pp):
    def spec(jj):
        return pl.BlockSpec(block, lambda b, c, pt: (pt[b * n_pages + c * pp + jj],) + (0,) * (len(block) - 1))
    return [spec(jj) for jj in range(pp)]


def _cmp_sample_kernel(pt_ref, *refs, pp):
    pages = refs[:pp]
    w1_ref, bias_ref, w2_ref, o_ref, carry_ref, x_s = refs[pp:]
    c = pl.program_id(1)
    page = pages[0].shape[-1]
    rows = pp * page // CMP_STRIDE

    @pl.when(c == 0)
    def _():
        carry_ref[...] = jnp.zeros_like(carry_ref)

    for jj, p in enumerate(pages):
        for kv in range(2):
            x_s[kv, jj * page:(jj + 1) * page, :] = p[0, kv].reshape(NSA_G * HD, page).T

    acc = []
    for kv in range(2):
        taps = [x_s[kv, pl.ds(t, rows, stride=CMP_STRIDE), :].astype(BF16) for t in range(CMP_STRIDE)]
        acc.append(_dot(jnp.concatenate(taps, axis=1), w1_ref[kv]))
    out = [None, None]
    for kv in range(2):
        a = acc[kv][:, :2 * CMP_HID]
        b = acc[kv][:, 2 * CMP_HID:]
        a_prev = pltpu.roll(a, 1, 0)
        a_prev = jnp.where(_iota(a.shape, 0) == 0, carry_ref[kv][0:1], a_prev)
        carry_ref[kv] = jnp.broadcast_to(a[rows - 1:rows], carry_ref.shape[1:])
        hid = jax.nn.gelu(a_prev + b + bias_ref[kv]).astype(BF16)
        for g in range(NSA_G):
            term = _dot(hid[:, g * CMP_HID:(g + 1) * CMP_HID], w2_ref[kv, g])
            out[g] = term if out[g] is None else out[g] + term
    for g in range(NSA_G):
        o_ref[0, g] = out[g].astype(o_ref.dtype)


def _cmp_sample_weights(w_cmp1, b_cmp1, w_cmp2, pe_cmp):
    w1 = w_cmp1.reshape(2, 2, CMP_STRIDE, HD, CMP_HID)
    z = jnp.zeros_like(w1[:, 0])
    top = jnp.concatenate([w1[:, 0], z, w1[:, 1], z], axis=-1)
    bot = jnp.concatenate([z, w1[:, 0], z, w1[:, 1]], axis=-1)
    w1bd = jnp.concatenate([top, bot], axis=2).astype(BF16)
    w1bd = w1bd.reshape(2, CMP_STRIDE * LANES, 4 * CMP_HID)
    bias = (jnp.einsum("kf,kfh->kh", pe_cmp.reshape(2, -1).astype(BF16), w_cmp1.astype(BF16),
                       preferred_element_type=F32) + b_cmp1)
    bias = jnp.concatenate([bias, bias], axis=-1).reshape(2, 1, 2 * CMP_HID)
    z2 = jnp.zeros((CMP_HID, HD), F32)
    lo = lambda w: jnp.concatenate([w, z2], axis=1)
    hi = lambda w: jnp.concatenate([z2, w], axis=1)
    w2 = jnp.stack([jnp.stack([lo(w_cmp2[0]), hi(w_cmp2[0])]),
                    jnp.stack([hi(w_cmp2[1]), lo(w_cmp2[1])])]).astype(BF16)
    return w1bd, bias, w2


def _compress_sample(cache, pt_flat, Bs, n_pages, csw):
    w1bd, bias, w2 = csw
    pp = min(2 * PAGES_PER_STEP, n_pages)
    page = cache.shape[-1]
    rows = pp * (page // CMP_STRIDE)
    n_slots = n_pages * (page // CMP_STRIDE)
    full = lambda a: pl.BlockSpec(a.shape, lambda b, c, pt: (0,) * a.ndim)
    return pl.pallas_call(
        functools.partial(_cmp_sample_kernel, pp=pp),
        grid_spec=pltpu.PrefetchScalarGridSpec(
            num_scalar_prefetch=1, grid=(Bs, n_pages // pp),
            in_specs=_page_specs((1,) + cache.shape[1:], n_pages, pp) + [full(w1bd), full(bias), full(w2)],
            out_specs=pl.BlockSpec((1, NSA_G, rows, LANES), lambda b, c, pt: (b, 0, c, 0)),
            scratch_shapes=[pltpu.VMEM((2, 8, 2 * CMP_HID), F32), pltpu.VMEM((2, pp * page, LANES), F32)]),
        out_shape=jax.ShapeDtypeStruct((Bs, NSA_G, n_slots, LANES), BF16),
        compiler_params=_cp("parallel", "arbitrary"),
        name="compress_sample",
    )(pt_flat, *([cache] * pp), w1bd, bias, w2)


def _page_cumsum_kernel(x_ref, o_ref):
    o_ref[...] = _cumsum_block(x_ref[...])


def _page_cumsum(logf_cache_t):
    n_phys, h, page = logf_cache_t.shape
    rows = n_phys * h
    tr = next(t for t in (2048, 1024, 512, 256, 128, 64, 32, 16, 8) if rows % t == 0)
    out = pl.pallas_call(
        _page_cumsum_kernel,
        grid=(rows // tr,),
        in_specs=[pl.BlockSpec((tr, page), lambda i: (i, 0))],
        out_specs=pl.BlockSpec((tr, page), lambda i: (i, 0)),
        out_shape=jax.ShapeDtypeStruct((rows, page), F32),
        compiler_params=_cp("parallel"),
        name="page_cumsum",
    )(logf_cache_t.reshape(rows, page))
    return out.reshape(n_phys, h, page)


ROWS_S = NSA_G * HG * 8


SEL_SEQS = 8


def _sel_sample_kernel(q_ref, cmp_ref, c2s_ref, slope_ref, qidx_ref, oc_ref, sel_ref, *, past, n_take):
    n_slots = cmp_ref.shape[2]
    rg = ROWS_S // NSA_G
    slot = _iota((1, n_slots), 1)
    cend = slot * CMP_STRIDE + (CMP_STRIDE - 1)
    c2s = c2s_ref[...]
    blk = _iota((8, LANES), 1)
    n_blk = past // SLC_BLOCK
    forced = (blk == 0) | (blk == n_blk - 1)
    for sq in range(q_ref.shape[0]):
        for g in range(NSA_G):
            r = slice(g * rg, (g + 1) * rg)
            cmp = cmp_ref[sq, g]
            d_c = past + qidx_ref[r] - cend
            mask = (d_c >= 0) & (slot >= 1)
            s = _dot_nt(q_ref[sq, r], cmp) - slope_ref[r] * d_c.astype(F32)
            pb = _masked_softmax(s, mask).astype(BF16)
            oc_ref[sq, r] = pltpu.roll(_dot(pb, cmp), HD, 1)
            imp_h = _dot(pb, c2s)
            imp = imp_h[0:8] + imp_h[8:16] + imp_h[16:24] + imp_h[24:32]
            val = jnp.where(blk < n_blk, imp + jnp.where(forced, FORCE, 0.0), NEG)
            sel8 = _topk_mask(val, n_take, 1)
            sel_ref[sq, r] = jnp.concatenate([sel8] * HG, axis=0).astype(sel_ref.dtype)


def _select_sample(qs, cmp_s, slope_rows, qidx_rows, Bs, past):
    n_slots = cmp_s.shape[2]
    n_slc = past // SLC_BLOCK + 1
    assert past % SLC_BLOCK == 0 and n_slc - 1 <= LANES and HG == 4
    nb = SEL_SEQS if Bs % SEL_SEQS == 0 else 1
    c2s = jnp.asarray(_cmp_to_slc_np(n_slots, -1), BF16)
    full = lambda a: pl.BlockSpec(a.shape, lambda b: (0,) * a.ndim)
    return pl.pallas_call(
        functools.partial(_sel_sample_kernel, past=past, n_take=min(SLC_TOPK, n_slc) - 1),
        grid=(Bs // nb,),
        in_specs=[pl.BlockSpec((nb, ROWS_S, LANES), lambda b: (b, 0, 0)),
                  pl.BlockSpec((nb, NSA_G, n_slots, LANES), lambda b: (b, 0, 0, 0)),
                  full(c2s), full(slope_rows), full(qidx_rows)],
        out_specs=[pl.BlockSpec((nb, ROWS_S, LANES), lambda b: (b, 0, 0)),
                   pl.BlockSpec((nb, ROWS_S, LANES), lambda b: (b, 0, 0))],
        out_shape=[jax.ShapeDtypeStruct((Bs, ROWS_S, LANES), F32),
                   jax.ShapeDtypeStruct((Bs, ROWS_S, LANES), BF16)],
        compiler_params=_cp("parallel"),
        name="select_sample",
    )(qs, cmp_s, c2s, slope_rows, qidx_rows)


def _online_update(s, ok, v, m_s, l_s, acc_s, base2=False, v_t=False):
    ex = jnp.exp2 if base2 else jnp.exp
    s = jnp.where(ok, s, NEG)
    m_old = m_s[...]
    m_new = jnp.maximum(m_old, jnp.max(s, axis=1, keepdims=True))
    alpha = ex(m_old - m_new)
    e = jnp.where(ok, ex(s - m_new), 0.0)
    l_s[...] = alpha * l_s[...] + jnp.sum(e, axis=1, keepdims=True)
    acc_s[...] = alpha * acc_s[...] + (_dot_nt if v_t else _dot)(e.astype(BF16), v)
    m_s[...] = m_new


def _pages_t(pages, kv, width):
    return jnp.concatenate([p[0, kv].reshape(width, p.shape[-1]).astype(BF16) for p in pages], axis=1)


def _pad_rows8(x):
    return jnp.concatenate([x, jnp.zeros((8 - x.shape[0], x.shape[1]), x.dtype)], axis=0)


def _slc_sample_kernel(pt_ref, *refs, pp, past, t_new):
    pages = refs[:pp]
    q_ref, sel_ref, new_ref, slope_ref, qidx_ref, o_ref, m_s, l_s, acc_s = refs[pp:]
    c = pl.program_id(1)
    page = pages[0].shape[-1]
    nk = pp * page

    @pl.when(c == 0)
    def _():
        m_s[...] = jnp.full(m_s.shape, NEG, F32)
        l_s[...] = jnp.zeros(l_s.shape, F32)
        acc_s[...] = jnp.zeros(acc_s.shape, F32)

    q = q_ref[0]
    slope = slope_ref[...]
    qidx = qidx_ref[...]
    sel = sel_ref[0]
    blk_lane = _iota(sel.shape, 1)
    b0 = c * (nk // SLC_BLOCK)
    in_chunk = (blk_lane >= b0) & (blk_lane < b0 + nk // SLC_BLOCK)

    @pl.when(jnp.max(jnp.where(in_chunk, sel.astype(F32), 0.0)) > 0.0)
    def _():
        kpos = c * nk + _iota((1, nk), 1)
        expand = _onehot(_iota((LANES, 1), 0) == (kpos >> 6))
        ok = _dot(sel, expand) > 0.5
        dist = (past + qidx - kpos).astype(F32)
        s = _dot(q, _pages_t(pages, 0, NSA_G * HD)) - slope * dist
        _online_update(s, ok, _pages_t(pages, 1, NSA_G * HD), m_s, l_s, acc_s, v_t=True)

    @pl.when(c == pl.num_programs(1) - 1)
    def _():
        new = _pad_rows8(new_ref[0])
        kj = _iota((1, 8), 1)
        ok_new = (kj <= qidx) & (kj < t_new)
        s_new = _dot_nt(q, new[:, :LANES].astype(BF16)) - slope * (qidx - kj).astype(F32)
        _online_update(s_new, ok_new, new[:, LANES:].astype(BF16), m_s, l_s, acc_s)
        o_ref[0] = acc_s[...] / l_s[...]


def _slc_sample(cache, pt_flat, qs, sel_rows, kvs_new, slope_rows, qidx_rows, Bs, n_pages, past):
    pp = min(PAGES_PER_STEP, n_pages)
    t_new = kvs_new.shape[1]
    full = lambda a: pl.BlockSpec(a.shape, lambda b, c, pt: (0,) * a.ndim)
    per_seq = lambda a: pl.BlockSpec((1,) + a.shape[1:], lambda b, c, pt: (b,) + (0,) * (a.ndim - 1))
    return pl.pallas_call(
        functools.partial(_slc_sample_kernel, pp=pp, past=past, t_new=t_new),
        grid_spec=pltpu.PrefetchScalarGridSpec(
            num_scalar_prefetch=1, grid=(Bs, n_pages // pp),
            in_specs=_page_specs((1,) + cache.shape[1:], n_pages, pp)
            + [per_seq(qs), per_seq(sel_rows), per_seq(kvs_new), full(slope_rows), full(qidx_rows)],
            out_specs=pl.BlockSpec((1, ROWS_S, LANES), lambda b, c, pt: (b, 0, 0)),
            scratch_shapes=[pltpu.VMEM((ROWS_S, 1), F32), pltpu.VMEM((ROWS_S, 1), F32),
                            pltpu.VMEM((ROWS_S, LANES), F32)]),
        out_shape=jax.ShapeDtypeStruct((Bs, ROWS_S, LANES), F32),
        compiler_params=_cp("parallel", "arbitrary"),
        name="selected_sample",
    )(pt_flat, *([cache] * pp), qs, sel_rows, kvs_new, slope_rows, qidx_rows)


def _win_sample_kernel(q_ref, win_ref, new_ref, oc_ref, os_ref, gates_ref, slope_ref, qidx_ref, o_ref, *, t_new):
    q = q_ref[0]
    n_win = win_ref.shape[-1]
    k_t = win_ref[0, 0].reshape(NSA_G * HD, n_win).astype(BF16)
    v_t = win_ref[0, 1].reshape(NSA_G * HD, n_win).astype(BF16)
    new = _pad_rows8(new_ref[0])
    kj = _iota((1, n_win + 8), 1)
    d_w = n_win + qidx_ref[...] - kj
    mask = (d_w >= 0) & (d_w < WINDOW) & (kj < n_win + t_new)
    s = jnp.concatenate([_dot(q, k_t), _dot_nt(q, new[:, :LANES].astype(BF16))], axis=1)
    p = _masked_softmax(s - slope_ref[...] * d_w.astype(F32), mask).astype(BF16)
    o_w = _dot_nt(p[:, :n_win], v_t) + _dot(p[:, n_win:], new[:, LANES:].astype(BF16))
    gates = gates_ref[0]
    o_ref[0] = gates[:, 0:1] * oc_ref[0] + gates[:, 1:2] * os_ref[0] + gates[:, 2:3] * o_w


def _win_sample(qs, state_win, kvw_new, oc, o_s, gate_rows, slope_rows, qidx_rows, Bs):
    t_new = kvw_new.shape[1]
    full = lambda a: pl.BlockSpec(a.shape, lambda b: (0,) * a.ndim)
    per_seq = lambda a: pl.BlockSpec((1,) + a.shape[1:], lambda b: (b,) + (0,) * (a.ndim - 1))
    return pl.pallas_call(
        functools.partial(_win_sample_kernel, t_new=t_new),
        grid=(Bs,),
        in_specs=[per_seq(qs), per_seq(state_win), per_seq(kvw_new), per_seq(oc), per_seq(o_s),
                  per_seq(gate_rows), full(slope_rows), full(qidx_rows)],
        out_specs=pl.BlockSpec((1, ROWS_S, LANES), lambda b: (b, 0, 0)),
        out_shape=jax.ShapeDtypeStruct((Bs, ROWS_S, LANES), F32),
        compiler_params=_cp("parallel"),
        name="window_sample",
    )(qs, state_win, kvw_new, oc, o_s, gate_rows, slope_rows, qidx_rows)


ROWS_F = FOX_H * 8
FOX_K = FOX_H * HD


def _fox_sample_kernel(pt_ref, *refs, pp, t_new):
    pages, pcs = refs[:pp], refs[pp:2 * pp]
    q_ref, new_ref, csnew_ref, qidx_ref, o_ref, m_s, l_s, acc_s, carry_s = refs[2 * pp:]
    c = pl.program_id(1)

    @pl.when(c == 0)
    def _():
        m_s[...] = jnp.full(m_s.shape, NEG, F32)
        l_s[...] = jnp.zeros(l_s.shape, F32)
        acc_s[...] = jnp.zeros(acc_s.shape, F32)
        carry_s[...] = jnp.zeros(carry_s.shape, F32)

    q = q_ref[0]
    carry = carry_s[...]
    chunks = []
    for pc in pcs:
        x = pc[0]
        chunks.append(x + carry)
        carry = carry + jnp.broadcast_to(x[:, x.shape[1] - 1:], carry.shape)
    carry_s[...] = carry
    d = jnp.concatenate(chunks, axis=1)
    nk = d.shape[1]
    bias = jnp.concatenate([jnp.broadcast_to(d[h:h + 1], (8, nk)) for h in range(FOX_H)], axis=0)
    s = _dot(q, _pages_t(pages, 0, FOX_K)) - bias * LOG2E
    _online_update(s, jnp.full((ROWS_F, nk), True), _pages_t(pages, 1, FOX_K), m_s, l_s, acc_s,
                   base2=True, v_t=True)

    @pl.when(c == pl.num_programs(1) - 1)
    def _():
        new = _pad_rows8(new_ref[0])
        kj = _iota((1, 8), 1)
        ok_new = (kj <= qidx_ref[...]) & (kj < t_new)
        total = jnp.concatenate([jnp.broadcast_to(carry[h:h + 1, 0:1], (8, 1)) for h in range(FOX_H)], axis=0)
        s_new = _dot_nt(q, new[:, :FOX_K].astype(BF16)) - (total + csnew_ref[0]) * LOG2E
        _online_update(s_new, ok_new, new[:, FOX_K:].astype(BF16), m_s, l_s, acc_s, base2=True)
        o_ref[0] = acc_s[...] / l_s[...]


def _fox_sample(cache, page_cs, pt_flat, qf_bd, kvf_new, csnew_rows, qidx_f, Bs, n_pages):
    pp = min(PAGES_PER_STEP, n_pages)
    page = cache.shape[-1]
    t_new = kvf_new.shape[1]
    full = lambda a: pl.BlockSpec(a.shape, lambda b, c, pt: (0,) * a.ndim)
    per_seq = lambda a: pl.BlockSpec((1,) + a.shape[1:], lambda b, c, pt: (b,) + (0,) * (a.ndim - 1))
    return pl.pallas_call(
        functools.partial(_fox_sample_kernel, pp=pp, t_new=t_new),
        grid_spec=pltpu.PrefetchScalarGridSpec(
            num_scalar_prefetch=1, grid=(Bs, n_pages // pp),
            in_specs=_page_specs((1,) + cache.shape[1:], n_pages, pp)
            + _page_specs((1,) + page_cs.shape[1:], n_pages, pp)
            + [per_seq(qf_bd), per_seq(kvf_new), per_seq(csnew_rows), full(qidx_f)],
            out_specs=pl.BlockSpec((1, ROWS_F, FOX_K), lambda b, c, pt: (b, 0, 0)),
            scratch_shapes=[pltpu.VMEM((ROWS_F, 1), F32), pltpu.VMEM((ROWS_F, 1), F32),
                            pltpu.VMEM((ROWS_F, FOX_K), F32), pltpu.VMEM((FOX_H, LANES), F32)]),
        out_shape=jax.ShapeDtypeStruct((Bs, ROWS_F, FOX_K), F32),
        compiler_params=_cp("parallel", "arbitrary"),
        name="fox_sample",
    )(pt_flat, *([cache] * pp), *([page_cs] * pp), qf_bd, kvf_new, csnew_rows, qidx_f)


def _alibi_slopes():
    return jnp.exp2(-8.0 * jnp.arange(1, NSA_H + 1, dtype=F32) / NSA_H)


def _ffn_tail(o_n, o_f, x2, mod, norms, lw, rows_per_mod):
    gate1, scale2, shift2, gate2 = mod
    g_post_mix, g_pre_ffn, g_post_ffn = norms
    x1, h2, gd = _out_proj(o_n, o_f, x2, gate1, scale2, shift2, g_post_mix, g_pre_ffn,
                           lw["wo"], lw["wr"], lw["br"], rows_per_mod)
    routed = _moe_routed(h2, gd, lw["weg"], lw["weu"], lw["wed"])
    return _final(routed, h2, x1, gate2, g_post_ffn, lw["wsg"], lw["wsu"], lw["wsd"], rows_per_mod)


def _prompt_layer(x, mod, norms, lw):
    B, T, d = x.shape
    n = B * T
    shift1, scale1, gate1, shift2, scale2, gate2 = mod
    g_pre_mix, g_post_mix, g_pre_ffn, g_post_ffn = norms
    x2 = x.reshape(n, d)
    pr = _in_proj(x2, scale1, shift1, g_pre_mix, lw["w_in"], lw["b_small"], T, "prompt")
    small = pr["small"]
    logf = small[:, N_GATE:N_GATE + FOX_H]
    gates_g = small[:, :N_GATE].reshape(n, NSA_G, HG * 3).transpose(1, 0, 2)
    gates_g = jnp.pad(gates_g, ((0, 0), (0, 0), (0, LANES - HG * 3)))
    slopes = _alibi_slopes()

    cmp = _compress_prompt(pr["kvc"], B, T, lw["cmp_w"])
    oc, selneg = _nsa_select(slopes, pr["qn"], cmp, B, T)
    o_n = _nsa_attend(slopes, pr["qn"], pr, selneg, oc, gates_g, B, T)

    d_t = _cumsum_prompt(logf.reshape(B, T, FOX_H).transpose(0, 2, 1))
    o_f = _fox_attend(pr["qf"], pr["kf_op"], pr["vf_op"], d_t, B, T)

    y = _ffn_tail(o_n, o_f, x2, (gate1, scale2, shift2, gate2), (g_post_mix, g_pre_ffn, g_post_ffn), lw, T)
    win_rows = min(WINDOW, T)
    state = lambda a, heads: jnp.moveaxis(a.reshape(B, 2, heads, HD, -1), -1, 1)
    states = (state(pr["kvc_t"], NSA_G), state(pr["kvs_t"], NSA_G), state(pr["kvf_t"], FOX_H),
              logf.reshape(B, T, FOX_H), state(pr["kvw_t"][:, :, T - win_rows:], NSA_G))
    return y.reshape(B, T, d), states


def _sample_layer(x, mod, norms, lw, caches):
    Bs, Tn, d = x.shape
    n = Bs * Tn
    assert Tn <= 8
    cache_cmp, cache_slc, cache_fox, cache_logf, state_win, page_table = caches
    n_pages = page_table.shape[1]
    page = cache_cmp.shape[1]
    past = n_pages * page
    shift1, scale1, gate1, shift2, scale2, gate2 = mod
    g_pre_mix, g_post_mix, g_pre_ffn, g_post_ffn = norms
    x2 = x.reshape(n, d)
    pr = _in_proj(x2, scale1, shift1, g_pre_mix, lw["w_in"], lw["b_small"], Tn, "sample")
    small = pr["small"]
    logf = small[:, N_GATE:N_GATE + FOX_H].reshape(Bs, Tn, FOX_H)
    pt_flat = page_table.reshape(-1)
    padq = ((0, 0), (0, 0), (0, 0), (0, 8 - Tn), (0, 0))

    qn = pr["qn"].reshape(Bs, Tn, NSA_G, HG, HD).transpose(0, 2, 3, 1, 4)
    qn = jnp.pad(qn, padq)
    z = jnp.zeros_like(qn[:, 0])
    qs = jnp.stack([jnp.concatenate([qn[:, 0], z], -1), jnp.concatenate([z, qn[:, 1]], -1)], axis=1)
    qs = qs.reshape(Bs, ROWS_S, LANES)
    slope_rows = jnp.broadcast_to(_alibi_slopes().reshape(NSA_H, 1, 1), (NSA_H, 8, 1)).reshape(ROWS_S, 1)
    qidx_rows = jnp.broadcast_to(jnp.arange(8, dtype=jnp.int32).reshape(1, 8, 1), (NSA_H, 8, 1)).reshape(ROWS_S, 1)
    gate_rows = small[:, :N_GATE].reshape(Bs, Tn, NSA_H, 3).transpose(0, 2, 1, 3)
    gate_rows = jnp.pad(gate_rows, ((0, 0), (0, 0), (0, 8 - Tn), (0, LANES - 3))).reshape(Bs, ROWS_S, LANES)

    tok_minor = lambda a: jnp.moveaxis(a, 1, -1)
    cmp_s = _compress_sample(tok_minor(cache_cmp), pt_flat, Bs, n_pages, lw["cmp_sw"])
    oc, sel_rows = _select_sample(qs, cmp_s, slope_rows, qidx_rows, Bs, past)
    o_s = _slc_sample(tok_minor(cache_slc), pt_flat, qs, sel_rows, pr["kvs"].reshape(Bs, Tn, KVW),
                      slope_rows, qidx_rows, Bs, n_pages, past)
    win = tok_minor(state_win)
    o_rows = _win_sample(qs, win, pr["kvw"].reshape(Bs, Tn, KVW), oc, o_s, gate_rows, slope_rows, qidx_rows, Bs)
    o_rows = o_rows.reshape(Bs, NSA_G, HG, 8, NSA_G, HD)[:, :, :, :Tn]
    o_n = jnp.stack([o_rows[:, 0, :, :, 0], o_rows[:, 1, :, :, 1]], axis=1)
    o_n = o_n.transpose(0, 3, 1, 2, 4).reshape(n, QW).astype(BF16)

    qf = jnp.pad(pr["qf"].reshape(Bs, Tn, FOX_H, HD).transpose(0, 2, 1, 3), ((0, 0), (0, 0), (0, 8 - Tn), (0, 0)))
    qf_bd = (qf[:, :, :, None, :] * jnp.eye(FOX_H, dtype=BF16)[None, :, None, :, None]).reshape(Bs, ROWS_F, FOX_K)
    page_cs = _page_cumsum(cache_logf.transpose(0, 2, 1))
    cs_new = jnp.cumsum(logf, axis=1).transpose(0, 2, 1)
    csnew_rows = jnp.broadcast_to(jnp.pad(cs_new, ((0, 0), (0, 0), (0, 8 - Tn)))[:, :, None, :],
                                 (Bs, FOX_H, 8, 8)).reshape(Bs, ROWS_F, 8)
    qidx_f = jnp.broadcast_to(jnp.arange(8, dtype=jnp.int32).reshape(1, 8, 1), (FOX_H, 8, 1)).reshape(ROWS_F, 1)
    o_frows = _fox_sample(tok_minor(cache_fox), page_cs, pt_flat, qf_bd,
                          pr["kvf"].reshape(Bs, Tn, KVF), csnew_rows, qidx_f, Bs, n_pages)
    o_frows = o_frows.reshape(Bs, FOX_H, 8, FOX_H, HD)[:, :, :Tn]
    o_f = jnp.stack([o_frows[:, h, :, h] for h in range(FOX_H)], axis=2)
    o_f = o_f.reshape(n, FOX_H * HD).astype(BF16)

    y = _ffn_tail(o_n, o_f, x2, (gate1, scale2, shift2, gate2), (g_post_mix, g_pre_ffn, g_post_ffn), lw, Tn)
    kvw_new = pr["kvw"].reshape(Bs, Tn, 2, NSA_G, HD)
    states = (pr["kvc"].reshape(Bs, Tn, 2, NSA_G, HD), pr["kvs"].reshape(Bs, Tn, 2, NSA_G, HD),
              pr["kvf"].reshape(Bs, Tn, 2, FOX_H, HD), logf,
              jnp.concatenate([state_win, kvw_new], axis=1)[:, Tn:])
    return y.reshape(Bs, Tn, d), states


def _layer_weights(w_in, b_nsa_gate, b_forget, w_cmp1, b_cmp1, w_cmp2, pe_cmp, w_out, w_router, b_router,
                   w_exp_gate, w_exp_up, w_exp_down, w_sh_gate, w_sh_up, w_sh_down):
    w_in_b, b_small = _in_weights(w_in, b_nsa_gate, b_forget)
    n_e = w_router.shape[1]
    wr = jnp.pad(w_router.T, ((0, LANES - n_e), (0, 0))).astype(BF16)
    br = jnp.concatenate([b_router.astype(F32), jnp.full((LANES - n_e,), NEG_INF, F32)]).reshape(LANES, 1)
    return dict(w_in=w_in_b, b_small=b_small,
                cmp_w=_cmp_weights(w_cmp1, b_cmp1, w_cmp2, pe_cmp),
                cmp_sw=_cmp_sample_weights(w_cmp1, b_cmp1, w_cmp2, pe_cmp),
                wo=w_out.astype(BF16), wr=wr, br=br,
                weg=w_exp_gate.astype(BF16), weu=w_exp_up.astype(BF16), wed=w_exp_down.astype(BF16),
                wsg=w_sh_gate.astype(BF16), wsu=w_sh_up.astype(BF16), wsd=w_sh_down.astype(BF16))


def kernel(x_prompt, x_sample, c_prompt, c_sample, cache_cmp_kv, cache_slc_kv, cache_fox_kv, cache_fox_logf,
           state_win_kv, page_table, w_ada, b_ada, g_pre_mix, g_post_mix, g_pre_ffn, g_post_ffn, w_in,
           b_nsa_gate, b_forget, w_cmp1, b_cmp1, w_cmp2, pe_cmp, w_out, w_router, b_router, w_exp_gate,
           w_exp_up, w_exp_down, w_sh_gate, w_sh_up, w_sh_down):
    depth = w_in.shape[0]
    Bp = x_prompt.shape[0]
    xp, xs = x_prompt, x_sample
    st_p = [[] for _ in range(5)]
    st_s = [[] for _ in range(5)]
    for l in range(depth):
        lw = _layer_weights(w_in[l], b_nsa_gate[l], b_forget[l], w_cmp1[l], b_cmp1[l], w_cmp2[l], pe_cmp[l],
                            w_out[l], w_router[l], b_router[l], w_exp_gate[l], w_exp_up[l], w_exp_down[l],
                            w_sh_gate[l], w_sh_up[l], w_sh_down[l])
        m = _modulation(jnp.concatenate([c_prompt, c_sample], axis=0), w_ada[l], b_ada[l])
        mods = jnp.split(m[:, None, :], 6, axis=-1)
        norms = (g_pre_mix[l], g_post_mix[l], g_pre_ffn[l], g_post_ffn[l])
        xp, sp = _prompt_layer(xp, [a[:Bp] for a in mods], norms, lw)
        caches = (cache_cmp_kv[l], cache_slc_kv[l], cache_fox_kv[l], cache_fox_logf[l], state_win_kv[l], page_table)
        xs, ss = _sample_layer(xs, [a[Bp:] for a in mods], norms, lw, caches)
        for lst, a in zip(st_p, sp):
            lst.append(a)
        for lst, a in zip(st_s, ss):
            lst.append(a)
    return (xp, xs, *[jnp.stack(s) for s in st_p], *[jnp.stack(s) for s in st_s])
```

```python
import functools

import numpy as np
import jax
import jax.numpy as jnp
from jax import lax
from jax.experimental import pallas as pl
from jax.experimental.pallas import tpu as pltpu

F32 = jnp.float32
BF16 = jnp.bfloat16

HD = 64
NSA_H = 8
NSA_G = 2
HG = NSA_H // NSA_G
FOX_H = 8
CMP_STRIDE = 16
CMP_LEN = 32
CMP_HID = 128
SLC_BLOCK = 64
SLC_TOPK = 16
WINDOW = 512
TOP_K = 8
ROUTED_SCALE = 2.5
RMS_EPS = 1e-6
NEG = -1e30
TINY = 1e-30
FORCE = 1e4
NEG_INF = float("-inf")

LANES = 128
KVW = 2 * NSA_G * HD
KVF = 2 * FOX_H * HD
QW = NSA_H * HD
VMEM_LIMIT = 56 * 1024 * 1024


def _cp(*sem):
    return pltpu.CompilerParams(dimension_semantics=sem, vmem_limit_bytes=VMEM_LIMIT)


def _rms(x, g):
    return x * lax.rsqrt(jnp.mean(x * x, axis=-1, keepdims=True) + RMS_EPS) * g


def _dot(a, b):
    return jnp.dot(a, b, preferred_element_type=F32)


def _dot_nt(a, b):
    return lax.dot_general(a, b, (((1,), (1,)), ((), ())), preferred_element_type=F32)


def _iota(shape, dim):
    return lax.broadcasted_iota(jnp.int32, shape, dim)


def _onehot(cond):
    return jnp.where(cond, 1.0, 0.0).astype(BF16)


def _pick_head(width_in, h_off):
    r = _iota((width_in, LANES), 0)
    c = _iota((width_in, LANES), 1)
    return _onehot((r == c + h_off) & (c < HD))


def _place_head(width_out, off):
    r = _iota((LANES, width_out), 0)
    c = _iota((LANES, width_out), 1)
    return _onehot((r >= HD) & (c == r - HD + off))


def _split3(x):
    hi = x.astype(BF16)
    r1 = x - hi.astype(F32)
    mid = r1.astype(BF16)
    lo = (r1 - mid.astype(F32)).astype(BF16)
    return hi, mid, lo


def _mod_kernel(c_ref, w_ref, b_ref, o_ref):
    c = c_ref[...]
    a = (c * jax.nn.sigmoid(c)).astype(BF16)
    o_ref[...] = _dot(a, w_ref[...].astype(BF16)) + b_ref[...]


def _modulation(c_all, w_ada, b_ada):
    nb, d = c_all.shape
    n_chunk = w_ada.shape[1] // d
    return pl.pallas_call(
        _mod_kernel,
        grid=(n_chunk,),
        in_specs=[pl.BlockSpec((nb, d), lambda i: (0, 0)),
                  pl.BlockSpec((d, d), lambda i: (0, i)),
                  pl.BlockSpec((1, d), lambda i: (0, i))],
        out_specs=pl.BlockSpec((nb, d), lambda i: (0, i)),
        out_shape=jax.ShapeDtypeStruct((nb, w_ada.shape[1]), F32),
        compiler_params=_cp("parallel"),
        name="modulation",
    )(c_all, w_ada, b_ada.reshape(1, -1))


N_GATE = 3 * NSA_H
SEL_TK = 256
LOG2E = 1.4426950408889634
_IN_BASE = (("kvc", KVW), ("kvs", KVW), ("kvw", KVW), ("kvf", KVF), ("small", LANES), ("qn", QW), ("qf", QW))
_IN_OUTS = {
    "prompt": (("kvc", "f32", KVW), ("small", "f32", LANES), ("qn", "bf16", QW), ("qf", "bf16", QW),
               ("ks_op", "bf16", NSA_G * LANES), ("vs_op", "bf16", NSA_G * LANES),
               ("kw_op", "bf16", NSA_G * LANES), ("vw_op", "bf16", NSA_G * LANES),
               ("kf_op", "bf16", FOX_H * LANES), ("vf_op", "bf16", FOX_H * LANES),
               ("kvc_t", "f32_t", KVW), ("kvs_t", "f32_t", KVW), ("kvw_t", "f32_t", KVW), ("kvf_t", "f32_t", KVF)),
    "sample": (("kvc", "f32", KVW), ("kvs", "f32", KVW), ("kvw", "f32", KVW), ("kvf", "f32", KVF),
               ("small", "f32", LANES), ("qn", "bf16", QW), ("qf", "bf16", QW)),
}


def _k_operand(kk, fill):
    lo = _iota(kk.shape, 1) < HD
    return jnp.concatenate([jnp.where(lo, kk, fill), jnp.where(lo, pltpu.roll(kk, HD, 1), fill)], axis=1)


def _v_operand(vv):
    lo = _iota(vv.shape, 1) < HD
    return jnp.concatenate([jnp.where(lo, 1.0, pltpu.roll(vv, HD, 1)), jnp.where(lo, 1.0, vv)], axis=1)


def _in_kernel(x_ref, sc_ref, sh_ref, g_ref, w_ref, bias_ref, *out_refs, outs, seq_len):
    x = x_ref[...]
    tm = x.shape[0]
    h = _rms(x, g_ref[...]) * (1.0 + sc_ref[...]) + sh_ref[...]
    hb = h.astype(BF16)
    base, off = {}, 0
    for name, width in _IN_BASE:
        base[name] = _dot(hb, w_ref[:, off:off + width])
        off += width
    z = base["small"] + bias_ref[...]
    logsig = jnp.minimum(z, 0.0) - jnp.log1p(jnp.exp(-jnp.abs(z)))
    base["small"] = jnp.where(_iota(z.shape, 1) < N_GATE, jax.nn.sigmoid(z), logsig)
    lane = _iota((tm, LANES), 1)
    nk, fk = NSA_G * HD, FOX_H * HD
    for (name, kind, _), o_ref in zip(outs, out_refs):
        if kind == "f32_t":
            o_ref[0] = base[name[:-2]].T
            continue
        if name == "ks_op":
            t = (pl.program_id(0) * tm) % seq_len + _iota((tm, LANES), 0)
            pos = jnp.where(lane == HD, (t & (SEL_TK - 1)).astype(F32), 0.0)
            val = _k_operand(base["kvs"][:, :nk], pos)
        elif name == "kw_op":
            val = _k_operand(base["kvw"][:, :nk], 0.0)
        elif name == "vs_op":
            val = _v_operand(base["kvs"][:, nk:])
        elif name == "vw_op":
            val = _v_operand(base["kvw"][:, nk:])
        elif name == "kf_op":
            kf = base["kvf"][:, :fk]
            val = jnp.concatenate(
                [jnp.where((lane < HD) == (i % 2 == 0), kf[:, (i // 2) * LANES:(i // 2 + 1) * LANES], 0.0)
                 for i in range(FOX_H)], axis=1)
        elif name == "vf_op":
            vf = base["kvf"][:, fk:]
            val = jnp.concatenate([_v_operand(vf[:, j * LANES:(j + 1) * LANES]) for j in range(FOX_H // 2)], axis=1)
        else:
            val = base[name]
        o_ref[...] = val.astype(o_ref.dtype)


def _in_weights(w_in, b_nsa_gate, b_forget):
    d = w_in.shape[0]
    o_qn, o_kvc, o_kvs, o_kvw = 0, QW, QW + KVW, QW + 2 * KVW
    o_gn = QW + 3 * KVW
    o_qf = o_gn + N_GATE
    o_kvf = o_qf + QW
    o_ff = o_kvf + KVF
    scale = HD ** -0.5
    cols = dict(
        kvc=w_in[:, o_kvc:o_kvc + KVW], kvs=w_in[:, o_kvs:o_kvs + KVW], kvw=w_in[:, o_kvw:o_kvw + KVW],
        kvf=w_in[:, o_kvf:o_kvf + KVF],
        small=jnp.concatenate([w_in[:, o_gn:o_gn + N_GATE], w_in[:, o_ff:o_ff + FOX_H],
                               jnp.zeros((d, LANES - N_GATE - FOX_H), F32)], axis=1),
        qn=w_in[:, o_qn:o_qn + QW] * scale, qf=w_in[:, o_qf:o_qf + QW] * (scale * LOG2E))
    w = jnp.concatenate([cols[name] for name, _ in _IN_BASE], axis=1).astype(BF16)
    bias = jnp.concatenate([b_nsa_gate, b_forget, jnp.zeros((LANES - N_GATE - FOX_H,), F32)]).reshape(1, LANES)
    return w, bias


def _in_proj(x2, scale, shift, g_pre, w, bias, rows_per_mod, group):
    n, d = x2.shape
    tm = min(256, n)
    outs = _IN_OUTS[group]
    if rows_per_mod >= tm:
        per = rows_per_mod // tm
        mod_spec = pl.BlockSpec((None, 1, d), lambda i: (i // per, 0, 0))
    else:
        assert not any(kind == "f32_t" for _, kind, _ in outs)
        per = 1
        scale = jnp.repeat(scale[:, 0, :], rows_per_mod, axis=0)
        shift = jnp.repeat(shift[:, 0, :], rows_per_mod, axis=0)
        mod_spec = pl.BlockSpec((tm, d), lambda i: (i, 0))
    n_seq = n // rows_per_mod
    out_specs, out_shape = [], []
    for _, kind, width in outs:
        if kind == "f32_t":
            out_specs.append(pl.BlockSpec((1, width, tm), lambda i: (i // per, 0, i % per)))
            out_shape.append(jax.ShapeDtypeStruct((n_seq, width, rows_per_mod), F32))
        else:
            out_specs.append(pl.BlockSpec((tm, width), lambda i: (i, 0)))
            out_shape.append(jax.ShapeDtypeStruct((n, width), F32 if kind == "f32" else BF16))
    full = lambda a: pl.BlockSpec(a.shape, lambda i: (0,) * a.ndim)
    res = pl.pallas_call(
        functools.partial(_in_kernel, outs=outs, seq_len=rows_per_mod),
        grid=(n // tm,),
        in_specs=[pl.BlockSpec((tm, d), lambda i: (i, 0)), mod_spec, mod_spec,
                  pl.BlockSpec((1, d), lambda i: (0, 0)), full(w), full(bias)],
        out_specs=out_specs,
        out_shape=out_shape,
        compiler_params=_cp("parallel"),
        name="in_proj_" + group,
    )(x2, scale, shift, g_pre.reshape(1, d), w, bias)
    return dict(zip([name for name, _, _ in outs], res))


def _cmp_kernel(subk_ref, subv_ref, w1_ref, pe_ref, b1_ref, w2_ref, o_ref):
    half = CMP_STRIDE * HD
    out = None
    for kv, sub_ref in enumerate((subk_ref, subv_ref)):
        sub = sub_ref[0, 0]
        w1 = w1_ref[kv]
        a = _dot(sub, w1[:half])
        b = _dot(sub, w1[half:])
        n_sub = a.shape[0]
        b_next = pltpu.roll(b, n_sub - 1, 0)
        bias = _dot(pe_ref[kv], w1)[0:1] + b1_ref[kv]
        hid = jax.nn.gelu(a + b_next + bias).astype(BF16)
        term = _dot(hid, w2_ref[kv])
        out = term if out is None else out + term
    o_ref[0, 0] = out.astype(o_ref.dtype)


def _cmp_weights(w_cmp1, b_cmp1, w_cmp2, pe_cmp):
    w1 = w_cmp1.astype(BF16)
    pe = jnp.broadcast_to(pe_cmp.reshape(2, 1, CMP_LEN * HD), (2, 8, CMP_LEN * HD)).astype(BF16)
    b1 = b_cmp1.reshape(2, 1, CMP_HID)
    z = jnp.zeros((CMP_HID, HD), F32)
    w2 = jnp.stack([jnp.concatenate([w_cmp2[0], z], axis=1),
                    jnp.concatenate([z, w_cmp2[1]], axis=1)]).astype(BF16)
    return w1, pe, b1, w2


def _compress_prompt(kvc, B, T, cw):
    w1, pe, b1, w2 = cw
    n_sub = T // CMP_STRIDE
    sub = kvc.reshape(B, n_sub, CMP_STRIDE, 2 * NSA_G, HD).transpose(0, 3, 1, 2, 4)
    sub = sub.reshape(B, 2 * NSA_G, n_sub, CMP_STRIDE * HD).astype(BF16)
    feat = CMP_STRIDE * HD
    return pl.pallas_call(
        _cmp_kernel,
        grid=(B, NSA_G),
        in_specs=[pl.BlockSpec((1, 1, n_sub, feat), lambda b, g: (b, g, 0, 0)),
                  pl.BlockSpec((1, 1, n_sub, feat), lambda b, g: (b, NSA_G + g, 0, 0)),
                  pl.BlockSpec(w1.shape, lambda b, g: (0, 0, 0)),
                  pl.BlockSpec(pe.shape, lambda b, g: (0, 0, 0)),
                  pl.BlockSpec(b1.shape, lambda b, g: (0, 0, 0)),
                  pl.BlockSpec(w2.shape, lambda b, g: (0, 0, 0))],
        out_specs=pl.BlockSpec((1, 1, n_sub, LANES), lambda b, g: (b, g, 0, 0)),
        out_shape=jax.ShapeDtypeStruct((B, NSA_G, n_sub, LANES), BF16),
        compiler_params=_cp("parallel", "parallel"),
        name="compress_prompt",
    )(sub, sub, w1, pe, b1, w2)


def _cumsum_block(x):
    n = x.shape[1]
    u = _onehot(_iota((n, n), 0) <= _iota((n, n), 1))
    r = x.shape[0]
    d3 = _dot(jnp.concatenate(_split3(x), axis=0), u)
    return d3[:r] + d3[r:2 * r] + d3[2 * r:]


def _cumsum_kernel(x_ref, o_ref, carry_ref):
    @pl.when(pl.program_id(1) == 0)
    def _():
        carry_ref[...] = jnp.zeros_like(carry_ref)

    d = _cumsum_block(x_ref[0]) + carry_ref[:, 0:1]
    o_ref[0] = d
    carry_ref[...] = jnp.broadcast_to(d[:, d.shape[1] - 1:], carry_ref.shape)


def _cumsum_prompt(logf_t):
    B, H, T = logf_t.shape
    tc = min(512, T)
    return pl.pallas_call(
        _cumsum_kernel,
        grid=(B, T // tc),
        in_specs=[pl.BlockSpec((1, H, tc), lambda b, c: (b, 0, c))],
        out_specs=pl.BlockSpec((1, H, tc), lambda b, c: (b, 0, c)),
        out_shape=jax.ShapeDtypeStruct((B, H, T), F32),
        scratch_shapes=[pltpu.VMEM((H, LANES), F32)],
        compiler_params=_cp("parallel", "arbitrary"),
        name="cumsum_prompt",
    )(logf_t)


MASK_BIG = 2.0 ** 30


def _topk_mask(val, n_take, axis):
    idx_f = _iota(val.shape, axis).astype(F32)
    sel = jnp.zeros(val.shape, F32)
    for _ in range(n_take):
        m = jnp.max(val, axis=axis, keepdims=True)
        first = jnp.min(jnp.where(val == m, idx_f, float(val.shape[axis])), axis=axis, keepdims=True)
        pick = idx_f == first
        sel = jnp.where(pick, 1.0, sel)
        val = jnp.where(pick, NEG_INF, val)
    return sel


def _masked_softmax(s, mask, axis=1):
    s = jnp.where(mask, s, NEG)
    m = jnp.max(s, axis=axis, keepdims=True)
    e = jnp.where(mask, jnp.exp(s - m), 0.0)
    return e / jnp.maximum(jnp.sum(e, axis=axis, keepdims=True), TINY)


def _flash_step(s, v, m_ref, acc_ref, base2):
    nk = s.shape[1]
    ex = jnp.exp2 if base2 else jnp.exp
    m_old = m_ref[...]
    m_new = jnp.maximum(m_old, jnp.max(s, axis=1, keepdims=True))
    alpha = ex(m_old - m_new)
    e = jnp.concatenate([ex(s[:, c * LANES:(c + 1) * LANES] - m_new) for c in range(nk // LANES)], axis=1)
    acc_ref[...] = alpha * acc_ref[...] + _dot(e.astype(BF16), v)
    m_ref[...] = m_new


def _hi_over_lo(acc):
    return acc / jnp.maximum(pltpu.roll(acc, HD, 1), TINY)


def _pair_heads(o_even, o_odd):
    lane = _iota(o_even.shape, 1)
    return jnp.where(lane < HD, pltpu.roll(o_even, HD, 1), o_odd)


def _cmp_to_slc_np(n_slots, first_token):
    tok = np.arange(n_slots)[:, None] + first_token
    start = tok * CMP_STRIDE
    bstart = np.arange(LANES)[None, :] * SLC_BLOCK
    m = (start < bstart + SLC_BLOCK) & (start + CMP_LEN > bstart) & (tok >= 0)
    return m.astype(np.float32)


def _nsa_sel_kernel(slopes_ref, q_ref, cmp_ref, cmpt_ref, c2st_ref, oc_ref, sel_ref, *, tq, n_sel, n_case):
    g = pl.program_id(1)
    t0 = pl.program_id(2) * tq
    n_slots = cmp_ref.shape[2]
    qpos = t0 + _iota((1, tq), 1)
    q = q_ref[...]

    def run(ns):
        cend = _iota((ns, 1), 0) * CMP_STRIDE + (CMP_LEN - 1)
        d_c = qpos - cend
        mask_c = d_c >= 0
        d_cf = d_c.astype(F32)
        cmp = cmp_ref[0, 0, :ns, :]
        cmpt = cmpt_ref[0, 0, :, :ns]
        nb = min(LANES, ns * CMP_STRIDE // SLC_BLOCK)
        c2st = c2st_ref[:nb, :ns]
        imp_t = jnp.zeros((nb, tq), F32)
        for h in range(HG):
            q128 = _dot(q, _pick_head(HG * HD, h * HD)).astype(BF16)
            s_t = _dot_nt(cmp, q128) - slopes_ref[g * HG + h] * d_cf
            p_t = _masked_softmax(s_t, mask_c, axis=0).astype(BF16)
            oc_ref[0, 0, h] = _dot(cmpt, p_t).T
            imp_t = imp_t + _dot(c2st, p_t)
        blk = _iota((nb, tq), 0)
        cur = qpos >> 6
        forced = (blk == 0) | (blk == cur) | (blk == cur - 1)
        valid = blk * SLC_BLOCK <= qpos
        val = jnp.where(valid, imp_t + jnp.where(forced, FORCE, 0.0), NEG)
        sel_t = jnp.where(valid, _topk_mask(val, n_sel, 0), 0.0)
        sel_t = jnp.concatenate([sel_t, jnp.zeros((LANES - nb, tq), F32)], axis=0) if nb < LANES else sel_t
        sel_ref[0, 0] = ((sel_t - 1.0) * MASK_BIG).T.astype(sel_ref.dtype)

    per = n_slots // n_case
    need = (t0 + tq + CMP_STRIDE - 1) // CMP_STRIDE
    case = jnp.minimum((need - 1) // per, n_case - 1)
    for k in range(n_case):
        pl.when(case == k)(functools.partial(run, (k + 1) * per))


NSA_TQ_SEL, NSA_TQ_ATT = 512, 256


def _nsa_select(slopes, qn, cmp, B, T):
    tq = min(NSA_TQ_SEL, T)
    nq = T // tq
    n_slots = cmp.shape[2]
    n_slc = -(-T // SLC_BLOCK)
    assert n_slc <= LANES and SLC_BLOCK == 64
    c2st = jnp.asarray(_cmp_to_slc_np(n_slots, 0).T, BF16)
    cmpt = jnp.swapaxes(cmp, 2, 3)
    return pl.pallas_call(
        functools.partial(_nsa_sel_kernel, tq=tq, n_sel=min(SLC_TOPK, n_slc),
                          n_case=4 if n_slots % (4 * LANES) == 0 else 1),
        grid=(B, NSA_G, nq),
        in_specs=[pl.BlockSpec(memory_space=pltpu.SMEM),
                  pl.BlockSpec((tq, HG * HD), lambda b, g, j: (b * nq + j, g)),
                  pl.BlockSpec((1, 1, n_slots, LANES), lambda b, g, j: (b, g, 0, 0)),
                  pl.BlockSpec((1, 1, LANES, n_slots), lambda b, g, j: (b, g, 0, 0)),
                  pl.BlockSpec((LANES, n_slots), lambda b, g, j: (0, 0))],
        out_specs=[pl.BlockSpec((1, 1, HG, tq, LANES), lambda b, g, j: (b, g, 0, j, 0)),
                   pl.BlockSpec((1, 1, tq, LANES), lambda b, g, j: (b, g, j, 0))],
        out_shape=[jax.ShapeDtypeStruct((B, NSA_G, HG, T, LANES), F32),
                   jax.ShapeDtypeStruct((B, NSA_G, T, LANES), BF16)],
        compiler_params=_cp("parallel", "parallel", "parallel"),
        name="nsa_select",
    )(slopes, qn, cmp, cmpt, c2st)


def _tile_bits(selneg, B, T, tq, tk):
    nq = T // tq
    n_tiles = T // tk
    bpt = tk // SLC_BLOCK
    assert n_tiles <= 32
    anyb = (selneg.reshape(B, NSA_G, nq, tq, LANES) == 0).any(axis=3)
    anyt = anyb[..., :n_tiles * bpt].reshape(B, NSA_G, nq, n_tiles, bpt).any(axis=-1)
    w = jnp.sum(anyt.astype(jnp.uint32) << jnp.arange(n_tiles, dtype=jnp.uint32), axis=-1, dtype=jnp.uint32)
    return lax.bitcast_convert_type(w, jnp.int32).reshape(-1)


def _nsa_attn_kernel(bits_ref, slopes_ref, q_ref, ks_ref, vs_ref, kw_ref, vw_ref, blk1h_ref, sel_ref, oc_ref,
                     gates_ref, o_ref, m_s, acc_s, *, tq, tk, wk):
    b, g, j = pl.program_id(0), pl.program_id(1), pl.program_id(2)
    t0 = j * tq
    q = q_ref[...]
    selneg = sel_ref[0, 0]
    lane = _iota((tq, LANES), 1)
    slopes = [slopes_ref[g * HG + h] for h in range(HG)]
    q_sel, q_win = [], []
    for h in range(HG):
        q128 = _dot(q, _pick_head(HG * HD, h * HD))
        q_win.append(q128.astype(BF16))
        q_sel.append(jnp.concatenate([jnp.where(lane == HD, slopes[h], q128).astype(BF16), selneg], axis=1))
    q4 = jnp.concatenate(q_sel, axis=0)
    qw4 = jnp.concatenate(q_win, axis=0)
    qpos = t0 + _iota((tq, 1), 0)

    m_s[...] = jnp.full(m_s.shape, NEG, F32)
    acc_s[...] = jnp.zeros(acc_s.shape, F32)
    word = bits_ref[(b * NSA_G + g) * pl.num_programs(2) + j]

    def tile(kt, diagonal):
        k0 = pl.multiple_of(kt * tk, tk)
        kop = jnp.concatenate([ks_ref[pl.ds(k0, tk), :], blk1h_ref[pl.ds(k0, tk), :]], axis=1)
        vop = vs_ref[pl.ds(k0, tk), :]
        s = _dot_nt(q4, kop)
        off = (k0 - t0).astype(F32)
        if diagonal:
            causal = (k0 + _iota((1, tk), 1)) <= qpos
        for h in range(HG):
            sh = s[h * tq:(h + 1) * tq] + slopes[h] * off
            if diagonal:
                sh = jnp.where(causal, sh, NEG)
            _flash_step(sh, vop, m_s.at[h], acc_s.at[h], False)

    def body(kt, carry):
        @pl.when(((word >> kt) & 1) == 1)
        def _():
            tile(kt, False)
        return carry

    n_before = t0 // tk
    lax.fori_loop(0, n_before, body, 0)
    for dd in range(max(1, tq // tk)):
        tile(n_before + dd, True)

    w0 = pl.multiple_of(jnp.maximum(t0 + tq - wk, 0), tq)
    kw = kw_ref[pl.ds(w0, wk), :]
    vw = vw_ref[pl.ds(w0, wk), :]
    d_w = qpos - (w0 + _iota((1, wk), 1))
    mask_w = (d_w >= 0) & (d_w < WINDOW)
    d_wf = d_w.astype(F32)
    s_w = _dot_nt(qw4, kw)
    gates = gates_ref[0]
    o_heads = []
    for h in range(HG):
        sw = jnp.where(mask_w, s_w[h * tq:(h + 1) * tq] - slopes[h] * d_wf, NEG)
        e_w = jnp.where(mask_w, jnp.exp(sw - jnp.max(sw, axis=1, keepdims=True)), 0.0)
        o_w = _hi_over_lo(_dot(e_w.astype(BF16), vw))
        o_s = _hi_over_lo(acc_s[h])
        o_heads.append(gates[:, 3 * h:3 * h + 1] * oc_ref[0, 0, h] + gates[:, 3 * h + 1:3 * h + 2] * o_s
                       + gates[:, 3 * h + 2:3 * h + 3] * o_w)
    o_ref[...] = jnp.concatenate([_pair_heads(o_heads[0], o_heads[1]), _pair_heads(o_heads[2], o_heads[3])],
                                 axis=1).astype(o_ref.dtype)


def _nsa_attend(slopes, qn, pr, selneg, oc, gates_g, B, T):
    tq = min(NSA_TQ_ATT, T)
    nq = T // tq
    tk = min(SEL_TK, T)
    assert tk == SEL_TK and HG == 4
    wk = min(WINDOW + tq, T)
    bits = _tile_bits(selneg, B, T, tq, tk)
    blk1h = jnp.asarray(np.arange(T)[:, None] // SLC_BLOCK == np.arange(LANES)[None, :], BF16)
    kv_spec = pl.BlockSpec((T, LANES), lambda b, g, j, bits: (b, g))
    return pl.pallas_call(
        functools.partial(_nsa_attn_kernel, tq=tq, tk=tk, wk=wk),
        grid_spec=pltpu.PrefetchScalarGridSpec(
            num_scalar_prefetch=1, grid=(B, NSA_G, nq),
            in_specs=[pl.BlockSpec(memory_space=pltpu.SMEM),
                      pl.BlockSpec((tq, HG * HD), lambda b, g, j, bits: (b * nq + j, g)),
                      kv_spec, kv_spec, kv_spec, kv_spec,
                      pl.BlockSpec((T, LANES), lambda b, g, j, bits: (0, 0)),
                      pl.BlockSpec((1, 1, tq, LANES), lambda b, g, j, bits: (b, g, j, 0)),
                      pl.BlockSpec((1, 1, HG, tq, LANES), lambda b, g, j, bits: (b, g, 0, j, 0)),
                      pl.BlockSpec((1, tq, LANES), lambda b, g, j, bits: (g, b * nq + j, 0))],
            out_specs=pl.BlockSpec((tq, HG * HD), lambda b, g, j, bits: (b * nq + j, g)),
            scratch_shapes=[pltpu.VMEM((HG, tq, LANES), F32), pltpu.VMEM((HG, tq, LANES), F32)]),
        out_shape=jax.ShapeDtypeStruct((B * T, QW), BF16),
        compiler_params=_cp("parallel", "parallel", "parallel"),
        name="nsa_attend",
    )(bits, slopes, qn, pr["ks_op"], pr["vs_op"], pr["kw_op"], pr["vw_op"], blk1h, selneg, oc, gates_g)


FOX_TQ, FOX_TK, FOX_ROWS = 1024, 512, 1024


def _fox_kernel(q_ref, k0_ref, k1_ref, v0_ref, v1_ref, d0_ref, d1_ref, o_ref, m_s, acc_s, *, tq, rows, tk):
    j = pl.program_id(2)
    n_part = tq // rows
    k_refs, v_refs, d_refs = (k0_ref, k1_ref), (v0_ref, v1_ref), (d0_ref, d1_ref)
    m_s[...] = jnp.full(m_s.shape, NEG, F32)
    acc_s[...] = jnp.zeros(acc_s.shape, F32)

    def step(kt, diagonal):
        k0 = pl.multiple_of(kt * tk, tk)
        for hh in range(2):
            k = k_refs[hh][pl.ds(k0, tk), :]
            v = v_refs[hh][pl.ds(k0, tk), :]
            bias = d_refs[hh][0, kt] * LOG2E
            for r in range(n_part):
                s = _dot_nt(q_ref[r * rows:(r + 1) * rows, :], k) - bias
                if diagonal:
                    qpos = j * tq + r * rows + _iota((rows, 1), 0)
                    s = jnp.where(k0 + _iota((1, tk), 1) <= qpos, s, NEG)
                _flash_step(s, v, m_s.at[hh * n_part + r], acc_s.at[hh * n_part + r], True)

    def body(kt, carry):
        step(kt, False)
        return carry

    n_full = (j * tq) // tk
    lax.fori_loop(0, n_full, body, 0)
    for dd in range(tq // tk):
        step(n_full + dd, True)
    for r in range(n_part):
        o_ref[r * rows:(r + 1) * rows, :] = _pair_heads(_hi_over_lo(acc_s[r]),
                                                         _hi_over_lo(acc_s[n_part + r])).astype(o_ref.dtype)


def _fox_attend(qf, kf_op, vf_op, d_t, B, T):
    tq = min(FOX_TQ, T)
    tk = min(FOX_TK, tq)
    rows = min(FOX_ROWS, tq)
    nq = T // tq
    d4 = d_t.reshape(B * FOX_H, T // tk, 1, tk)
    kv_spec = lambda hh: pl.BlockSpec((T, LANES), lambda b, hp, j: (b, 2 * hp + hh))
    d_spec = lambda hh: pl.BlockSpec((1, T // tk, 1, tk), lambda b, hp, j: (b * FOX_H + 2 * hp + hh, 0, 0, 0))
    n_chain = 2 * (tq // rows)
    return pl.pallas_call(
        functools.partial(_fox_kernel, tq=tq, rows=rows, tk=tk),
        grid=(B, FOX_H // 2, nq),
        in_specs=[pl.BlockSpec((tq, 2 * HD), lambda b, hp, j: (b * nq + j, hp)),
                  kv_spec(0), kv_spec(1), kv_spec(0), kv_spec(1), d_spec(0), d_spec(1)],
        out_specs=pl.BlockSpec((tq, 2 * HD), lambda b, hp, j: (b * nq + j, hp)),
        out_shape=jax.ShapeDtypeStruct((B * T, FOX_H * HD), BF16),
        scratch_shapes=[pltpu.VMEM((n_chain, rows, LANES), F32), pltpu.VMEM((n_chain, rows, LANES), F32)],
        compiler_params=_cp("parallel", "parallel", "parallel"),
        name="fox_attend",
    )(qf, kf_op, kf_op, vf_op, vf_op, d4, d4)


def _out_kernel(on_ref, of_ref, x_ref, g1_ref, sc2_ref, sh2_ref, gpm_ref, gpf_ref, wo_ref, wr_ref, br_ref,
                x1_ref, h2_ref, gd_ref):
    y = _dot(on_ref[...], wo_ref[:QW]) + _dot(of_ref[...], wo_ref[QW:])
    x1 = x_ref[...] + g1_ref[...] * _rms(y, gpm_ref[...])
    x1_ref[...] = x1
    h2 = (_rms(x1, gpf_ref[...]) * (1.0 + sc2_ref[...]) + sh2_ref[...]).astype(BF16)
    h2_ref[...] = h2
    scores_t = jax.nn.sigmoid(_dot_nt(wr_ref[...], h2))
    sel_t = _topk_mask(scores_t + br_ref[...], TOP_K, 0)
    g_t = sel_t * scores_t
    gd_ref[...] = (g_t / jnp.sum(g_t, axis=0, keepdims=True) * ROUTED_SCALE).T


def _out_proj(o_n, o_f, x2, gate1, scale2, shift2, g_post_mix, g_pre_ffn, wo, wr, br, rows_per_mod):
    n, d = x2.shape
    tm = min(256, n)
    if rows_per_mod >= tm:
        per = rows_per_mod // tm
        mod_spec = pl.BlockSpec((None, 1, d), lambda i: (i // per, 0, 0))
    else:
        gate1, scale2, shift2 = (jnp.repeat(a[:, 0, :], rows_per_mod, axis=0) for a in (gate1, scale2, shift2))
        mod_spec = pl.BlockSpec((tm, d), lambda i: (i, 0))
    row = lambda w: pl.BlockSpec((tm, w), lambda i: (i, 0))
    full = lambda a: pl.BlockSpec(a.shape, lambda i: (0,) * a.ndim)
    gpm, gpf = g_post_mix.reshape(1, d), g_pre_ffn.reshape(1, d)
    return pl.pallas_call(
        _out_kernel,
        grid=(n // tm,),
        in_specs=[row(QW), row(QW), row(d), mod_spec, mod_spec, mod_spec, full(gpm), full(gpf),
                  full(wo), full(wr), full(br)],
        out_specs=[row(d), row(d), row(LANES)],
        out_shape=[jax.ShapeDtypeStruct((n, d), F32), jax.ShapeDtypeStruct((n, d), BF16),
                   jax.ShapeDtypeStruct((n, LANES), F32)],
        compiler_params=_cp("parallel"),
        name="out_proj_router",
    )(o_n, o_f, x2, gate1, scale2, shift2, gpm, gpf, wo, wr, br)


MOE_EB, MOE_TM = 8, 1024


def _moe_kernel(h_ref, gd_ref, wg_ref, wu_ref, wd_ref, o_ref, *, eb):
    blk = pl.program_id(1)

    @pl.when(blk == 0)
    def _():
        o_ref[...] = jnp.zeros_like(o_ref)

    h = h_ref[...]
    f = wg_ref.shape[2]
    gd = pltpu.roll(gd_ref[...], (-(blk * eb)) & (LANES - 1), 1)
    parts = []
    for i in range(eb):
        a = _dot(h, wg_ref[i])
        parts.append((a * jax.nn.sigmoid(a) * _dot(h, wu_ref[i]) * gd[:, i:i + 1]).astype(BF16))
    hb = jnp.concatenate(parts, axis=1)
    o_ref[...] += _dot(hb, wd_ref[...].reshape(eb * f, wd_ref.shape[2]))


def _moe_routed(h2, gd, wg, wu, wd):
    n, d = h2.shape
    n_e, _, f = wg.shape
    tm = min(MOE_TM, n)
    eb = MOE_EB
    assert n_e % eb == 0
    return pl.pallas_call(
        functools.partial(_moe_kernel, eb=eb),
        grid=(n // tm, n_e // eb),
        in_specs=[pl.BlockSpec((tm, d), lambda i, e: (i, 0)),
                  pl.BlockSpec((tm, LANES), lambda i, e: (i, 0)),
                  pl.BlockSpec((eb, d, f), lambda i, e: (e, 0, 0)),
                  pl.BlockSpec((eb, d, f), lambda i, e: (e, 0, 0)),
                  pl.BlockSpec((eb, f, d), lambda i, e: (e, 0, 0))],
        out_specs=pl.BlockSpec((tm, d), lambda i, e: (i, 0)),
        out_shape=jax.ShapeDtypeStruct((n, d), F32),
        compiler_params=_cp("parallel", "arbitrary"),
        name="moe_routed",
    )(h2, gd, wg, wu, wd)


def _final_kernel(r_ref, h_ref, x1_ref, g2_ref, gpf_ref, wg_ref, wu_ref, wd_ref, y_ref):
    h = h_ref[...]
    a = _dot(h, wg_ref[...])
    hb = (a * jax.nn.sigmoid(a) * _dot(h, wu_ref[...])).astype(BF16)
    f = r_ref[...] + _dot(hb, wd_ref[...])
    y_ref[...] = x1_ref[...] + g2_ref[...] * _rms(f, gpf_ref[...])


def _final(routed, h2, x1, gate2, g_post_ffn, wsg, wsu, wsd, rows_per_mod):
    n, d = x1.shape
    tm = min(512, n)
    if rows_per_mod >= tm:
        per = rows_per_mod // tm
        mod_spec = pl.BlockSpec((None, 1, d), lambda i: (i // per, 0, 0))
    else:
        gate2 = jnp.repeat(gate2[:, 0, :], rows_per_mod, axis=0)
        mod_spec = pl.BlockSpec((tm, d), lambda i: (i, 0))
    row = pl.BlockSpec((tm, d), lambda i: (i, 0))
    full = lambda a: pl.BlockSpec(a.shape, lambda i: (0,) * a.ndim)
    gpf = g_post_ffn.reshape(1, d)
    return pl.pallas_call(
        _final_kernel,
        grid=(n // tm,),
        in_specs=[row, row, row, mod_spec, full(gpf), full(wsg), full(wsu), full(wsd)],
        out_specs=row,
        out_shape=jax.ShapeDtypeStruct((n, d), F32),
        compiler_params=_cp("parallel"),
        name="shared_ffn_final",
    )(routed, h2, x1, gate2, gpf, wsg, wsu, wsd)


PAGES_PER_STEP = 16


def _page_specs(block, n_pages, pp):
    def spec(jj):
        return pl.BlockSpec(block, lambda b, c, pt: (pt[b * n_pages + c * pp + jj],) + (0,) * (len(block) - 1))
    return [spec(jj) for jj in range(pp)]


def _cmp_sample_kernel(pt_ref, *refs, pp):
    pages = refs[:pp]
    w1_ref, bias_ref, w2_ref, o_ref, carry_ref, x_s = refs[pp:]
    c = pl.program_id(1)
    page = pages[0].shape[-1]
    rows = pp * page // CMP_STRIDE

    @pl.when(c == 0)
    def _():
        carry_ref[...] = jnp.zeros_like(carry_ref)

    for jj, p in enumerate(pages):
        for kv in range(2):
            x_s[kv, jj * page:(jj + 1) * page, :] = p[0, kv].reshape(NSA_G * HD, page).T

    acc = []
    for kv in range(2):
        taps = [x_s[kv, pl.ds(t, rows, stride=CMP_STRIDE), :].astype(BF16) for t in range(CMP_STRIDE)]
        acc.append(_dot(jnp.concatenate(taps, axis=1), w1_ref[kv]))
    out = [None, None]
    for kv in range(2):
        a = acc[kv][:, :2 * CMP_HID]
        b = acc[kv][:, 2 * CMP_HID:]
        a_prev = pltpu.roll(a, 1, 0)
        a_prev = jnp.where(_iota(a.shape, 0) == 0, carry_ref[kv][0:1], a_prev)
        carry_ref[kv] = jnp.broadcast_to(a[rows - 1:rows], carry_ref.shape[1:])
        hid = jax.nn.gelu(a_prev + b + bias_ref[kv]).astype(BF16)
        for g in range(NSA_G):
            term = _dot(hid[:, g * CMP_HID:(g + 1) * CMP_HID], w2_ref[kv, g])
            out[g] = term if out[g] is None else out[g] + term
    for g in range(NSA_G):
        o_ref[0, g] = out[g].astype(o_ref.dtype)


def _cmp_sample_weights(w_cmp1, b_cmp1, w_cmp2, pe_cmp):
    w1 = w_cmp1.reshape(2, 2, CMP_STRIDE, HD, CMP_HID)
    z = jnp.zeros_like(w1[:, 0])
    top = jnp.concatenate([w1[:, 0], z, w1[:, 1], z], axis=-1)
    bot = jnp.concatenate([z, w1[:, 0], z, w1[:, 1]], axis=-1)
    w1bd = jnp.concatenate([top, bot], axis=2).astype(BF16)
    w1bd = w1bd.reshape(2, CMP_STRIDE * LANES, 4 * CMP_HID)
    bias = (jnp.einsum("kf,kfh->kh", pe_cmp.reshape(2, -1).astype(BF16), w_cmp1.astype(BF16),
                       preferred_element_type=F32) + b_cmp1)
    bias = jnp.concatenate([bias, bias], axis=-1).reshape(2, 1, 2 * CMP_HID)
    z2 = jnp.zeros((CMP_HID, HD), F32)
    lo = lambda w: jnp.concatenate([w, z2], axis=1)
    hi = lambda w: jnp.concatenate([z2, w], axis=1)
    w2 = jnp.stack([jnp.stack([lo(w_cmp2[0]), hi(w_cmp2[0])]),
                    jnp.stack([hi(w_cmp2[1]), lo(w_cmp2[1])])]).astype(BF16)
    return w1bd, bias, w2


def _compress_sample(cache, pt_flat, Bs, n_pages, csw):
    w1bd, bias, w2 = csw
    pp = min(2 * PAGES_PER_STEP, n_pages)
    page = cache.shape[-1]
    rows = pp * (page // CMP_STRIDE)
    n_slots = n_pages * (page // CMP_STRIDE)
    full = lambda a: pl.BlockSpec(a.shape, lambda b, c, pt: (0,) * a.ndim)
    return pl.pallas_call(
        functools.partial(_cmp_sample_kernel, pp=pp),
        grid_spec=pltpu.PrefetchScalarGridSpec(
            num_scalar_prefetch=1, grid=(Bs, n_pages // pp),
            in_specs=_page_specs((1,) + cache.shape[1:], n_pages, pp) + [full(w1bd), full(bias), full(w2)],
            out_specs=pl.BlockSpec((1, NSA_G, rows, LANES), lambda b, c, pt: (b, 0, c, 0)),
            scratch_shapes=[pltpu.VMEM((2, 8, 2 * CMP_HID), F32), pltpu.VMEM((2, pp * page, LANES), F32)]),
        out_shape=jax.ShapeDtypeStruct((Bs, NSA_G, n_slots, LANES), BF16),
        compiler_params=_cp("parallel", "arbitrary"),
        name="compress_sample",
    )(pt_flat, *([cache] * pp), w1bd, bias, w2)


def _page_cumsum_kernel(x_ref, o_ref):
    o_ref[...] = _cumsum_block(x_ref[...])


def _page_cumsum(logf_cache_t):
    n_phys, h, page = logf_cache_t.shape
    rows = n_phys * h
    tr = next(t for t in (2048, 1024, 512, 256, 128, 64, 32, 16, 8) if rows % t == 0)
    out = pl.pallas_call(
        _page_cumsum_kernel,
        grid=(rows // tr,),
        in_specs=[pl.BlockSpec((tr, page), lambda i: (i, 0))],
        out_specs=pl.BlockSpec((tr, page), lambda i: (i, 0)),
        out_shape=jax.ShapeDtypeStruct((rows, page), F32),
        compiler_params=_cp("parallel"),
        name="page_cumsum",
    )(logf_cache_t.reshape(rows, page))
    return out.reshape(n_phys, h, page)


ROWS_S = NSA_G * HG * 8


SEL_SEQS = 8


def _sel_sample_kernel(q_ref, cmp_ref, c2s_ref, slope_ref, qidx_ref, oc_ref, sel_ref, *, past, n_take):
    n_slots = cmp_ref.shape[2]
    rg = ROWS_S // NSA_G
    slot = _iota((1, n_slots), 1)
    cend = slot * CMP_STRIDE + (CMP_STRIDE - 1)
    c2s = c2s_ref[...]
    blk = _iota((8, LANES), 1)
    n_blk = past // SLC_BLOCK
    forced = (blk == 0) | (blk == n_blk - 1)
    for sq in range(q_ref.shape[0]):
        for g in range(NSA_G):
            r = slice(g * rg, (g + 1) * rg)
            cmp = cmp_ref[sq, g]
            d_c = past + qidx_ref[r] - cend
            mask = (d_c >= 0) & (slot >= 1)
            s = _dot_nt(q_ref[sq, r], cmp) - slope_ref[r] * d_c.astype(F32)
            pb = _masked_softmax(s, mask).astype(BF16)
            oc_ref[sq, r] = pltpu.roll(_dot(pb, cmp), HD, 1)
            imp_h = _dot(pb, c2s)
            imp = imp_h[0:8] + imp_h[8:16] + imp_h[16:24] + imp_h[24:32]
            val = jnp.where(blk < n_blk, imp + jnp.where(forced, FORCE, 0.0), NEG)
            sel8 = _topk_mask(val, n_take, 1)
            sel_ref[sq, r] = jnp.concatenate([sel8] * HG, axis=0).astype(sel_ref.dtype)


def _select_sample(qs, cmp_s, slope_rows, qidx_rows, Bs, past):
    n_slots = cmp_s.shape[2]
    n_slc = past // SLC_BLOCK + 1
    assert past % SLC_BLOCK == 0 and n_slc - 1 <= LANES and HG == 4
    nb = SEL_SEQS if Bs % SEL_SEQS == 0 else 1
    c2s = jnp.asarray(_cmp_to_slc_np(n_slots, -1), BF16)
    full = lambda a: pl.BlockSpec(a.shape, lambda b: (0,) * a.ndim)
    return pl.pallas_call(
        functools.partial(_sel_sample_kernel, past=past, n_take=min(SLC_TOPK, n_slc) - 1),
        grid=(Bs // nb,),
        in_specs=[pl.BlockSpec((nb, ROWS_S, LANES), lambda b: (b, 0, 0)),
                  pl.BlockSpec((nb, NSA_G, n_slots, LANES), lambda b: (b, 0, 0, 0)),
                  full(c2s), full(slope_rows), full(qidx_rows)],
        out_specs=[pl.BlockSpec((nb, ROWS_S, LANES), lambda b: (b, 0, 0)),
                   pl.BlockSpec((nb, ROWS_S, LANES), lambda b: (b, 0, 0))],
        out_shape=[jax.ShapeDtypeStruct((Bs, ROWS_S, LANES), F32),
                   jax.ShapeDtypeStruct((Bs, ROWS_S, LANES), BF16)],
        compiler_params=_cp("parallel"),
        name="select_sample",
    )(qs, cmp_s, c2s, slope_rows, qidx_rows)


def _online_update(s, ok, v, m_s, l_s, acc_s, base2=False, v_t=False):
    ex = jnp.exp2 if base2 else jnp.exp
    s = jnp.where(ok, s, NEG)
    m_old = m_s[...]
    m_new = jnp.maximum(m_old, jnp.max(s, axis=1, keepdims=True))
    alpha = ex(m_old - m_new)
    e = jnp.where(ok, ex(s - m_new), 0.0)
    l_s[...] = alpha * l_s[...] + jnp.sum(e, axis=1, keepdims=True)
    acc_s[...] = alpha * acc_s[...] + (_dot_nt if v_t else _dot)(e.astype(BF16), v)
    m_s[...] = m_new


def _pages_t(pages, kv, width):
    return jnp.concatenate([p[0, kv].reshape(width, p.shape[-1]).astype(BF16) for p in pages], axis=1)


def _pad_rows8(x):
    return jnp.concatenate([x, jnp.zeros((8 - x.shape[0], x.shape[1]), x.dtype)], axis=0)


def _slc_sample_kernel(pt_ref, *refs, pp, past, t_new):
    pages = refs[:pp]
    q_ref, sel_ref, new_ref, slope_ref, qidx_ref, o_ref, m_s, l_s, acc_s = refs[pp:]
    c = pl.program_id(1)
    page = pages[0].shape[-1]
    nk = pp * page

    @pl.when(c == 0)
    def _():
        m_s[...] = jnp.full(m_s.shape, NEG, F32)
        l_s[...] = jnp.zeros(l_s.shape, F32)
        acc_s[...] = jnp.zeros(acc_s.shape, F32)

    q = q_ref[0]
    slope = slope_ref[...]
    qidx = qidx_ref[...]
    sel = sel_ref[0]
    blk_lane = _iota(sel.shape, 1)
    b0 = c * (nk // SLC_BLOCK)
    in_chunk = (blk_lane >= b0) & (blk_lane < b0 + nk // SLC_BLOCK)

    @pl.when(jnp.max(jnp.where(in_chunk, sel.astype(F32), 0.0)) > 0.0)
    def _():
        kpos = c * nk + _iota((1, nk), 1)
        expand = _onehot(_iota((LANES, 1), 0) == (kpos >> 6))
        ok = _dot(sel, expand) > 0.5
        dist = (past + qidx - kpos).astype(F32)
        s = _dot(q, _pages_t(pages, 0, NSA_G * HD)) - slope * dist
        _online_update(s, ok, _pages_t(pages, 1, NSA_G * HD), m_s, l_s, acc_s, v_t=True)

    @pl.when(c == pl.num_programs(1) - 1)
    def _():
        new = _pad_rows8(new_ref[0])
        kj = _iota((1, 8), 1)
        ok_new = (kj <= qidx) & (kj < t_new)
        s_new = _dot_nt(q, new[:, :LANES].astype(BF16)) - slope * (qidx - kj).astype(F32)
        _online_update(s_new, ok_new, new[:, LANES:].astype(BF16), m_s, l_s, acc_s)
        o_ref[0] = acc_s[...] / l_s[...]


def _slc_sample(cache, pt_flat, qs, sel_rows, kvs_new, slope_rows, qidx_rows, Bs, n_pages, past):
    pp = min(PAGES_PER_STEP, n_pages)
    t_new = kvs_new.shape[1]
    full = lambda a: pl.BlockSpec(a.shape, lambda b, c, pt: (0,) * a.ndim)
    per_seq = lambda a: pl.BlockSpec((1,) + a.shape[1:], lambda b, c, pt: (b,) + (0,) * (a.ndim - 1))
    return pl.pallas_call(
        functools.partial(_slc_sample_kernel, pp=pp, past=past, t_new=t_new),
        grid_spec=pltpu.PrefetchScalarGridSpec(
            num_scalar_prefetch=1, grid=(Bs, n_pages // pp),
            in_specs=_page_specs((1,) + cache.shape[1:], n_pages, pp)
            + [per_seq(qs), per_seq(sel_rows), per_seq(kvs_new), full(slope_rows), full(qidx_rows)],
            out_specs=pl.BlockSpec((1, ROWS_S, LANES), lambda b, c, pt: (b, 0, 0)),
            scratch_shapes=[pltpu.VMEM((ROWS_S, 1), F32), pltpu.VMEM((ROWS_S, 1), F32),
                            pltpu.VMEM((ROWS_S, LANES), F32)]),
        out_shape=jax.ShapeDtypeStruct((Bs, ROWS_S, LANES), F32),
        compiler_params=_cp("parallel", "arbitrary"),
        name="selected_sample",
    )(pt_flat, *([cache] * pp), qs, sel_rows, kvs_new, slope_rows, qidx_rows)


def _win_sample_kernel(q_ref, win_ref, new_ref, oc_ref, os_ref, gates_ref, slope_ref, qidx_ref, o_ref, *, t_new):
    q = q_ref[0]
    n_win = win_ref.shape[-1]
    k_t = win_ref[0, 0].reshape(NSA_G * HD, n_win).astype(BF16)
    v_t = win_ref[0, 1].reshape(NSA_G * HD, n_win).astype(BF16)
    new = _pad_rows8(new_ref[0])
    kj = _iota((1, n_win + 8), 1)
    d_w = n_win + qidx_ref[...] - kj
    mask = (d_w >= 0) & (d_w < WINDOW) & (kj < n_win + t_new)
    s = jnp.concatenate([_dot(q, k_t), _dot_nt(q, new[:, :LANES].astype(BF16))], axis=1)
    p = _masked_softmax(s - slope_ref[...] * d_w.astype(F32), mask).astype(BF16)
    o_w = _dot_nt(p[:, :n_win], v_t) + _dot(p[:, n_win:], new[:, LANES:].astype(BF16))
    gates = gates_ref[0]
    o_ref[0] = gates[:, 0:1] * oc_ref[0] + gates[:, 1:2] * os_ref[0] + gates[:, 2:3] * o_w


def _win_sample(qs, state_win, kvw_new, oc, o_s, gate_rows, slope_rows, qidx_rows, Bs):
    t_new = kvw_new.shape[1]
    full = lambda a: pl.BlockSpec(a.shape, lambda b: (0,) * a.ndim)
    per_seq = lambda a: pl.BlockSpec((1,) + a.shape[1:], lambda b: (b,) + (0,) * (a.ndim - 1))
    return pl.pallas_call(
        functools.partial(_win_sample_kernel, t_new=t_new),
        grid=(Bs,),
        in_specs=[per_seq(qs), per_seq(state_win), per_seq(kvw_new), per_seq(oc), per_seq(o_s),
                  per_seq(gate_rows), full(slope_rows), full(qidx_rows)],
        out_specs=pl.BlockSpec((1, ROWS_S, LANES), lambda b: (b, 0, 0)),
        out_shape=jax.ShapeDtypeStruct((Bs, ROWS_S, LANES), F32),
        compiler_params=_cp("parallel"),
        name="window_sample",
    )(qs, state_win, kvw_new, oc, o_s, gate_rows, slope_rows, qidx_rows)


ROWS_F = FOX_H * 8
FOX_K = FOX_H * HD


def _fox_sample_kernel(pt_ref, *refs, pp, t_new):
    pages, pcs = refs[:pp], refs[pp:2 * pp]
    q_ref, new_ref, csnew_ref, qidx_ref, o_ref, m_s, l_s, acc_s, carry_s = refs[2 * pp:]
    c = pl.program_id(1)

    @pl.when(c == 0)
    def _():
        m_s[...] = jnp.full(m_s.shape, NEG, F32)
        l_s[...] = jnp.zeros(l_s.shape, F32)
        acc_s[...] = jnp.zeros(acc_s.shape, F32)
        carry_s[...] = jnp.zeros(carry_s.shape, F32)

    q = q_ref[0]
    carry = carry_s[...]
    chunks = []
    for pc in pcs:
        x = pc[0]
        chunks.append(x + carry)
        carry = carry + jnp.broadcast_to(x[:, x.shape[1] - 1:], carry.shape)
    carry_s[...] = carry
    d = jnp.concatenate(chunks, axis=1)
    nk = d.shape[1]
    bias = jnp.concatenate([jnp.broadcast_to(d[h:h + 1], (8, nk)) for h in range(FOX_H)], axis=0)
    s = _dot(q, _pages_t(pages, 0, FOX_K)) - bias * LOG2E
    _online_update(s, jnp.full((ROWS_F, nk), True), _pages_t(pages, 1, FOX_K), m_s, l_s, acc_s,
                   base2=True, v_t=True)

    @pl.when(c == pl.num_programs(1) - 1)
    def _():
        new = _pad_rows8(new_ref[0])
        kj = _iota((1, 8), 1)
        ok_new = (kj <= qidx_ref[...]) & (kj < t_new)
        total = jnp.concatenate([jnp.broadcast_to(carry[h:h + 1, 0:1], (8, 1)) for h in range(FOX_H)], axis=0)
        s_new = _dot_nt(q, new[:, :FOX_K].astype(BF16)) - (total + csnew_ref[0]) * LOG2E
        _online_update(s_new, ok_new, new[:, FOX_K:].astype(BF16), m_s, l_s, acc_s, base2=True)
        o_ref[0] = acc_s[...] / l_s[...]


def _fox_sample(cache, page_cs, pt_flat, qf_bd, kvf_new, csnew_rows, qidx_f, Bs, n_pages):
    pp = min(PAGES_PER_STEP, n_pages)
    page = cache.shape[-1]
    t_new = kvf_new.shape[1]
    full = lambda a: pl.BlockSpec(a.shape, lambda b, c, pt: (0,) * a.ndim)
    per_seq = lambda a: pl.BlockSpec((1,) + a.shape[1:], lambda b, c, pt: (b,) + (0,) * (a.ndim - 1))
    return pl.pallas_call(
        functools.partial(_fox_sample_kernel, pp=pp, t_new=t_new),
        grid_spec=pltpu.PrefetchScalarGridSpec(
            num_scalar_prefetch=1, grid=(Bs, n_pages // pp),
            in_specs=_page_specs((1,) + cache.shape[1:], n_pages, pp)
            + _page_specs((1,) + page_cs.shape[1:], n_pages, pp)
            + [per_seq(qf_bd), per_seq(kvf_new), per_seq(csnew_rows), full(qidx_f)],
            out_specs=pl.BlockSpec((1, ROWS_F, FOX_K), lambda b, c, pt: (b, 0, 0)),
            scratch_shapes=[pltpu.VMEM((ROWS_F, 1), F32), pltpu.VMEM((ROWS_F, 1), F32),
                            pltpu.VMEM((ROWS_F, FOX_K), F32), pltpu.VMEM((FOX_H, LANES), F32)]),
        out_shape=jax.ShapeDtypeStruct((Bs, ROWS_F, FOX_K), F32),
        compiler_params=_cp("parallel", "arbitrary"),
        name="fox_sample",
    )(pt_flat, *([cache] * pp), *([page_cs] * pp), qf_bd, kvf_new, csnew_rows, qidx_f)


def _alibi_slopes():
    return jnp.exp2(-8.0 * jnp.arange(1, NSA_H + 1, dtype=F32) / NSA_H)


def _ffn_tail(o_n, o_f, x2, mod, norms, lw, rows_per_mod):
    gate1, scale2, shift2, gate2 = mod
    g_post_mix, g_pre_ffn, g_post_ffn = norms
    x1, h2, gd = _out_proj(o_n, o_f, x2, gate1, scale2, shift2, g_post_mix, g_pre_ffn,
                           lw["wo"], lw["wr"], lw["br"], rows_per_mod)
    routed = _moe_routed(h2, gd, lw["weg"], lw["weu"], lw["wed"])
    return _final(routed, h2, x1, gate2, g_post_ffn, lw["wsg"], lw["wsu"], lw["wsd"], rows_per_mod)


def _prompt_layer(x, mod, norms, lw):
    B, T, d = x.shape
    n = B * T
    shift1, scale1, gate1, shift2, scale2, gate2 = mod
    g_pre_mix, g_post_mix, g_pre_ffn, g_post_ffn = norms
    x2 = x.reshape(n, d)
    pr = _in_proj(x2, scale1, shift1, g_pre_mix, lw["w_in"], lw["b_small"], T, "prompt")
    small = pr["small"]
    logf = small[:, N_GATE:N_GATE + FOX_H]
    gates_g = small[:, :N_GATE].reshape(n, NSA_G, HG * 3).transpose(1, 0, 2)
    gates_g = jnp.pad(gates_g, ((0, 0), (0, 0), (0, LANES - HG * 3)))
    slopes = _alibi_slopes()

    cmp = _compress_prompt(pr["kvc"], B, T, lw["cmp_w"])
    oc, selneg = _nsa_select(slopes, pr["qn"], cmp, B, T)
    o_n = _nsa_attend(slopes, pr["qn"], pr, selneg, oc, gates_g, B, T)

    d_t = _cumsum_prompt(logf.reshape(B, T, FOX_H).transpose(0, 2, 1))
    o_f = _fox_attend(pr["qf"], pr["kf_op"], pr["vf_op"], d_t, B, T)

    y = _ffn_tail(o_n, o_f, x2, (gate1, scale2, shift2, gate2), (g_post_mix, g_pre_ffn, g_post_ffn), lw, T)
    win_rows = min(WINDOW, T)
    state = lambda a, heads: jnp.moveaxis(a.reshape(B, 2, heads, HD, -1), -1, 1)
    states = (state(pr["kvc_t"], NSA_G), state(pr["kvs_t"], NSA_G), state(pr["kvf_t"], FOX_H),
              logf.reshape(B, T, FOX_H), state(pr["kvw_t"][:, :, T - win_rows:], NSA_G))
    return y.reshape(B, T, d), states


def _sample_layer(x, mod, norms, lw, caches):
    Bs, Tn, d = x.shape
    n = Bs * Tn
    assert Tn <= 8
    cache_cmp, cache_slc, cache_fox, cache_logf, state_win, page_table = caches
    n_pages = page_table.shape[1]
    page = cache_cmp.shape[1]
    past = n_pages * page
    shift1, scale1, gate1, shift2, scale2, gate2 = mod
    g_pre_mix, g_post_mix, g_pre_ffn, g_post_ffn = norms
    x2 = x.reshape(n, d)
    pr = _in_proj(x2, scale1, shift1, g_pre_mix, lw["w_in"], lw["b_small"], Tn, "sample")
    small = pr["small"]
    logf = small[:, N_GATE:N_GATE + FOX_H].reshape(Bs, Tn, FOX_H)
    pt_flat = page_table.reshape(-1)
    padq = ((0, 0), (0, 0), (0, 0), (0, 8 - Tn), (0, 0))

    qn = pr["qn"].reshape(Bs, Tn, NSA_G, HG, HD).transpose(0, 2, 3, 1, 4)
    qn = jnp.pad(qn, padq)
    z = jnp.zeros_like(qn[:, 0])
    qs = jnp.stack([jnp.concatenate([qn[:, 0], z], -1), jnp.concatenate([z, qn[:, 1]], -1)], axis=1)
    qs = qs.reshape(Bs, ROWS_S, LANES)
    slope_rows = jnp.broadcast_to(_alibi_slopes().reshape(NSA_H, 1, 1), (NSA_H, 8, 1)).reshape(ROWS_S, 1)
    qidx_rows = jnp.broadcast_to(jnp.arange(8, dtype=jnp.int32).reshape(1, 8, 1), (NSA_H, 8, 1)).reshape(ROWS_S, 1)
    gate_rows = small[:, :N_GATE].reshape(Bs, Tn, NSA_H, 3).transpose(0, 2, 1, 3)
    gate_rows = jnp.pad(gate_rows, ((0, 0), (0, 0), (0, 8 - Tn), (0, LANES - 3))).reshape(Bs, ROWS_S, LANES)

    tok_minor = lambda a: jnp.moveaxis(a, 1, -1)
    cmp_s = _compress_sample(tok_minor(cache_cmp), pt_flat, Bs, n_pages, lw["cmp_sw"])
    oc, sel_rows = _select_sample(qs, cmp_s, slope_rows, qidx_rows, Bs, past)
    o_s = _slc_sample(tok_minor(cache_slc), pt_flat, qs, sel_rows, pr["kvs"].reshape(Bs, Tn, KVW),
                      slope_rows, qidx_rows, Bs, n_pages, past)
    win = tok_minor(state_win)
    o_rows = _win_sample(qs, win, pr["kvw"].reshape(Bs, Tn, KVW), oc, o_s, gate_rows, slope_rows, qidx_rows, Bs)
    o_rows = o_rows.reshape(Bs, NSA_G, HG, 8, NSA_G, HD)[:, :, :, :Tn]
    o_n = jnp.stack([o_rows[:, 0, :, :, 0], o_rows[:, 1, :, :, 1]], axis=1)
    o_n = o_n.transpose(0, 3, 1, 2, 4).reshape(n, QW).astype(BF16)

    qf = jnp.pad(pr["qf"].reshape(Bs, Tn, FOX_H, HD).transpose(0, 2, 1, 3), ((0, 0), (0, 0), (0, 8 - Tn), (0, 0)))
    qf_bd = (qf[:, :, :, None, :] * jnp.eye(FOX_H, dtype=BF16)[None, :, None, :, None]).reshape(Bs, ROWS_F, FOX_K)
    page_cs = _page_cumsum(cache_logf.transpose(0, 2, 1))
    cs_new = jnp.cumsum(logf, axis=1).transpose(0, 2, 1)
    csnew_rows = jnp.broadcast_to(jnp.pad(cs_new, ((0, 0), (0, 0), (0, 8 - Tn)))[:, :, None, :],
                                 (Bs, FOX_H, 8, 8)).reshape(Bs, ROWS_F, 8)
    qidx_f = jnp.broadcast_to(jnp.arange(8, dtype=jnp.int32).reshape(1, 8, 1), (FOX_H, 8, 1)).reshape(ROWS_F, 1)
    o_frows = _fox_sample(tok_minor(cache_fox), page_cs, pt_flat, qf_bd,
                          pr["kvf"].reshape(Bs, Tn, KVF), csnew_rows, qidx_f, Bs, n_pages)
    o_frows = o_frows.reshape(Bs, FOX_H, 8, FOX_H, HD)[:, :, :Tn]
    o_f = jnp.stack([o_frows[:, h, :, h] for h in range(FOX_H)], axis=2)
    o_f = o_f.reshape(n, FOX_H * HD).astype(BF16)

    y = _ffn_tail(o_n, o_f, x2, (gate1, scale2, shift2, gate2), (g_post_mix, g_pre_ffn, g_post_ffn), lw, Tn)
    kvw_new = pr["kvw"].reshape(Bs, Tn, 2, NSA_G, HD)
    states = (pr["kvc"].reshape(Bs, Tn, 2, NSA_G, HD), pr["kvs"].reshape(Bs, Tn, 2, NSA_G, HD),
              pr["kvf"].reshape(Bs, Tn, 2, FOX_H, HD), logf,
              jnp.concatenate([state_win, kvw_new], axis=1)[:, Tn:])
    return y.reshape(Bs, Tn, d), states


def _layer_weights(w_in, b_nsa_gate, b_forget, w_cmp1, b_cmp1, w_cmp2, pe_cmp, w_out, w_router, b_router,
                   w_exp_gate, w_exp_up, w_exp_down, w_sh_gate, w_sh_up, w_sh_down):
    w_in_b, b_small = _in_weights(w_in, b_nsa_gate, b_forget)
    n_e = w_router.shape[1]
    wr = jnp.pad(w_router.T, ((0, LANES - n_e), (0, 0))).astype(BF16)
    br = jnp.concatenate([b_router.astype(F32), jnp.full((LANES - n_e,), NEG_INF, F32)]).reshape(LANES, 1)
    return dict(w_in=w_in_b, b_small=b_small,
                cmp_w=_cmp_weights(w_cmp1, b_cmp1, w_cmp2, pe_cmp),
                cmp_sw=_cmp_sample_weights(w_cmp1, b_cmp1, w_cmp2, pe_cmp),
                wo=w_out.astype(BF16), wr=wr, br=br,
                weg=w_exp_gate.astype(BF16), weu=w_exp_up.astype(BF16), wed=w_exp_down.astype(BF16),
                wsg=w_sh_gate.astype(BF16), wsu=w_sh_up.astype(BF16), wsd=w_sh_down.astype(BF16))


def kernel(x_prompt, x_sample, c_prompt, c_sample, cache_cmp_kv, cache_slc_kv, cache_fox_kv, cache_fox_logf,
           state_win_kv, page_table, w_ada, b_ada, g_pre_mix, g_post_mix, g_pre_ffn, g_post_ffn, w_in,
           b_nsa_gate, b_forget, w_cmp1, b_cmp1, w_cmp2, pe_cmp, w_out, w_router, b_router, w_exp_gate,
           w_exp_up, w_exp_down, w_sh_gate, w_sh_up, w_sh_down):
    depth = w_in.shape[0]
    Bp = x_prompt.shape[0]
    xp, xs = x_prompt, x_sample
    st_p = [[] for _ in range(5)]
    st_s = [[] for _ in range(5)]
    for l in range(depth):
        lw = _layer_weights(w_in[l], b_nsa_gate[l], b_forget[l], w_cmp1[l], b_cmp1[l], w_cmp2[l], pe_cmp[l],
                            w_out[l], w_router[l], b_router[l], w_exp_gate[l], w_exp_up[l], w_exp_down[l],
                            w_sh_gate[l], w_sh_up[l], w_sh_down[l])
        m = _modulation(jnp.concatenate([c_prompt, c_sample], axis=0), w_ada[l], b_ada[l])
        mods = jnp.split(m[:, None, :], 6, axis=-1)
        norms = (g_pre_mix[l], g_post_mix[l], g_pre_ffn[l], g_post_ffn[l])
        xp, sp = _prompt_layer(xp, [a[:Bp] for a in mods], norms, lw)
        caches = (cache_cmp_kv[l], cache_slc_kv[l], cache_fox_kv[l], cache_fox_logf[l], state_win_kv[l], page_table)
        xs, ss = _sample_layer(xs, [a[Bp:] for a in mods], norms, lw, caches)
        for lst, a in zip(st_p, sp):
            lst.append(a)
        for lst, a in zip(st_s, ss):
            lst.append(a)
    return (xp, xs, *[jnp.stack(s) for s in st_p], *[jnp.stack(s) for s in st_s])
```

```python
import functools

import numpy as np
import jax
import jax.numpy as jnp
from jax import lax
from jax.experimental import pallas as pl
from jax.experimental.pallas import tpu as pltpu

F32 = jnp.float32
BF16 = jnp.bfloat16

HD = 64
NSA_H = 8
NSA_G = 2
HG = NSA_H // NSA_G
FOX_H = 8
CMP_STRIDE = 16
CMP_LEN = 32
CMP_HID = 128
SLC_BLOCK = 64
SLC_TOPK = 16
WINDOW = 512
TOP_K = 8
ROUTED_SCALE = 2.5
RMS_EPS = 1e-6
NEG = -1e30
TINY = 1e-30
FORCE = 1e4
NEG_INF = float("-inf")

LANES = 128
KVW = 2 * NSA_G * HD
KVF = 2 * FOX_H * HD
QW = NSA_H * HD
VMEM_LIMIT = 56 * 1024 * 1024


def _cp(*sem):
    return pltpu.CompilerParams(dimension_semantics=sem, vmem_limit_bytes=VMEM_LIMIT)


def _rms(x, g):
    return x * lax.rsqrt(jnp.mean(x * x, axis=-1, keepdims=True) + RMS_EPS) * g


def _dot(a, b):
    return jnp.dot(a, b, preferred_element_type=F32)


def _dot_nt(a, b):
    return lax.dot_general(a, b, (((1,), (1,)), ((), ())), preferred_element_type=F32)


def _iota(shape, dim):
    return lax.broadcasted_iota(jnp.int32, shape, dim)


def _onehot(cond):
    return jnp.where(cond, 1.0, 0.0).astype(BF16)


def _pick_head(width_in, h_off):
    r = _iota((width_in, LANES), 0)
    c = _iota((width_in, LANES), 1)
    return _onehot((r == c + h_off) & (c < HD))


def _place_head(width_out, off):
    r = _iota((LANES, width_out), 0)
    c = _iota((LANES, width_out), 1)
    return _onehot((r >= HD) & (c == r - HD + off))


def _split3(x):
    hi = x.astype(BF16)
    r1 = x - hi.astype(F32)
    mid = r1.astype(BF16)
    lo = (r1 - mid.astype(F32)).astype(BF16)
    return hi, mid, lo


def _mod_kernel(c_ref, w_ref, b_ref, o_ref):
    c = c_ref[...]
    a = (c * jax.nn.sigmoid(c)).astype(BF16)
    o_ref[...] = _dot(a, w_ref[...].astype(BF16)) + b_ref[...]


def _modulation(c_all, w_ada, b_ada):
    nb, d = c_all.shape
    n_chunk = w_ada.shape[1] // d
    return pl.pallas_call(
        _mod_kernel,
        grid=(n_chunk,),
        in_specs=[pl.BlockSpec((nb, d), lambda i: (0, 0)),
                  pl.BlockSpec((d, d), lambda i: (0, i)),
                  pl.BlockSpec((1, d), lambda i: (0, i))],
        out_specs=pl.BlockSpec((nb, d), lambda i: (0, i)),
        out_shape=jax.ShapeDtypeStruct((nb, w_ada.shape[1]), F32),
        compiler_params=_cp("parallel"),
        name="modulation",
    )(c_all, w_ada, b_ada.reshape(1, -1))


N_GATE = 3 * NSA_H
SEL_TK = 256
LOG2E = 1.4426950408889634
_IN_BASE = (("kvc", KVW), ("kvs", KVW), ("kvw", KVW), ("kvf", KVF), ("small", LANES), ("qn", QW), ("qf", QW))
_IN_OUTS = {
    "prompt": (("kvc", "f32", KVW), ("small", "f32", LANES), ("qn", "bf16", QW), ("qf", "bf16", QW),
               ("ks_op", "bf16", NSA_G * LANES), ("vs_op", "bf16", NSA_G * LANES),
               ("kw_op", "bf16", NSA_G * LANES), ("vw_op", "bf16", NSA_G * LANES),
               ("kf_op", "bf16", FOX_H * LANES), ("vf_op", "bf16", FOX_H * LANES),
               ("kvc_t", "f32_t", KVW), ("kvs_t", "f32_t", KVW), ("kvw_t", "f32_t", KVW), ("kvf_t", "f32_t", KVF)),
    "sample": (("kvc", "f32", KVW), ("kvs", "f32", KVW), ("kvw", "f32", KVW), ("kvf", "f32", KVF),
               ("small", "f32", LANES), ("qn", "bf16", QW), ("qf", "bf16", QW)),
}


def _k_operand(kk, fill):
    lo = _iota(kk.shape, 1) < HD
    return jnp.concatenate([jnp.where(lo, kk, fill), jnp.where(lo, pltpu.roll(kk, HD, 1), fill)], axis=1)


def _v_operand(vv):
    lo = _iota(vv.shape, 1) < HD
    return jnp.concatenate([jnp.where(lo, 1.0, pltpu.roll(vv, HD, 1)), jnp.where(lo, 1.0, vv)], axis=1)


def _in_kernel(x_ref, sc_ref, sh_ref, g_ref, w_ref, bias_ref, *out_refs, outs, seq_len):
    x = x_ref[...]
    tm = x.shape[0]
    h = _rms(x, g_ref[...]) * (1.0 + sc_ref[...]) + sh_ref[...]
    hb = h.astype(BF16)
    base, off = {}, 0
    for name, width in _IN_BASE:
        base[name] = _dot(hb, w_ref[:, off:off + width])
        off += width
    z = base["small"] + bias_ref[...]
    logsig = jnp.minimum(z, 0.0) - jnp.log1p(jnp.exp(-jnp.abs(z)))
    base["small"] = jnp.where(_iota(z.shape, 1) < N_GATE, jax.nn.sigmoid(z), logsig)
    lane = _iota((tm, LANES), 1)
    nk, fk = NSA_G * HD, FOX_H * HD
    for (name, kind, _), o_ref in zip(outs, out_refs):
        if kind == "f32_t":
            o_ref[0] = base[name[:-2]].T
            continue
        if name == "ks_op":
            t = (pl.program_id(0) * tm) % seq_len + _iota((tm, LANES), 0)
            pos = jnp.where(lane == HD, (t & (SEL_TK - 1)).astype(F32), 0.0)
            val = _k_operand(base["kvs"][:, :nk], pos)
        elif name == "kw_op":
            val = _k_operand(base["kvw"][:, :nk], 0.0)
        elif name == "vs_op":
            val = _v_operand(base["kvs"][:, nk:])
        elif name == "vw_op":
            val = _v_operand(base["kvw"][:, nk:])
        elif name == "kf_op":
            kf = base["kvf"][:, :fk]
            val = jnp.concatenate(
                [jnp.where((lane < HD) == (i % 2 == 0), kf[:, (i // 2) * LANES:(i // 2 + 1) * LANES], 0.0)
                 for i in range(FOX_H)], axis=1)
        elif name == "vf_op":
            vf = base["kvf"][:, fk:]
            val = jnp.concatenate([_v_operand(vf[:, j * LANES:(j + 1) * LANES]) for j in range(FOX_H // 2)], axis=1)
        else:
            val = base[name]
        o_ref[...] = val.astype(o_ref.dtype)


def _in_weights(w_in, b_nsa_gate, b_forget):
    d = w_in.shape[0]
    o_qn, o_kvc, o_kvs, o_kvw = 0, QW, QW + KVW, QW + 2 * KVW
    o_gn = QW + 3 * KVW
    o_qf = o_gn + N_GATE
    o_kvf = o_qf + QW
    o_ff = o_kvf + KVF
    scale = HD ** -0.5
    cols = dict(
        kvc=w_in[:, o_kvc:o_kvc + KVW], kvs=w_in[:, o_kvs:o_kvs + KVW], kvw=w_in[:, o_kvw:o_kvw + KVW],
        kvf=w_in[:, o_kvf:o_kvf + KVF],
        small=jnp.concatenate([w_in[:, o_gn:o_gn + N_GATE], w_in[:, o_ff:o_ff + FOX_H],
                               jnp.zeros((d, LANES - N_GATE - FOX_H), F32)], axis=1),
        qn=w_in[:, o_qn:o_qn + QW] * scale, qf=w_in[:, o_qf:o_qf + QW] * (scale * LOG2E))
    w = jnp.concatenate([cols[name] for name, _ in _IN_BASE], axis=1).astype(BF16)
    bias = jnp.concatenate([b_nsa_gate, b_forget, jnp.zeros((LANES - N_GATE - FOX_H,), F32)]).reshape(1, LANES)
    return w, bias


def _in_proj(x2, scale, shift, g_pre, w, bias, rows_per_mod, group):
    n, d = x2.shape
    tm = min(256, n)
    outs = _IN_OUTS[group]
    if rows_per_mod >= tm:
        per = rows_per_mod // tm
        mod_spec = pl.BlockSpec((None, 1, d), lambda i: (i // per, 0, 0))
    else:
        assert not any(kind == "f32_t" for _, kind, _ in outs)
        per = 1
        scale = jnp.repeat(scale[:, 0, :], rows_per_mod, axis=0)
        shift = jnp.repeat(shift[:, 0, :], rows_per_mod, axis=0)
        mod_spec = pl.BlockSpec((tm, d), lambda i: (i, 0))
    n_seq = n // rows_per_mod
    out_specs, out_shape = [], []
    for _, kind, width in outs:
        if kind == "f32_t":
            out_specs.append(pl.BlockSpec((1, width, tm), lambda i: (i // per, 0, i % per)))
            out_shape.append(jax.ShapeDtypeStruct((n_seq, width, rows_per_mod), F32))
        else:
            out_specs.append(pl.BlockSpec((tm, width), lambda i: (i, 0)))
            out_shape.append(jax.ShapeDtypeStruct((n, width), F32 if kind == "f32" else BF16))
    full = lambda a: pl.BlockSpec(a.shape, lambda i: (0,) * a.ndim)
    res = pl.pallas_call(
        functools.partial(_in_kernel, outs=outs, seq_len=rows_per_mod),
        grid=(n // tm,),
        in_specs=[pl.BlockSpec((tm, d), lambda i: (i, 0)), mod_spec, mod_spec,
                  pl.BlockSpec((1, d), lambda i: (0, 0)), full(w), full(bias)],
        out_specs=out_specs,
        out_shape=out_shape,
        compiler_params=_cp("parallel"),
        name="in_proj_" + group,
    )(x2, scale, shift, g_pre.reshape(1, d), w, bias)
    return dict(zip([name for name, _, _ in outs], res))


def _cmp_kernel(subk_ref, subv_ref, w1_ref, pe_ref, b1_ref, w2_ref, o_ref):
    half = CMP_STRIDE * HD
    out = None
    for kv, sub_ref in enumerate((subk_ref, subv_ref)):
        sub = sub_ref[0, 0]
        w1 = w1_ref[kv]
        a = _dot(sub, w1[:half])
        b = _dot(sub, w1[half:])
        n_sub = a.shape[0]
        b_next = pltpu.roll(b, n_sub - 1, 0)
        bias = _dot(pe_ref[kv], w1)[0:1] + b1_ref[kv]
        hid = jax.nn.gelu(a + b_next + bias).astype(BF16)
        term = _dot(hid, w2_ref[kv])
        out = term if out is None else out + term
    o_ref[0, 0] = out.astype(o_ref.dtype)


def _cmp_weights(w_cmp1, b_cmp1, w_cmp2, pe_cmp):
    w1 = w_cmp1.astype(BF16)
    pe = jnp.broadcast_to(pe_cmp.reshape(2, 1, CMP_LEN * HD), (2, 8, CMP_LEN * HD)).astype(BF16)
    b1 = b_cmp1.reshape(2, 1, CMP_HID)
    z = jnp.zeros((CMP_HID, HD), F32)
    w2 = jnp.stack([jnp.concatenate([w_cmp2[0], z], axis=1),
                    jnp.concatenate([z, w_cmp2[1]], axis=1)]).astype(BF16)
    return w1, pe, b1, w2


def _compress_prompt(kvc, B, T, cw):
    w1, pe, b1, w2 = cw
    n_sub = T // CMP_STRIDE
    sub = kvc.reshape(B, n_sub, CMP_STRIDE, 2 * NSA_G, HD).transpose(0, 3, 1, 2, 4)
    sub = sub.reshape(B, 2 * NSA_G, n_sub, CMP_STRIDE * HD).astype(BF16)
    feat = CMP_STRIDE * HD
    return pl.pallas_call(
        _cmp_kernel,
        grid=(B, NSA_G),
        in_specs=[pl.BlockSpec((1, 1, n_sub, feat), lambda b, g: (b, g, 0, 0)),
                  pl.BlockSpec((1, 1, n_sub, feat), lambda b, g: (b, NSA_G + g, 0, 0)),
                  pl.BlockSpec(w1.shape, lambda b, g: (0, 0, 0)),
                  pl.BlockSpec(pe.shape, lambda b, g: (0, 0, 0)),
                  pl.BlockSpec(b1.shape, lambda b, g: (0, 0, 0)),
                  pl.BlockSpec(w2.shape, lambda b, g: (0, 0, 0))],
        out_specs=pl.BlockSpec((1, 1, n_sub, LANES), lambda b, g: (b, g, 0, 0)),
        out_shape=jax.ShapeDtypeStruct((B, NSA_G, n_sub, LANES), BF16),
        compiler_params=_cp("parallel", "parallel"),
        name="compress_prompt",
    )(sub, sub, w1, pe, b1, w2)


def _cumsum_block(x):
    n = x.shape[1]
    u = _onehot(_iota((n, n), 0) <= _iota((n, n), 1))
    r = x.shape[0]
    d3 = _dot(jnp.concatenate(_split3(x), axis=0), u)
    return d3[:r] + d3[r:2 * r] + d3[2 * r:]


def _cumsum_kernel(x_ref, o_ref, carry_ref):
    @pl.when(pl.program_id(1) == 0)
    def _():
        carry_ref[...] = jnp.zeros_like(carry_ref)

    d = _cumsum_block(x_ref[0]) + carry_ref[:, 0:1]
    o_ref[0] = d
    carry_ref[...] = jnp.broadcast_to(d[:, d.shape[1] - 1:], carry_ref.shape)


def _cumsum_prompt(logf_t):
    B, H, T = logf_t.shape
    tc = min(512, T)
    return pl.pallas_call(
        _cumsum_kernel,
        grid=(B, T // tc),
        in_specs=[pl.BlockSpec((1, H, tc), lambda b, c: (b, 0, c))],
        out_specs=pl.BlockSpec((1, H, tc), lambda b, c: (b, 0, c)),
        out_shape=jax.ShapeDtypeStruct((B, H, T), F32),
        scratch_shapes=[pltpu.VMEM((H, LANES), F32)],
        compiler_params=_cp("parallel", "arbitrary"),
        name="cumsum_prompt",
    )(logf_t)


MASK_BIG = 2.0 ** 30


def _topk_mask(val, n_take, axis):
    idx_f = _iota(val.shape, axis).astype(F32)
    sel = jnp.zeros(val.shape, F32)
    for _ in range(n_take):
        m = jnp.max(val, axis=axis, keepdims=True)
        first = jnp.min(jnp.where(val == m, idx_f, float(val.shape[axis])), axis=axis, keepdims=True)
        pick = idx_f == first
        sel = jnp.where(pick, 1.0, sel)
        val = jnp.where(pick, NEG_INF, val)
    return sel


def _masked_softmax(s, mask, axis=1):
    s = jnp.where(mask, s, NEG)
    m = jnp.max(s, axis=axis, keepdims=True)
    e = jnp.where(mask, jnp.exp(s - m), 0.0)
    return e / jnp.maximum(jnp.sum(e, axis=axis, keepdims=True), TINY)


def _flash_step(s, v, m_ref, acc_ref, base2):
    nk = s.shape[1]
    ex = jnp.exp2 if base2 else jnp.exp
    m_old = m_ref[...]
    m_new = jnp.maximum(m_old, jnp.max(s, axis=1, keepdims=True))
    alpha = ex(m_old - m_new)
    e = jnp.concatenate([ex(s[:, c * LANES:(c + 1) * LANES] - m_new) for c in range(nk // LANES)], axis=1)
    acc_ref[...] = alpha * acc_ref[...] + _dot(e.astype(BF16), v)
    m_ref[...] = m_new


def _hi_over_lo(acc):
    return acc / jnp.maximum(pltpu.roll(acc, HD, 1), TINY)


def _pair_heads(o_even, o_odd):
    lane = _iota(o_even.shape, 1)
    return jnp.where(lane < HD, pltpu.roll(o_even, HD, 1), o_odd)


def _cmp_to_slc_np(n_slots, first_token):
    tok = np.arange(n_slots)[:, None] + first_token
    start = tok * CMP_STRIDE
    bstart = np.arange(LANES)[None, :] * SLC_BLOCK
    m = (start < bstart + SLC_BLOCK) & (start + CMP_LEN > bstart) & (tok >= 0)
    return m.astype(np.float32)


def _nsa_sel_kernel(slopes_ref, q_ref, cmp_ref, cmpt_ref, c2st_ref, oc_ref, sel_ref, *, tq, n_sel, n_case):
    g = pl.program_id(1)
    t0 = pl.program_id(2) * tq
    n_slots = cmp_ref.shape[2]
    qpos = t0 + _iota((1, tq), 1)
    q = q_ref[...]

    def run(ns):
        cend = _iota((ns, 1), 0) * CMP_STRIDE + (CMP_LEN - 1)
        d_c = qpos - cend
        mask_c = d_c >= 0
        d_cf = d_c.astype(F32)
        cmp = cmp_ref[0, 0, :ns, :]
        cmpt = cmpt_ref[0, 0, :, :ns]
        nb = min(LANES, ns * CMP_STRIDE // SLC_BLOCK)
        c2st = c2st_ref[:nb, :ns]
        imp_t = jnp.zeros((nb, tq), F32)
        for h in range(HG):
            q128 = _dot(q, _pick_head(HG * HD, h * HD)).astype(BF16)
            s_t = _dot_nt(cmp, q128) - slopes_ref[g * HG + h] * d_cf
            p_t = _masked_softmax(s_t, mask_c, axis=0).astype(BF16)
            oc_ref[0, 0, h] = _dot(cmpt, p_t).T
            imp_t = imp_t + _dot(c2st, p_t)
        blk = _iota((nb, tq), 0)
        cur = qpos >> 6
        forced = (blk == 0) | (blk == cur) | (blk == cur - 1)
        valid = blk * SLC_BLOCK <= qpos
        val = jnp.where(valid, imp_t + jnp.where(forced, FORCE, 0.0), NEG)
        sel_t = jnp.where(valid, _topk_mask(val, n_sel, 0), 0.0)
        sel_t = jnp.concatenate([sel_t, jnp.zeros((LANES - nb, tq), F32)], axis=0) if nb < LANES else sel_t
        sel_ref[0, 0] = ((sel_t - 1.0) * MASK_BIG).T.astype(sel_ref.dtype)

    per = n_slots // n_case
    need = (t0 + tq + CMP_STRIDE - 1) // CMP_STRIDE
    case = jnp.minimum((need - 1) // per, n_case - 1)
    for k in range(n_case):
        pl.when(case == k)(functools.partial(run, (k + 1) * per))


NSA_TQ_SEL, NSA_TQ_ATT = 512, 256


def _nsa_select(slopes, qn, cmp, B, T):
    tq = min(NSA_TQ_SEL, T)
    nq = T // tq
    n_slots = cmp.shape[2]
    n_slc = -(-T // SLC_BLOCK)
    assert n_slc <= LANES and SLC_BLOCK == 64
    c2st = jnp.asarray(_cmp_to_slc_np(n_slots, 0).T, BF16)
    cmpt = jnp.swapaxes(cmp, 2, 3)
    return pl.pallas_call(
        functools.partial(_nsa_sel_kernel, tq=tq, n_sel=min(SLC_TOPK, n_slc),
                          n_case=4 if n_slots % (4 * LANES) == 0 else 1),
        grid=(B, NSA_G, nq),
        in_specs=[pl.BlockSpec(memory_space=pltpu.SMEM),
                  pl.BlockSpec((tq, HG * HD), lambda b, g, j: (b * nq + j, g)),
                  pl.BlockSpec((1, 1, n_slots, LANES), lambda b, g, j: (b, g, 0, 0)),
                  pl.BlockSpec((1, 1, LANES, n_slots), lambda b, g, j: (b, g, 0, 0)),
                  pl.BlockSpec((LANES, n_slots), lambda b, g, j: (0, 0))],
        out_specs=[pl.BlockSpec((1, 1, HG, tq, LANES), lambda b, g, j: (b, g, 0, j, 0)),
                   pl.BlockSpec((1, 1, tq, LANES), lambda b, g, j: (b, g, j, 0))],
        out_shape=[jax.ShapeDtypeStruct((B, NSA_G, HG, T, LANES), F32),
                   jax.ShapeDtypeStruct((B, NSA_G, T, LANES), BF16)],
        compiler_params=_cp("parallel", "parallel", "parallel"),
        name="nsa_select",
    )(slopes, qn, cmp, cmpt, c2st)


def _tile_bits(selneg, B, T, tq, tk):
    nq = T // tq
    n_tiles = T // tk
    bpt = tk // SLC_BLOCK
    assert n_tiles <= 32
    anyb = (selneg.reshape(B, NSA_G, nq, tq, LANES) == 0).any(axis=3)
    anyt = anyb[..., :n_tiles * bpt].reshape(B, NSA_G, nq, n_tiles, bpt).any(axis=-1)
    w = jnp.sum(anyt.astype(jnp.uint32) << jnp.arange(n_tiles, dtype=jnp.uint32), axis=-1, dtype=jnp.uint32)
    return lax.bitcast_convert_type(w, jnp.int32).reshape(-1)


def _nsa_attn_kernel(bits_ref, slopes_ref, q_ref, ks_ref, vs_ref, kw_ref, vw_ref, blk1h_ref, sel_ref, oc_ref,
                     gates_ref, o_ref, m_s, acc_s, *, tq, tk, wk):
    b, g, j = pl.program_id(0), pl.program_id(1), pl.program_id(2)
    t0 = j * tq
    q = q_ref[...]
    selneg = sel_ref[0, 0]
    lane = _iota((tq, LANES), 1)
    slopes = [slopes_ref[g * HG + h] for h in range(HG)]
    q_sel, q_win = [], []
    for h in range(HG):
        q128 = _dot(q, _pick_head(HG * HD, h * HD))
        q_win.append(q128.astype(BF16))
        q_sel.append(jnp.concatenate([jnp.where(lane == HD, slopes[h], q128).astype(BF16), selneg], axis=1))
    q4 = jnp.concatenate(q_sel, axis=0)
    qw4 = jnp.concatenate(q_win, axis=0)
    qpos = t0 + _iota((tq, 1), 0)

    m_s[...] = jnp.full(m_s.shape, NEG, F32)
    acc_s[...] = jnp.zeros(acc_s.shape, F32)
    word = bits_ref[(b * NSA_G + g) * pl.num_programs(2) + j]

    def tile(kt, diagonal):
        k0 = pl.multiple_of(kt * tk, tk)
        kop = jnp.concatenate([ks_ref[pl.ds(k0, tk), :], blk1h_ref[pl.ds(k0, tk), :]], axis=1)
        vop = vs_ref[pl.ds(k0, tk), :]
        s = _dot_nt(q4, kop)
        off = (k0 - t0).astype(F32)
        if diagonal:
            causal = (k0 + _iota((1, tk), 1)) <= qpos
        for h in range(HG):
            sh = s[h * tq:(h + 1) * tq] + slopes[h] * off
            if diagonal:
                sh = jnp.where(causal, sh, NEG)
            _flash_step(sh, vop, m_s.at[h], acc_s.at[h], False)

    def body(kt, carry):
        @pl.when(((word >> kt) & 1) == 1)
        def _():
            tile(kt, False)
        return carry

    n_before = t0 // tk
    lax.fori_loop(0, n_before, body, 0)
    for dd in range(max(1, tq // tk)):
        tile(n_before + dd, True)

    w0 = pl.multiple_of(jnp.maximum(t0 + tq - wk, 0), tq)
    kw = kw_ref[pl.ds(w0, wk), :]
    vw = vw_ref[pl.ds(w0, wk), :]
    d_w = qpos - (w0 + _iota((1, wk), 1))
    mask_w = (d_w >= 0) & (d_w < WINDOW)
    d_wf = d_w.astype(F32)
    s_w = _dot_nt(qw4, kw)
    gates = gates_ref[0]
    o_heads = []
    for h in range(HG):
        sw = jnp.where(mask_w, s_w[h * tq:(h + 1) * tq] - slopes[h] * d_wf, NEG)
        e_w = jnp.where(mask_w, jnp.exp(sw - jnp.max(sw, axis=1, keepdims=True)), 0.0)
        o_w = _hi_over_lo(_dot(e_w.astype(BF16), vw))
        o_s = _hi_over_lo(acc_s[h])
        o_heads.append(gates[:, 3 * h:3 * h + 1] * oc_ref[0, 0, h] + gates[:, 3 * h + 1:3 * h + 2] * o_s
                       + gates[:, 3 * h + 2:3 * h + 3] * o_w)
    o_ref[...] = jnp.concatenate([_pair_heads(o_heads[0], o_heads[1]), _pair_heads(o_heads[2], o_heads[3])],
                                 axis=1).astype(o_ref.dtype)


def _nsa_attend(slopes, qn, pr, selneg, oc, gates_g, B, T):
    tq = min(NSA_TQ_ATT, T)
    nq = T // tq
    tk = min(SEL_TK, T)
    assert tk == SEL_TK and HG == 4
    wk = min(WINDOW + tq, T)
    bits = _tile_bits(selneg, B, T, tq, tk)
    blk1h = jnp.asarray(np.arange(T)[:, None] // SLC_BLOCK == np.arange(LANES)[None, :], BF16)
    kv_spec = pl.BlockSpec((T, LANES), lambda b, g, j, bits: (b, g))
    return pl.pallas_call(
        functools.partial(_nsa_attn_kernel, tq=tq, tk=tk, wk=wk),
        grid_spec=pltpu.PrefetchScalarGridSpec(
            num_scalar_prefetch=1, grid=(B, NSA_G, nq),
            in_specs=[pl.BlockSpec(memory_space=pltpu.SMEM),
                      pl.BlockSpec((tq, HG * HD), lambda b, g, j, bits: (b * nq + j, g)),
                      kv_spec, kv_spec, kv_spec, kv_spec,
                      pl.BlockSpec((T, LANES), lambda b, g, j, bits: (0, 0)),
                      pl.BlockSpec((1, 1, tq, LANES), lambda b, g, j, bits: (b, g, j, 0)),
                      pl.BlockSpec((1, 1, HG, tq, LANES), lambda b, g, j, bits: (b, g, 0, j, 0)),
                      pl.BlockSpec((1, tq, LANES), lambda b, g, j, bits: (g, b * nq + j, 0))],
            out_specs=pl.BlockSpec((tq, HG * HD), lambda b, g, j, bits: (b * nq + j, g)),
            scratch_shapes=[pltpu.VMEM((HG, tq, LANES), F32), pltpu.VMEM((HG, tq, LANES), F32)]),
        out_shape=jax.ShapeDtypeStruct((B * T, QW), BF16),
        compiler_params=_cp("parallel", "parallel", "parallel"),
        name="nsa_attend",
    )(bits, slopes, qn, pr["ks_op"], pr["vs_op"], pr["kw_op"], pr["vw_op"], blk1h, selneg, oc, gates_g)


FOX_TQ, FOX_TK, FOX_ROWS = 1024, 512, 1024


def _fox_kernel(q_ref, k0_ref, k1_ref, v0_ref, v1_ref, d0_ref, d1_ref, o_ref, m_s, acc_s, *, tq, rows, tk):
    j = pl.program_id(2)
    n_part = tq // rows
    k_refs, v_refs, d_refs = (k0_ref, k1_ref), (v0_ref, v1_ref), (d0_ref, d1_ref)
    m_s[...] = jnp.full(m_s.shape, NEG, F32)
    acc_s[...] = jnp.zeros(acc_s.shape, F32)

    def step(kt, diagonal):
        k0 = pl.multiple_of(kt * tk, tk)
        for hh in range(2):
            k = k_refs[hh][pl.ds(k0, tk), :]
            v = v_refs[hh][pl.ds(k0, tk), :]
            bias = d_refs[hh][0, kt] * LOG2E
            for r in range(n_part):
                s = _dot_nt(q_ref[r * rows:(r + 1) * rows, :], k) - bias
                if diagonal:
                    qpos = j * tq + r * rows + _iota((rows, 1), 0)
                    s = jnp.where(k0 + _iota((1, tk), 1) <= qpos, s, NEG)
                _flash_step(s, v, m_s.at[hh * n_part + r], acc_s.at[hh * n_part + r], True)

    def body(kt, carry):
        step(kt, False)
        return carry

    n_full = (j * tq) // tk
    lax.fori_loop(0, n_full, body, 0)
    for dd in range(tq // tk):
        step(n_full + dd, True)
    for r in range(n_part):
        o_ref[r * rows:(r + 1) * rows, :] = _pair_heads(_hi_over_lo(acc_s[r]),
                                                         _hi_over_lo(acc_s[n_part + r])).astype(o_ref.dtype)


def _fox_attend(qf, kf_op, vf_op, d_t, B, T):
    tq = min(FOX_TQ, T)
    tk = min(FOX_TK, tq)
    rows = min(FOX_ROWS, tq)
    nq = T // tq
    d4 = d_t.reshape(B * FOX_H, T // tk, 1, tk)
    kv_spec = lambda hh: pl.BlockSpec((T, LANES), lambda b, hp, j: (b, 2 * hp + hh))
    d_spec = lambda hh: pl.BlockSpec((1, T // tk, 1, tk), lambda b, hp, j: (b * FOX_H + 2 * hp + hh, 0, 0, 0))
    n_chain = 2 * (tq // rows)
    return pl.pallas_call(
        functools.partial(_fox_kernel, tq=tq, rows=rows, tk=tk),
        grid=(B, FOX_H // 2, nq),
        in_specs=[pl.BlockSpec((tq, 2 * HD), lambda b, hp, j: (b * nq + j, hp)),
                  kv_spec(0), kv_spec(1), kv_spec(0), kv_spec(1), d_spec(0), d_spec(1)],
        out_specs=pl.BlockSpec((tq, 2 * HD), lambda b, hp, j: (b * nq + j, hp)),
        out_shape=jax.ShapeDtypeStruct((B * T, FOX_H * HD), BF16),
        scratch_shapes=[pltpu.VMEM((n_chain, rows, LANES), F32), pltpu.VMEM((n_chain, rows, LANES), F32)],
        compiler_params=_cp("parallel", "parallel", "parallel"),
        name="fox_attend",
    )(qf, kf_op, kf_op, vf_op, vf_op, d4, d4)


def _out_kernel(on_ref, of_ref, x_ref, g1_ref, sc2_ref, sh2_ref, gpm_ref, gpf_ref, wo_ref, wr_ref, br_ref,
                x1_ref, h2_ref, gd_ref):
    y = _dot(on_ref[...], wo_ref[:QW]) + _dot(of_ref[...], wo_ref[QW:])
    x1 = x_ref[...] + g1_ref[...] * _rms(y, gpm_ref[...])
    x1_ref[...] = x1
    h2 = (_rms(x1, gpf_ref[...]) * (1.0 + sc2_ref[...]) + sh2_ref[...]).astype(BF16)
    h2_ref[...] = h2
    scores_t = jax.nn.sigmoid(_dot_nt(wr_ref[...], h2))
    sel_t = _topk_mask(scores_t + br_ref[...], TOP_K, 0)
    g_t = sel_t * scores_t
    gd_ref[...] = (g_t / jnp.sum(g_t, axis=0, keepdims=True) * ROUTED_SCALE).T


def _out_proj(o_n, o_f, x2, gate1, scale2, shift2, g_post_mix, g_pre_ffn, wo, wr, br, rows_per_mod):
    n, d = x2.shape
    tm = min(256, n)
    if rows_per_mod >= tm:
        per = rows_per_mod // tm
        mod_spec = pl.BlockSpec((None, 1, d), lambda i: (i // per, 0, 0))
    else:
        gate1, scale2, shift2 = (jnp.repeat(a[:, 0, :], rows_per_mod, axis=0) for a in (gate1, scale2, shift2))
        mod_spec = pl.BlockSpec((tm, d), lambda i: (i, 0))
    row = lambda w: pl.BlockSpec((tm, w), lambda i: (i, 0))
    full = lambda a: pl.BlockSpec(a.shape, lambda i: (0,) * a.ndim)
    gpm, gpf = g_post_mix.reshape(1, d), g_pre_ffn.reshape(1, d)
    return pl.pallas_call(
        _out_kernel,
        grid=(n // tm,),
        in_specs=[row(QW), row(QW), row(d), mod_spec, mod_spec, mod_spec, full(gpm), full(gpf),
                  full(wo), full(wr), full(br)],
        out_specs=[row(d), row(d), row(LANES)],
        out_shape=[jax.ShapeDtypeStruct((n, d), F32), jax.ShapeDtypeStruct((n, d), BF16),
                   jax.ShapeDtypeStruct((n, LANES), F32)],
        compiler_params=_cp("parallel"),
        name="out_proj_router",
    )(o_n, o_f, x2, gate1, scale2, shift2, gpm, gpf, wo, wr, br)


MOE_EB, MOE_TM = 8, 1024


def _moe_kernel(h_ref, gd_ref, wg_ref, wu_ref, wd_ref, o_ref, *, eb):
    blk = pl.program_id(1)

    @pl.when(blk == 0)
    def _():
        o_ref[...] = jnp.zeros_like(o_ref)

    h = h_ref[...]
    f = wg_ref.shape[2]
    gd = pltpu.roll(gd_ref[...], (-(blk * eb)) & (LANES - 1), 1)
    parts = []
    for i in range(eb):
        a = _dot(h, wg_ref[i])
        parts.append((a * jax.nn.sigmoid(a) * _dot(h, wu_ref[i]) * gd[:, i:i + 1]).astype(BF16))
    hb = jnp.concatenate(parts, axis=1)
    o_ref[...] += _dot(hb, wd_ref[...].reshape(eb * f, wd_ref.shape[2]))


def _moe_routed(h2, gd, wg, wu, wd):
    n, d = h2.shape
    n_e, _, f = wg.shape
    tm = min(MOE_TM, n)
    eb = MOE_EB
    assert n_e % eb == 0
    return pl.pallas_call(
        functools.partial(_moe_kernel, eb=eb),
        grid=(n // tm, n_e // eb),
        in_specs=[pl.BlockSpec((tm, d), lambda i, e: (i, 0)),
                  pl.BlockSpec((tm, LANES), lambda i, e: (i, 0)),
                  pl.BlockSpec((eb, d, f), lambda i, e: (e, 0, 0)),
                  pl.BlockSpec((eb, d, f), lambda i, e: (e, 0, 0)),
                  pl.BlockSpec((eb, f, d), lambda i, e: (e, 0, 0))],
        out_specs=pl.BlockSpec((tm, d), lambda i, e: (i, 0)),
        out_shape=jax.ShapeDtypeStruct((n, d), F32),
        compiler_params=_cp("parallel", "arbitrary"),
        name="moe_routed",
    )(h2, gd, wg, wu, wd)


def _final_kernel(r_ref, h_ref, x1_ref, g2_ref, gpf_ref, wg_ref, wu_ref, wd_ref, y_ref):
    h = h_ref[...]
    a = _dot(h, wg_ref[...])
    hb = (a * jax.nn.sigmoid(a) * _dot(h, wu_ref[...])).astype(BF16)
    f = r_ref[...] + _dot(hb, wd_ref[...])
    y_ref[...] = x1_ref[...] + g2_ref[...] * _rms(f, gpf_ref[...])


def _final(routed, h2, x1, gate2, g_post_ffn, wsg, wsu, wsd, rows_per_mod):
    n, d = x1.shape
    tm = min(512, n)
    if rows_per_mod >= tm:
        per = rows_per_mod // tm
        mod_spec = pl.BlockSpec((None, 1, d), lambda i: (i // per, 0, 0))
    else:
        gate2 = jnp.repeat(gate2[:, 0, :], rows_per_mod, axis=0)
        mod_spec = pl.BlockSpec((tm, d), lambda i: (i, 0))
    row = pl.BlockSpec((tm, d), lambda i: (i, 0))
    full = lambda a: pl.BlockSpec(a.shape, lambda i: (0,) * a.ndim)
    gpf = g_post_ffn.reshape(1, d)
    return pl.pallas_call(
        _final_kernel,
        grid=(n // tm,),
        in_specs=[row, row, row, mod_spec, full(gpf), full(wsg), full(wsu), full(wsd)],
        out_specs=row,
        out_shape=jax.ShapeDtypeStruct((n, d), F32),
        compiler_params=_cp("parallel"),
        name="shared_ffn_final",
    )(routed, h2, x1, gate2, gpf, wsg, wsu, wsd)


PAGES_PER_STEP = 16


def _page_specs(block, n_pages, pp):
    def spec(jj):
        return pl.BlockSpec(block, lambda b, c, pt: (pt[b * n_pages + c * pp + jj],) + (0,) * (len(block) - 1))
    return [spec(jj) for jj in range(pp)]


def _cmp_sample_kernel(pt_ref, *refs, pp):
    pages = refs[:pp]
    w1_ref, bias_ref, w2_ref, o_ref, carry_ref, x_s = refs[pp:]
    c = pl.program_id(1)
    page = pages[0].shape[-1]
    rows = pp * page // CMP_STRIDE

    @pl.when(c == 0)
    def _():
        carry_ref[...] = jnp.zeros_like(carry_ref)

    for jj, p in enumerate(pages):
        for kv in range(2):
            x_s[kv, jj * page:(jj + 1) * page, :] = p[0, kv].reshape(NSA_G * HD, page).T

    acc = []
    for kv in range(2):
        taps = [x_s[kv, pl.ds(t, rows, stride=CMP_STRIDE), :].astype(BF16) for t in range(CMP_STRIDE)]
        acc.append(_dot(jnp.concatenate(taps, axis=1), w1_ref[kv]))
    out = [None, None]
    for kv in range(2):
        a = acc[kv][:, :2 * CMP_HID]
        b = acc[kv][:, 2 * CMP_HID:]
        a_prev = pltpu.roll(a, 1, 0)
        a_prev = jnp.where(_iota(a.shape, 0) == 0, carry_ref[kv][0:1], a_prev)
        carry_ref[kv] = jnp.broadcast_to(a[rows - 1:rows], carry_ref.shape[1:])
        hid = jax.nn.gelu(a_prev + b + bias_ref[kv]).astype(BF16)
        for g in range(NSA_G):
            term = _dot(hid[:, g * CMP_HID:(g + 1) * CMP_HID], w2_ref[kv, g])
            out[g] = term if out[g] is None else out[g] + term
    for g in range(NSA_G):
        o_ref[0, g] = out[g].astype(o_ref.dtype)


def _cmp_sample_weights(w_cmp1, b_cmp1, w_cmp2, pe_cmp):
    w1 = w_cmp1.reshape(2, 2, CMP_STRIDE, HD, CMP_HID)
    z = jnp.zeros_like(w1[:, 0])
    top = jnp.concatenate([w1[:, 0], z, w1[:, 1], z], axis=-1)
    bot = jnp.concatenate([z, w1[:, 0], z, w1[:, 1]], axis=-1)
    w1bd = jnp.concatenate([top, bot], axis=2).astype(BF16)
    w1bd = w1bd.reshape(2, CMP_STRIDE * LANES, 4 * CMP_HID)
    bias = (jnp.einsum("kf,kfh->kh", pe_cmp.reshape(2, -1).astype(BF16), w_cmp1.astype(BF16),
                       preferred_element_type=F32) + b_cmp1)
    bias = jnp.concatenate([bias, bias], axis=-1).reshape(2, 1, 2 * CMP_HID)
    z2 = jnp.zeros((CMP_HID, HD), F32)
    lo = lambda w: jnp.concatenate([w, z2], axis=1)
    hi = lambda w: jnp.concatenate([z2, w], axis=1)
    w2 = jnp.stack([jnp.stack([lo(w_cmp2[0]), hi(w_cmp2[0])]),
                    jnp.stack([hi(w_cmp2[1]), lo(w_cmp2[1])])]).astype(BF16)
    return w1bd, bias, w2


def _compress_sample(cache, pt_flat, Bs, n_pages, csw):
    w1bd, bias, w2 = csw
    pp = min(2 * PAGES_PER_STEP, n_pages)
    page = cache.shape[-1]
    rows = pp * (page // CMP_STRIDE)
    n_slots = n_pages * (page // CMP_STRIDE)
    full = lambda a: pl.BlockSpec(a.shape, lambda b, c, pt: (0,) * a.ndim)
    return pl.pallas_call(
        functools.partial(_cmp_sample_kernel, pp=pp),
        grid_spec=pltpu.PrefetchScalarGridSpec(
            num_scalar_prefetch=1, grid=(Bs, n_pages // pp),
            in_specs=_page_specs((1,) + cache.shape[1:], n_pages, pp) + [full(w1bd), full(bias), full(w2)],
            out_specs=pl.BlockSpec((1, NSA_G, rows, LANES), lambda b, c, pt: (b, 0, c, 0)),
            scratch_shapes=[pltpu.VMEM((2, 8, 2 * CMP_HID), F32), pltpu.VMEM((2, pp * page, LANES), F32)]),
        out_shape=jax.ShapeDtypeStruct((Bs, NSA_G, n_slots, LANES), BF16),
        compiler_params=_cp("parallel", "arbitrary"),
        name="compress_sample",
    )(pt_flat, *([cache] * pp), w1bd, bias, w2)


def _page_cumsum_kernel(x_ref, o_ref):
    o_ref[...] = _cumsum_block(x_ref[...])


def _page_cumsum(logf_cache_t):
    n_phys, h, page = logf_cache_t.shape
    rows = n_phys * h
    tr = next(t for t in (2048, 1024, 512, 256, 128, 64, 32, 16, 8) if rows % t == 0)
    out = pl.pallas_call(
        _page_cumsum_kernel,
        grid=(rows // tr,),
        in_specs=[pl.BlockSpec((tr, page), lambda i: (i, 0))],
        out_specs=pl.BlockSpec((tr, page), lambda i: (i, 0)),
        out_shape=jax.ShapeDtypeStruct((rows, page), F32),
        compiler_params=_cp("parallel"),
        name="page_cumsum",
    )(logf_cache_t.reshape(rows, page))
    return out.reshape(n_phys, h, page)


ROWS_S = NSA_G * HG * 8


SEL_SEQS = 8


def _sel_sample_kernel(q_ref, cmp_ref, c2s_ref, slope_ref, qidx_ref, oc_ref, sel_ref, *, past, n_take):
    n_slots = cmp_ref.shape[2]
    rg = ROWS_S // NSA_G
    slot = _iota((1, n_slots), 1)
    cend = slot * CMP_STRIDE + (CMP_STRIDE - 1)
    c2s = c2s_ref[...]
    blk = _iota((8, LANES), 1)
    n_blk = past // SLC_BLOCK
    forced = (blk == 0) | (blk == n_blk - 1)
    for sq in range(q_ref.shape[0]):
        for g in range(NSA_G):
            r = slice(g * rg, (g + 1) * rg)
            cmp = cmp_ref[sq, g]
            d_c = past + qidx_ref[r] - cend
            mask = (d_c >= 0) & (slot >= 1)
            s = _dot_nt(q_ref[sq, r], cmp) - slope_ref[r] * d_c.astype(F32)
            pb = _masked_softmax(s, mask).astype(BF16)
            oc_ref[sq, r] = pltpu.roll(_dot(pb, cmp), HD, 1)
            imp_h = _dot(pb, c2s)
            imp = imp_h[0:8] + imp_h[8:16] + imp_h[16:24] + imp_h[24:32]
            val = jnp.where(blk < n_blk, imp + jnp.where(forced, FORCE, 0.0), NEG)
            sel8 = _topk_mask(val, n_take, 1)
            sel_ref[sq, r] = jnp.concatenate([sel8] * HG, axis=0).astype(sel_ref.dtype)


def _select_sample(qs, cmp_s, slope_rows, qidx_rows, Bs, past):
    n_slots = cmp_s.shape[2]
    n_slc = past // SLC_BLOCK + 1
    assert past % SLC_BLOCK == 0 and n_slc - 1 <= LANES and HG == 4
    nb = SEL_SEQS if Bs % SEL_SEQS == 0 else 1
    c2s = jnp.asarray(_cmp_to_slc_np(n_slots, -1), BF16)
    full = lambda a: pl.BlockSpec(a.shape, lambda b: (0,) * a.ndim)
    return pl.pallas_call(
        functools.partial(_sel_sample_kernel, past=past, n_take=min(SLC_TOPK, n_slc) - 1),
        grid=(Bs // nb,),
        in_specs=[pl.BlockSpec((nb, ROWS_S, LANES), lambda b: (b, 0, 0)),
                  pl.BlockSpec((nb, NSA_G, n_slots, LANES), lambda b: (b, 0, 0, 0)),
                  full(c2s), full(slope_rows), full(qidx_rows)],
        out_specs=[pl.BlockSpec((nb, ROWS_S, LANES), lambda b: (b, 0, 0)),
                   pl.BlockSpec((nb, ROWS_S, LANES), lambda b: (b, 0, 0))],
        out_shape=[jax.ShapeDtypeStruct((Bs, ROWS_S, LANES), F32),
                   jax.ShapeDtypeStruct((Bs, ROWS_S, LANES), BF16)],
        compiler_params=_cp("parallel"),
        name="select_sample",
    )(qs, cmp_s, c2s, slope_rows, qidx_rows)


def _online_update(s, ok, v, m_s, l_s, acc_s, base2=False, v_t=False):
    ex = jnp.exp2 if base2 else jnp.exp
    s = jnp.where(ok, s, NEG)
    m_old = m_s[...]
    m_new = jnp.maximum(m_old, jnp.max(s, axis=1, keepdims=True))
    alpha = ex(m_old - m_new)
    e = jnp.where(ok, ex(s - m_new), 0.0)
    l_s[...] = alpha * l_s[...] + jnp.sum(e, axis=1, keepdims=True)
    acc_s[...] = alpha * acc_s[...] + (_dot_nt if v_t else _dot)(e.astype(BF16), v)
    m_s[...] = m_new


def _pages_t(pages, kv, width):
    return jnp.concatenate([p[0, kv].reshape(width, p.shape[-1]).astype(BF16) for p in pages], axis=1)


def _pad_rows8(x):
    return jnp.concatenate([x, jnp.zeros((8 - x.shape[0], x.shape[1]), x.dtype)], axis=0)


def _slc_sample_kernel(pt_ref, *refs, pp, past, t_new):
    pages = refs[:pp]
    q_ref, sel_ref, new_ref, slope_ref, qidx_ref, o_ref, m_s, l_s, acc_s = refs[pp:]
    c = pl.program_id(1)
    page = pages[0].shape[-1]
    nk = pp * page

    @pl.when(c == 0)
    def _():
        m_s[...] = jnp.full(m_s.shape, NEG, F32)
        l_s[...] = jnp.zeros(l_s.shape, F32)
        acc_s[...] = jnp.zeros(acc_s.shape, F32)

    q = q_ref[0]
    slope = slope_ref[...]
    qidx = qidx_ref[...]
    sel = sel_ref[0]
    blk_lane = _iota(sel.shape, 1)
    b0 = c * (nk // SLC_BLOCK)
    in_chunk = (blk_lane >= b0) & (blk_lane < b0 + nk // SLC_BLOCK)

    @pl.when(jnp.max(jnp.where(in_chunk, sel.astype(F32), 0.0)) > 0.0)
    def _():
        kpos = c * nk + _iota((1, nk), 1)
        expand = _onehot(_iota((LANES, 1), 0) == (kpos >> 6))
        ok = _dot(sel, expand) > 0.5
        dist = (past + qidx - kpos).astype(F32)
        s = _dot(q, _pages_t(pages, 0, NSA_G * HD)) - slope * dist
        _online_update(s, ok, _pages_t(pages, 1, NSA_G * HD), m_s, l_s, acc_s, v_t=True)

    @pl.when(c == pl.num_programs(1) - 1)
    def _():
        new = _pad_rows8(new_ref[0])
        kj = _iota((1, 8), 1)
        ok_new = (kj <= qidx) & (kj < t_new)
        s_new = _dot_nt(q, new[:, :LANES].astype(BF16)) - slope * (qidx - kj).astype(F32)
        _online_update(s_new, ok_new, new[:, LANES:].astype(BF16), m_s, l_s, acc_s)
        o_ref[0] = acc_s[...] / l_s[...]


def _slc_sample(cache, pt_flat, qs, sel_rows, kvs_new, slope_rows, qidx_rows, Bs, n_pages, past):
    pp = min(PAGES_PER_STEP, n_pages)
    t_new = kvs_new.shape[1]
    full = lambda a: pl.BlockSpec(a.shape, lambda b, c, pt: (0,) * a.ndim)
    per_seq = lambda a: pl.BlockSpec((1,) + a.shape[1:], lambda b, c, pt: (b,) + (0,) * (a.ndim - 1))
    return pl.pallas_call(
        functools.partial(_slc_sample_kernel, pp=pp, past=past, t_new=t_new),
        grid_spec=pltpu.PrefetchScalarGridSpec(
            num_scalar_prefetch=1, grid=(Bs, n_pages // pp),
            in_specs=_page_specs((1,) + cache.shape[1:], n_pages, pp)
            + [per_seq(qs), per_seq(sel_rows), per_seq(kvs_new), full(slope_rows), full(qidx_rows)],
            out_specs=pl.BlockSpec((1, ROWS_S, LANES), lambda b, c, pt: (b, 0, 0)),
            scratch_shapes=[pltpu.VMEM((ROWS_S, 1), F32), pltpu.VMEM((ROWS_S, 1), F32),
                            pltpu.VMEM((ROWS_S, LANES), F32)]),
        out_shape=jax.ShapeDtypeStruct((Bs, ROWS_S, LANES), F32),
        compiler_params=_cp("parallel", "arbitrary"),
        name="selected_sample",
    )(pt_flat, *([cache] * pp), qs, sel_rows, kvs_new, slope_rows, qidx_rows)


def _win_sample_kernel(q_ref, win_ref, new_ref, oc_ref, os_ref, gates_ref, slope_ref, qidx_ref, o_ref, *, t_new):
    q = q_ref[0]
    n_win = win_ref.shape[-1]
    k_t = win_ref[0, 0].reshape(NSA_G * HD, n_win).astype(BF16)
    v_t = win_ref[0, 1].reshape(NSA_G * HD, n_win).astype(BF16)
    new = _pad_rows8(new_ref[0])
    kj = _iota((1, n_win + 8), 1)
    d_w = n_win + qidx_ref[...] - kj
    mask = (d_w >= 0) & (d_w < WINDOW) & (kj < n_win + t_new)
    s = jnp.concatenate([_dot(q, k_t), _dot_nt(q, new[:, :LANES].astype(BF16))], axis=1)
    p = _masked_softmax(s - slope_ref[...] * d_w.astype(F32), mask).astype(BF16)
    o_w = _dot_nt(p[:, :n_win], v_t) + _dot(p[:, n_win:], new[:, LANES:].astype(BF16))
    gates = gates_ref[0]
    o_ref[0] = gates[:, 0:1] * oc_ref[0] + gates[:, 1:2] * os_ref[0] + gates[:, 2:3] * o_w


def _win_sample(qs, state_win, kvw_new, oc, o_s, gate_rows, slope_rows, qidx_rows, Bs):
    t_new = kvw_new.shape[1]
    full = lambda a: pl.BlockSpec(a.shape, lambda b: (0,) * a.ndim)
    per_seq = lambda a: pl.BlockSpec((1,) + a.shape[1:], lambda b: (b,) + (0,) * (a.ndim - 1))
    return pl.pallas_call(
        functools.partial(_win_sample_kernel, t_new=t_new),
        grid=(Bs,),
        in_specs=[per_seq(qs), per_seq(state_win), per_seq(kvw_new), per_seq(oc), per_seq(o_s),
                  per_seq(gate_rows), full(slope_rows), full(qidx_rows)],
        out_specs=pl.BlockSpec((1, ROWS_S, LANES), lambda b: (b, 0, 0)),
        out_shape=jax.ShapeDtypeStruct((Bs, ROWS_S, LANES), F32),
        compiler_params=_cp("parallel"),
        name="window_sample",
    )(qs, state_win, kvw_new, oc, o_s, gate_rows, slope_rows, qidx_rows)


ROWS_F = FOX_H * 8
FOX_K = FOX_H * HD


def _fox_sample_kernel(pt_ref, *refs, pp, t_new):
    pages, pcs = refs[:pp], refs[pp:2 * pp]
    q_ref, new_ref, csnew_ref, qidx_ref, o_ref, m_s, l_s, acc_s, carry_s = refs[2 * pp:]
    c = pl.program_id(1)

    @pl.when(c == 0)
    def _():
        m_s[...] = jnp.full(m_s.shape, NEG, F32)
        l_s[...] = jnp.zeros(l_s.shape, F32)
        acc_s[...] = jnp.zeros(acc_s.shape, F32)
        carry_s[...] = jnp.zeros(carry_s.shape, F32)

    q = q_ref[0]
    carry = carry_s[...]
    chunks = []
    for pc in pcs:
        x = pc[0]
        chunks.append(x + carry)
        carry = carry + jnp.broadcast_to(x[:, x.shape[1] - 1:], carry.shape)
    carry_s[...] = carry
    d = jnp.concatenate(chunks, axis=1)
    nk = d.shape[1]
    bias = jnp.concatenate([jnp.broadcast_to(d[h:h + 1], (8, nk)) for h in range(FOX_H)], axis=0)
    s = _dot(q, _pages_t(pages, 0, FOX_K)) - bias * LOG2E
    _online_update(s, jnp.full((ROWS_F, nk), True), _pages_t(pages, 1, FOX_K), m_s, l_s, acc_s,
                   base2=True, v_t=True)

    @pl.when(c == pl.num_programs(1) - 1)
    def _():
        new = _pad_rows8(new_ref[0])
        kj = _iota((1, 8), 1)
        ok_new = (kj <= qidx_ref[...]) & (kj < t_new)
        total = jnp.concatenate([jnp.broadcast_to(carry[h:h + 1, 0:1], (8, 1)) for h in range(FOX_H)], axis=0)
        s_new = _dot_nt(q, new[:, :FOX_K].astype(BF16)) - (total + csnew_ref[0]) * LOG2E
        _online_update(s_new, ok_new, new[:, FOX_K:].astype(BF16), m_s, l_s, acc_s, base2=True)
        o_ref[0] = acc_s[...] / l_s[...]


def _fox_sample(cache, page_cs, pt_flat, qf_bd, kvf_new, csnew_rows, qidx_f, Bs, n_pages):
    pp = min(2 * PAGES_PER_STEP, n_pages)
    page = cache.shape[-1]
    t_new = kvf_new.shape[1]
    full = lambda a: pl.BlockSpec(a.shape, lambda b, c, pt: (0,) * a.ndim)
    per_seq = lambda a: pl.BlockSpec((1,) + a.shape[1:], lambda b, c, pt: (b,) + (0,) * (a.ndim - 1))
    return pl.pallas_call(
        functools.partial(_fox_sample_kernel, pp=pp, t_new=t_new),
        grid_spec=pltpu.PrefetchScalarGridSpec(
            num_scalar_prefetch=1, grid=(Bs, n_pages // pp),
            in_specs=_page_specs((1,) + cache.shape[1:], n_pages, pp)
            + _page_specs((1,) + page_cs.shape[1:], n_pages, pp)
            + [per_seq(qf_bd), per_seq(kvf_new), per_seq(csnew_rows), full(qidx_f)],
            out_specs=pl.BlockSpec((1, ROWS_F, FOX_K), lambda b, c, pt: (b, 0, 0)),
            scratch_shapes=[pltpu.VMEM((ROWS_F, 1), F32), pltpu.VMEM((ROWS_F, 1), F32),
                            pltpu.VMEM((ROWS_F, FOX_K), F32), pltpu.VMEM((FOX_H, LANES), F32)]),
        out_shape=jax.ShapeDtypeStruct((Bs, ROWS_F, FOX_K), F32),
        compiler_params=_cp("parallel", "arbitrary"),
        name="fox_sample",
    )(pt_flat, *([cache] * pp), *([page_cs] * pp), qf_bd, kvf_new, csnew_rows, qidx_f)


def _alibi_slopes():
    return jnp.exp2(-8.0 * jnp.arange(1, NSA_H + 1, dtype=F32) / NSA_H)


def _ffn_tail(o_n, o_f, x2, mod, norms, lw, rows_per_mod):
    gate1, scale2, shift2, gate2 = mod
    g_post_mix, g_pre_ffn, g_post_ffn = norms
    x1, h2, gd = _out_proj(o_n, o_f, x2, gate1, scale2, shift2, g_post_mix, g_pre_ffn,
                           lw["wo"], lw["wr"], lw["br"], rows_per_mod)
    routed = _moe_routed(h2, gd, lw["weg"], lw["weu"], lw["wed"])
    return _final(routed, h2, x1, gate2, g_post_ffn, lw["wsg"], lw["wsu"], lw["wsd"], rows_per_mod)


def _prompt_layer(x, mod, norms, lw):
    B, T, d = x.shape
    n = B * T
    shift1, scale1, gate1, shift2, scale2, gate2 = mod
    g_pre_mix, g_post_mix, g_pre_ffn, g_post_ffn = norms
    x2 = x.reshape(n, d)
    pr = _in_proj(x2, scale1, shift1, g_pre_mix, lw["w_in"], lw["b_small"], T, "prompt")
    small = pr["small"]
    logf = small[:, N_GATE:N_GATE + FOX_H]
    gates_g = small[:, :N_GATE].reshape(n, NSA_G, HG * 3).transpose(1, 0, 2)
    gates_g = jnp.pad(gates_g, ((0, 0), (0, 0), (0, LANES - HG * 3)))
    slopes = _alibi_slopes()

    cmp = _compress_prompt(pr["kvc"], B, T, lw["cmp_w"])
    oc, selneg = _nsa_select(slopes, pr["qn"], cmp, B, T)
    o_n = _nsa_attend(slopes, pr["qn"], pr, selneg, oc, gates_g, B, T)

    d_t = _cumsum_prompt(logf.reshape(B, T, FOX_H).transpose(0, 2, 1))
    o_f = _fox_attend(pr["qf"], pr["kf_op"], pr["vf_op"], d_t, B, T)

    y = _ffn_tail(o_n, o_f, x2, (gate1, scale2, shift2, gate2), (g_post_mix, g_pre_ffn, g_post_ffn), lw, T)
    win_rows = min(WINDOW, T)
    state = lambda a, heads: jnp.moveaxis(a.reshape(B, 2, heads, HD, -1), -1, 1)
    states = (state(pr["kvc_t"], NSA_G), state(pr["kvs_t"], NSA_G), state(pr["kvf_t"], FOX_H),
              logf.reshape(B, T, FOX_H), state(pr["kvw_t"][:, :, T - win_rows:], NSA_G))
    return y.reshape(B, T, d), states


def _sample_layer(x, mod, norms, lw, caches):
    Bs, Tn, d = x.shape
    n = Bs * Tn
    assert Tn <= 8
    cache_cmp, cache_slc, cache_fox, cache_logf, state_win, page_table = caches
    n_pages = page_table.shape[1]
    page = cache_cmp.shape[1]
    past = n_pages * page
    shift1, scale1, gate1, shift2, scale2, gate2 = mod
    g_pre_mix, g_post_mix, g_pre_ffn, g_post_ffn = norms
    x2 = x.reshape(n, d)
    pr = _in_proj(x2, scale1, shift1, g_pre_mix, lw["w_in"], lw["b_small"], Tn, "sample")
    small = pr["small"]
    logf = small[:, N_GATE:N_GATE + FOX_H].reshape(Bs, Tn, FOX_H)
    pt_flat = page_table.reshape(-1)
    padq = ((0, 0), (0, 0), (0, 0), (0, 8 - Tn), (0, 0))

    qn = pr["qn"].reshape(Bs, Tn, NSA_G, HG, HD).transpose(0, 2, 3, 1, 4)
    qn = jnp.pad(qn, padq)
    z = jnp.zeros_like(qn[:, 0])
    qs = jnp.stack([jnp.concatenate([qn[:, 0], z], -1), jnp.concatenate([z, qn[:, 1]], -1)], axis=1)
    qs = qs.reshape(Bs, ROWS_S, LANES)
    slope_rows = jnp.broadcast_to(_alibi_slopes().reshape(NSA_H, 1, 1), (NSA_H, 8, 1)).reshape(ROWS_S, 1)
    qidx_rows = jnp.broadcast_to(jnp.arange(8, dtype=jnp.int32).reshape(1, 8, 1), (NSA_H, 8, 1)).reshape(ROWS_S, 1)
    gate_rows = small[:, :N_GATE].reshape(Bs, Tn, NSA_H, 3).transpose(0, 2, 1, 3)
    gate_rows = jnp.pad(gate_rows, ((0, 0), (0, 0), (0, 8 - Tn), (0, LANES - 3))).reshape(Bs, ROWS_S, LANES)

    tok_minor = lambda a: jnp.moveaxis(a, 1, -1)
    cmp_s = _compress_sample(tok_minor(cache_cmp), pt_flat, Bs, n_pages, lw["cmp_sw"])
    oc, sel_rows = _select_sample(qs, cmp_s, slope_rows, qidx_rows, Bs, past)
    o_s = _slc_sample(tok_minor(cache_slc), pt_flat, qs, sel_rows, pr["kvs"].reshape(Bs, Tn, KVW),
                      slope_rows, qidx_rows, Bs, n_pages, past)
    win = tok_minor(state_win)
    o_rows = _win_sample(qs, win, pr["kvw"].reshape(Bs, Tn, KVW), oc, o_s, gate_rows, slope_rows, qidx_rows, Bs)
    o_rows = o_rows.reshape(Bs, NSA_G, HG, 8, NSA_G, HD)[:, :, :, :Tn]
    o_n = jnp.stack([o_rows[:, 0, :, :, 0], o_rows[:, 1, :, :, 1]], axis=1)
    o_n = o_n.transpose(0, 3, 1, 2, 4).reshape(n, QW).astype(BF16)

    qf = jnp.pad(pr["qf"].reshape(Bs, Tn, FOX_H, HD).transpose(0, 2, 1, 3), ((0, 0), (0, 0), (0, 8 - Tn), (0, 0)))
    qf_bd = (qf[:, :, :, None, :] * jnp.eye(FOX_H, dtype=BF16)[None, :, None, :, None]).reshape(Bs, ROWS_F, FOX_K)
    page_cs = _page_cumsum(cache_logf.transpose(0, 2, 1))
    cs_new = jnp.cumsum(logf, axis=1).transpose(0, 2, 1)
    csnew_rows = jnp.broadcast_to(jnp.pad(cs_new, ((0, 0), (0, 0), (0, 8 - Tn)))[:, :, None, :],
                                 (Bs, FOX_H, 8, 8)).reshape(Bs, ROWS_F, 8)
    qidx_f = jnp.broadcast_to(jnp.arange(8, dtype=jnp.int32).reshape(1, 8, 1), (FOX_H, 8, 1)).reshape(ROWS_F, 1)
    o_frows = _fox_sample(tok_minor(cache_fox), page_cs, pt_flat, qf_bd,
                          pr["kvf"].reshape(Bs, Tn, KVF), csnew_rows, qidx_f, Bs, n_pages)
    o_frows = o_frows.reshape(Bs, FOX_H, 8, FOX_H, HD)[:, :, :Tn]
    o_f = jnp.stack([o_frows[:, h, :, h] for h in range(FOX_H)], axis=2)
    o_f = o_f.reshape(n, FOX_H * HD).astype(BF16)

    y = _ffn_tail(o_n, o_f, x2, (gate1, scale2, shift2, gate2), (g_post_mix, g_pre_ffn, g_post_ffn), lw, Tn)
    kvw_new = pr["kvw"].reshape(Bs, Tn, 2, NSA_G, HD)
    states = (pr["kvc"].reshape(Bs, Tn, 2, NSA_G, HD), pr["kvs"].reshape(Bs, Tn, 2, NSA_G, HD),
              pr["kvf"].reshape(Bs, Tn, 2, FOX_H, HD), logf,
              jnp.concatenate([state_win, kvw_new], axis=1)[:, Tn:])
    return y.reshape(Bs, Tn, d), states


def _layer_weights(w_in, b_nsa_gate, b_forget, w_cmp1, b_cmp1, w_cmp2, pe_cmp, w_out, w_router, b_router,
                   w_exp_gate, w_exp_up, w_exp_down, w_sh_gate, w_sh_up, w_sh_down):
    w_in_b, b_small = _in_weights(w_in, b_nsa_gate, b_forget)
    n_e = w_router.shape[1]
    wr = jnp.pad(w_router.T, ((0, LANES - n_e), (0, 0))).astype(BF16)
    br = jnp.concatenate([b_router.astype(F32), jnp.full((LANES - n_e,), NEG_INF, F32)]).reshape(LANES, 1)
    return dict(w_in=w_in_b, b_small=b_small,
                cmp_w=_cmp_weights(w_cmp1, b_cmp1, w_cmp2, pe_cmp),
                cmp_sw=_cmp_sample_weights(w_cmp1, b_cmp1, w_cmp2, pe_cmp),
                wo=w_out.astype(BF16), wr=wr, br=br,
                weg=w_exp_gate.astype(BF16), weu=w_exp_up.astype(BF16), wed=w_exp_down.astype(BF16),
                wsg=w_sh_gate.astype(BF16), wsu=w_sh_up.astype(BF16), wsd=w_sh_down.astype(BF16))


def kernel(x_prompt, x_sample, c_prompt, c_sample, cache_cmp_kv, cache_slc_kv, cache_fox_kv, cache_fox_logf,
           state_win_kv, page_table, w_ada, b_ada, g_pre_mix, g_post_mix, g_pre_ffn, g_post_ffn, w_in,
           b_nsa_gate, b_forget, w_cmp1, b_cmp1, w_cmp2, pe_cmp, w_out, w_router, b_router, w_exp_gate,
           w_exp_up, w_exp_down, w_sh_gate, w_sh_up, w_sh_down):
    depth = w_in.shape[0]
    Bp = x_prompt.shape[0]
    xp, xs = x_prompt, x_sample
    st_p = [[] for _ in range(5)]
    st_s = [[] for _ in range(5)]
    for l in range(depth):
        lw = _layer_weights(w_in[l], b_nsa_gate[l], b_forget[l], w_cmp1[l], b_cmp1[l], w_cmp2[l], pe_cmp[l],
                            w_out[l], w_router[l], b_router[l], w_exp_gate[l], w_exp_up[l], w_exp_down[l],
                            w_sh_gate[l], w_sh_up[l], w_sh_down[l])
        m = _modulation(jnp.concatenate([c_prompt, c_sample], axis=0), w_ada[l], b_ada[l])
        mods = jnp.split(m[:, None, :], 6, axis=-1)
        norms = (g_pre_mix[l], g_post_mix[l], g_pre_ffn[l], g_post_ffn[l])
        xp, sp = _prompt_layer(xp, [a[:Bp] for a in mods], norms, lw)
        caches = (cache_cmp_kv[l], cache_slc_kv[l], cache_fox_kv[l], cache_fox_logf[l], state_win_kv[l], page_table)
        xs, ss = _sample_layer(xs, [a[Bp:] for a in mods], norms, lw, caches)
        for lst, a in zip(st_p, sp):
            lst.append(a)
        for lst, a in zip(st_s, ss):
            lst.append(a)
    return (xp, xs, *[jnp.stack(s) for s in st_p], *[jnp.stack(s) for s in st_s])
```
